```python
import math
import jax, jax.numpy as jnp
from jax import lax
import numpy as np

D_MODEL = 1024
BATCH = 8
SEQ = 2048
DEPTH = 2

CTX_LEN = 256
GRID_W = 64
D_MIX = D_MODEL
EPS = 1e-6

A_HEADS = 4
A_DK = 128
A_DV = 128
A_QK_WIDTH = A_HEADS * A_DK
A_WIDTH = A_HEADS * A_DV
A_CONV = 5
A_CHUNK = 64

B_HEADS = 4
B_NOPE = 64
B_ROPE = 32
B_V = 64
B_Q_LORA = 192
B_KV_LORA = 128
B_WIDTH = B_HEADS * B_V
B_QBLOCK = 128
ROPE_THETA = 10000.0

C_WIDTH = D_MIX - A_WIDTH - B_WIDTH
C_CONV = 3

IN_SIZES = (A_QK_WIDTH, A_QK_WIDTH, A_WIDTH, A_WIDTH, 2 * A_HEADS, 2 * A_HEADS,
            B_Q_LORA, B_KV_LORA + B_ROPE, B_WIDTH,
            C_WIDTH, C_WIDTH, C_WIDTH, C_WIDTH)
N_IN = sum(IN_SIZES)

kernel_name = "hybrid_parallel_heads_dit_block"

F32 = jnp.float32


def rmsnorm(x, g):
    xf = x.astype(F32)
    y = xf * lax.rsqrt(jnp.mean(xf * xf, axis=-1, keepdims=True) + EPS)
    return (y * g.astype(F32)).astype(x.dtype)


def l2norm(x):
    return x * lax.rsqrt(jnp.sum(x * x, axis=-1, keepdims=True) + EPS)


def dwconv_centred(x, w):
    pad = w.shape[0] // 2
    return lax.conv_general_dilated(
        x, w[:, None, :].astype(x.dtype), window_strides=(1,), padding=[(pad, pad)],
        dimension_numbers=("NWC", "WIO", "NWC"), feature_group_count=x.shape[-1])


def split_in(u):
    idx = [int(i) for i in np.cumsum(IN_SIZES)[:-1]]
    return jnp.split(u, idx, axis=-1)


def gdn_prepare(q, k, v, b_raw, a_raw, conv_w, a_log, dt_bias):
    bn, t, _ = q.shape
    qkv = jax.nn.silu(dwconv_centred(jnp.concatenate([q, k, v], axis=-1), conv_w)).astype(F32)
    q, k, v = jnp.split(qkv, [A_QK_WIDTH, 2 * A_QK_WIDTH], axis=-1)
    q = l2norm(q.reshape(bn, t, A_HEADS, A_DK)) * (A_DK ** -0.5)
    k = l2norm(k.reshape(bn, t, A_HEADS, A_DK))
    v = v.reshape(bn, t, A_HEADS, A_DV)
    beta = jax.nn.sigmoid(b_raw.astype(F32).reshape(bn, t, 2, A_HEADS))
    g = -jnp.exp(a_log.astype(F32)) * jax.nn.softplus(
        a_raw.astype(F32).reshape(bn, t, 2, A_HEADS) + dt_bias.astype(F32))
    return q, k, v, g, beta


def gated_delta_chunked(q, k, v, g, beta, s0):
    bn, t, h, _ = q.shape
    n = t // A_CHUNK

    def to_chunks(a):
        a = jnp.moveaxis(a, 2, 1)
        return a.reshape(bn, h, n, A_CHUNK, *a.shape[3:])

    q, k, v, g, beta = (to_chunks(a) for a in (q, k, v, g, beta))
    gc = jnp.cumsum(g, axis=-1)
    causal = jnp.tril(jnp.ones((A_CHUNK, A_CHUNK), bool))
    strict = jnp.tril(jnp.ones((A_CHUNK, A_CHUNK), F32), -1)
    diff = gc[..., :, None] - gc[..., None, :]
    decay = jnp.where(causal, jnp.exp(jnp.where(causal, diff, 0.0)), 0.0)
    kb = k * beta[..., None]
    lower = jnp.einsum("bhnik,bhnjk->bhnij", kb, k) * decay * strict
    a_mat = jnp.eye(A_CHUNK, dtype=F32) + lower
    u = lax.linalg.triangular_solve(a_mat, v * beta[..., None], left_side=True, lower=True, unit_diagonal=True)
    w = lax.linalg.triangular_solve(a_mat, kb * jnp.exp(gc)[..., None], left_side=True, lower=True,
                                    unit_diagonal=True)
    qk = jnp.einsum("bhnik,bhnjk->bhnij", q, k) * decay
    q_dec = q * jnp.exp(gc)[..., None]
    k_dec = k * jnp.exp(gc[..., -1:] - gc)[..., None]
    g_tot = jnp.exp(gc[..., -1])

    def step(s, xs):
        w_n, u_n, qk_n, qd_n, kd_n, gt_n = xs
        v_new = u_n - jnp.einsum("bhck,bhkv->bhcv", w_n, s)
        o_n = jnp.einsum("bhck,bhkv->bhcv", qd_n, s) + jnp.einsum("bhij,bhjv->bhiv", qk_n, v_new)
        s = s * gt_n[..., None, None] + jnp.einsum("bhck,bhcv->bhkv", kd_n, v_new)
        return s, o_n

    xs = tuple(jnp.moveaxis(a, 2, 0) for a in (w, u, qk, q_dec, k_dec, g_tot))
    s_fin, o = lax.scan(step, s0, xs)
    o = jnp.transpose(o, (1, 0, 3, 2, 4)).reshape(bn, t, h, A_DV)
    return o, s_fin


def gdn_bidir(q, k, v, g, beta, s0_f, s0_b):
    o_f, s_f = gated_delta_chunked(q, k, v, g[:, :, 0], beta[:, :, 0], s0_f)
    rev = lambda a: jnp.flip(a, axis=1)
    o_b, s_b = gated_delta_chunked(rev(q), rev(k), rev(v), rev(g[:, :, 1]), rev(beta[:, :, 1]), s0_b)
    return o_f + rev(o_b), s_f, s_b


def gdn_output(o, z, norm_g):
    bn, t = z.shape[:2]
    o = rmsnorm(o, norm_g).reshape(bn, t, A_WIDTH).astype(z.dtype)
    return o * jax.nn.silu(z)


def axial_rope_tables(rows):
    r = jnp.repeat(jnp.arange(rows), GRID_W)
    col = jnp.tile(jnp.arange(GRID_W), rows)
    n_freq = B_ROPE // 4
    inv_freq = ROPE_THETA ** (-jnp.arange(n_freq, dtype=F32) / n_freq)
    ang = jnp.concatenate([r[:, None] * inv_freq, col[:, None] * inv_freq], axis=-1)
    return jnp.cos(ang)[:, None, :], jnp.sin(ang)[:, None, :]


def rope_2d(x, cos, sin):
    x2 = x.astype(F32).reshape(*x.shape[:-1], B_ROPE // 2, 2)
    x0, x1 = x2[..., 0], x2[..., 1]
    out = jnp.stack([x0 * cos - x1 * sin, x0 * sin + x1 * cos], axis=-1)
    return out.reshape(x.shape).astype(x.dtype)


def mla_queries(q_a, norm_g, w_qb, cos, sin):
    bn, t, _ = q_a.shape
    q = (rmsnorm(q_a, norm_g) @ w_qb).reshape(bn, t, B_HEADS, B_NOPE + B_ROPE)
    if cos is not None:
        q = jnp.concatenate([q[..., :B_NOPE], rope_2d(q[..., B_NOPE:], cos, sin)], axis=-1)
    return q


def mla_keys_values(kv_a, norm_g, w_kvb, cos, sin):
    bn, t, _ = kv_a.shape
    c_kv, k_pe = kv_a[..., :B_KV_LORA], kv_a[..., B_KV_LORA:]
    kv = (rmsnorm(c_kv, norm_g) @ w_kvb).reshape(bn, t, B_HEADS, B_NOPE + B_V)
    k_nope, v = kv[..., :B_NOPE], kv[..., B_NOPE:]
    k_pe = k_pe[:, :, None, :]
    if cos is not None:
        k_pe = rope_2d(k_pe, cos, sin)
    k = jnp.concatenate([k_nope, jnp.broadcast_to(k_pe, (bn, t, B_HEADS, B_ROPE))], axis=-1)
    return k, v


def softmax_attention(q, k, v):
    bn, t, h, dq = q.shape
    nb = t // B_QBLOCK
    scale = dq ** -0.5
    qb = jnp.moveaxis(q.reshape(bn, nb, B_QBLOCK, h, dq), 1, 0)

    def block(q_blk):
        s = jnp.einsum("bqhd,bkhd->bhqk", q_blk, k).astype(F32) * scale
        p = jax.nn.softmax(s, axis=-1).astype(v.dtype)
        return jnp.einsum("bhqk,bkhd->bqhd", p, v)

    o = lax.map(block, qb)
    return jnp.moveaxis(o, 0, 1).reshape(bn, t, h * v.shape[-1])


def conv_branch(h, b_gate, c_gate, z, conv_w):
    return b_gate * dwconv_centred(c_gate * h, conv_w) * jax.nn.silu(z)


def modulate(x, g, shift, scale):
    return rmsnorm(x, g) * (1.0 + scale) + shift


def hybrid_layer(x, ctx, mod_x, mod_c, norm_g, w_in, gdn_conv, gdn_a_log, gdn_dt_bias, gdn_norm_g,
                 mla_q_norm_g, mla_w_qb, mla_kv_norm_g, mla_w_kvb, conv_w, w_out, cos, sin, update_ctx):
    bn = x.shape[0]
    shift_x, scale_x, gate_x = jnp.split(mod_x, 3, axis=-1)
    shift_c, scale_c, gate_c = jnp.split(mod_c, 3, axis=-1)
    hx = modulate(x, norm_g, shift_x, scale_x)
    hc = modulate(ctx, norm_g, shift_c, scale_c)
    (xq, xk, xv, xz_a, xb, xa, xq_a, xkv_a, xz_b, xh, xbg, xcg, xz_c) = split_in(hx @ w_in)
    (cq, ck, cv, cz_a, cb, ca, cq_a, ckv_a, cz_b, ch, cbg, ccg, cz_c) = split_in(hc @ w_in)

    zeros = jnp.zeros((bn, A_HEADS, A_DK, A_DV), F32)
    o_ac, s_f, s_b = gdn_bidir(*gdn_prepare(cq, ck, cv, cb, ca, gdn_conv, gdn_a_log, gdn_dt_bias), zeros, zeros)
    o_ax, _, _ = gdn_bidir(*gdn_prepare(xq, xk, xv, xb, xa, gdn_conv, gdn_a_log, gdn_dt_bias), s_f, s_b)
    a_x = gdn_output(o_ax, xz_a, gdn_norm_g)

    k_c, v_c = mla_keys_values(ckv_a, mla_kv_norm_g, mla_w_kvb, None, None)
    k_x, v_x = mla_keys_values(xkv_a, mla_kv_norm_g, mla_w_kvb, cos, sin)
    q_x = mla_queries(xq_a, mla_q_norm_g, mla_w_qb, cos, sin)
    b_x = softmax_attention(q_x, jnp.concatenate([k_x, k_c], axis=1),
                            jnp.concatenate([v_x, v_c], axis=1)) * jax.nn.silu(xz_b)

    c_x = conv_branch(xh, xbg, xcg, xz_c, conv_w)

    x = x + gate_x * (jnp.concatenate([a_x, b_x, c_x], axis=-1) @ w_out)

    if update_ctx:
        a_c = gdn_output(o_ac, cz_a, gdn_norm_g)
        q_c = mla_queries(cq_a, mla_q_norm_g, mla_w_qb, None, None)
        b_c = softmax_attention(q_c, k_c, v_c) * jax.nn.silu(cz_b)
        c_c = conv_branch(ch, cbg, ccg, cz_c, conv_w)
        ctx = ctx + gate_c * (jnp.concatenate([a_c, b_c, c_c], axis=-1) @ w_out)
    return x, ctx


def setup_inputs(seed: int = 0) -> dict:
    key = jax.random.key(seed)
    ks = jax.random.split(key, 20)
    nrm = lambda k, shape, s: jax.random.normal(k, shape, F32) * s
    x = nrm(ks[0], (BATCH, SEQ, D_MODEL), 1.0)
    c = nrm(ks[1], (BATCH, D_MODEL), 1.0)
    ctx = nrm(ks[2], (BATCH, CTX_LEN, D_MODEL), 1.0)
    c_ctx = nrm(ks[3], (D_MODEL,), 1.0)
    w_ada = nrm(ks[4], (DEPTH, D_MODEL, 3 * D_MODEL), 0.5 * D_MODEL ** -0.5)
    b_ada = nrm(ks[5], (DEPTH, 3 * D_MODEL), 0.02)
    norm_g = 1.0 + nrm(ks[6], (DEPTH, D_MODEL), 0.1)
    w_in = nrm(ks[7], (DEPTH, D_MODEL, N_IN), D_MODEL ** -0.5)
    gdn_conv = nrm(ks[8], (DEPTH, A_CONV, 2 * A_QK_WIDTH + A_WIDTH), A_CONV ** -0.5)
    gdn_a_log = jnp.log(jax.random.uniform(ks[9], (DEPTH, 2, A_HEADS), F32, 1.0, 16.0))
    dt = jnp.exp(jax.random.uniform(ks[10], (DEPTH, 2, A_HEADS), F32, math.log(1e-3), math.log(1e-1)))
    gdn_dt_bias = dt + jnp.log(-jnp.expm1(-dt))
    gdn_norm_g = 1.0 + nrm(ks[11], (DEPTH, A_DV), 0.1)
    mla_q_norm_g = 1.0 + nrm(ks[12], (DEPTH, B_Q_LORA), 0.1)
    mla_w_qb = nrm(ks[13], (DEPTH, B_Q_LORA, B_HEADS * (B_NOPE + B_ROPE)), B_Q_LORA ** -0.5)
    mla_kv_norm_g = 1.0 + nrm(ks[14], (DEPTH, B_KV_LORA), 0.1)
    mla_w_kvb = nrm(ks[15], (DEPTH, B_KV_LORA, B_HEADS * (B_NOPE + B_V)), B_KV_LORA ** -0.5)
    conv_w = nrm(ks[16], (DEPTH, C_CONV, C_WIDTH), C_CONV ** -0.5)
    w_out = nrm(ks[17], (DEPTH, D_MIX, D_MODEL), D_MIX ** -0.5)
    final_norm_g = 1.0 + nrm(ks[18], (D_MODEL,), 0.1)
    return {"x": x, "c": c, "ctx": ctx, "c_ctx": c_ctx, "w_ada": w_ada, "b_ada": b_ada,
            "norm_g": norm_g, "w_in": w_in, "gdn_conv": gdn_conv, "gdn_a_log": gdn_a_log,
            "gdn_dt_bias": gdn_dt_bias, "gdn_norm_g": gdn_norm_g, "mla_q_norm_g": mla_q_norm_g,
            "mla_w_qb": mla_w_qb, "mla_kv_norm_g": mla_kv_norm_g, "mla_w_kvb": mla_w_kvb,
            "conv_w": conv_w, "w_out": w_out, "final_norm_g": final_norm_g}


def reference(x, c, ctx, c_ctx, w_ada, b_ada, norm_g, w_in, gdn_conv, gdn_a_log, gdn_dt_bias, gdn_norm_g,
              mla_q_norm_g, mla_w_qb, mla_kv_norm_g, mla_w_kvb, conv_w, w_out, final_norm_g):
    ROWS = x.shape[1] // GRID_W
    cos, sin = axial_rope_tables(ROWS)
    for l in range(DEPTH):
        mod_x = (jax.nn.silu(c) @ w_ada[l] + b_ada[l])[:, None, :]
        mod_c = jax.nn.silu(c_ctx) @ w_ada[l] + b_ada[l]
        x, ctx = hybrid_layer(x, ctx, mod_x, mod_c, norm_g[l], w_in[l], gdn_conv[l], gdn_a_log[l],
                              gdn_dt_bias[l], gdn_norm_g[l], mla_q_norm_g[l], mla_w_qb[l],
                              mla_kv_norm_g[l], mla_w_kvb[l], conv_w[l], w_out[l], cos, sin,
                              l < DEPTH - 1)
    return rmsnorm(x, final_norm_g)
```

```python
import functools
import math

import jax
import jax.numpy as jnp
import numpy as np
from jax.experimental import pallas as pl
from jax.experimental.pallas import tpu as pltpu

F32 = jnp.float32
BF16 = jnp.bfloat16

EPS = 1e-6
GRID_W = 64
ROPE_THETA = 10000.0

A_HEADS = 4
A_DK = 128
A_DV = 128
A_QK = A_HEADS * A_DK
A_W = A_HEADS * A_DV
A_CONV = 5
B_HEADS = 4
B_NOPE = 64
B_ROPE = 32
B_V = 64
B_Q_LORA = 192
B_KV_LORA = 128
B_W = B_HEADS * B_V
C_W = 256
C_CONV = 3

LANE = 128
HALO = 16
GDN_CHUNK = 128
VMEM_LIMIT = 48 * 1024 * 1024

S_QKV = 0
S_ZA = 1536
S_ZB = 2048
S_H = 2304
S_BG = 2560
S_CG = 2816
S_ZC = 3072
S_QA = 3328
S_CKV = 3584
S_KPM = 3712
S_KPS = 3840
NP_IN = 3968
CIN_W = 3 * A_QK + C_W


def _dot(a, b):
    return jnp.dot(a, b, preferred_element_type=F32)


def _dot_nt(a, b):
    return jax.lax.dot_general(a, b, (((1,), (1,)), ((), ())), preferred_element_type=F32)


def _silu(x):
    return x * jax.nn.sigmoid(x)


def _params(sem):
    return pltpu.CompilerParams(dimension_semantics=sem, vmem_limit_bytes=VMEM_LIMIT)


def _ada_kernel(c_ref, w_ref, b_ref, o_ref):
    s = _silu(c_ref[...]).astype(BF16)
    o_ref[0] = _dot(s, w_ref[0].astype(BF16)) + b_ref[0]


def _ada(cc, w_ada, b_ada):
    depth, d, n3 = w_ada.shape
    r = cc.shape[0]
    tn = 768
    return pl.pallas_call(
        _ada_kernel,
        grid=(depth, n3 // tn),
        in_specs=[pl.BlockSpec((r, d), lambda l, j: (0, 0)),
                  pl.BlockSpec((1, d, tn), lambda l, j: (l, 0, j)),
                  pl.BlockSpec((1, 1, tn), lambda l, j: (l, 0, j))],
        out_specs=pl.BlockSpec((1, r, tn), lambda l, j: (l, 0, j)),
        out_shape=jax.ShapeDtypeStruct((depth, r, n3), F32),
        compiler_params=_params(("parallel", "parallel")),
        name="ada",
    )(cc, w_ada, b_ada.reshape(depth, 1, n3))


def _inproj_kernel(x_ref, mod_ref, ng_ref, w_ref, qng_ref, wqm_ref, wqs_ref, kvng_ref, wk_ref, wv_ref,
                   tab_ref, gsc_ref,
                   cin_ref, sza_ref, gates_ref, qm_ref, km_ref, vm_ref, szb_ref, zc_ref):
    x = x_ref[0]
    y = x * jax.lax.rsqrt(jnp.mean(x * x, axis=-1, keepdims=True) + EPS) * ng_ref[...]
    hn = (y * (1.0 + mod_ref[0, 1:2, :]) + mod_ref[0, 0:1, :]).astype(BF16)

    def seg(a, b):
        return _dot(hn, w_ref[:, a:b])

    cin_ref[0, :, 0:3 * A_QK] = seg(S_QKV, S_ZA).astype(BF16)
    sza_ref[0] = _silu(seg(S_ZA, S_ZB)).astype(BF16)
    szb_ref[0] = _silu(seg(S_ZB, S_H)).astype(BF16)
    cin_ref[0, :, 3 * A_QK:CIN_W] = (seg(S_CG, S_ZC) * seg(S_H, S_BG)).astype(BF16)
    zc_ref[0] = (seg(S_BG, S_CG) * _silu(seg(S_ZC, S_QA))).astype(BF16)

    tab = tab_ref[...]
    cq, sq = tab[:, 0:LANE], tab[:, LANE:2 * LANE]
    ck, sk = tab[:, 2 * LANE:3 * LANE], tab[:, 3 * LANE:4 * LANE]

    qa = seg(S_QA, S_QA + B_Q_LORA)
    nq = (qa * jax.lax.rsqrt(jnp.mean(qa * qa, axis=-1, keepdims=True) + EPS) * qng_ref[...]).astype(BF16)
    qmain = _dot(nq, wqm_ref[...])
    qswap = _dot(nq, wqs_ref[...])
    for h in range(B_HEADS):
        sl = slice(h * LANE, (h + 1) * LANE)
        qm_ref[0, :, sl] = (qmain[:, sl] * cq + qswap[:, sl] * sq).astype(BF16)

    ckv = seg(S_CKV, S_KPM)
    nkv = (ckv * jax.lax.rsqrt(jnp.mean(ckv * ckv, axis=-1, keepdims=True) + EPS) * kvng_ref[...]).astype(BF16)
    kk = _dot(nkv, wk_ref[...])
    vm_ref[0] = _dot(nkv, wv_ref[...]).astype(BF16)
    kpm = seg(S_KPM, S_KPS)
    kpe = kpm * ck + seg(S_KPS, NP_IN) * sk
    for h in range(B_HEADS):
        sl = slice(h * LANE, (h + 1) * LANE)
        km_ref[0, :, sl] = (kk[:, sl] + kpe).astype(BF16)

    lane = jax.lax.broadcasted_iota(jnp.int32, kpm.shape, 1)
    z = kpm + gsc_ref[1:2, :]
    sp = jnp.maximum(z, 0.0) + jnp.log1p(jnp.exp(-jnp.abs(z)))
    g = -jnp.exp(gsc_ref[0:1, :]) * sp
    gates_ref[0] = jnp.where(lane < 2 * A_HEADS, jax.nn.sigmoid(kpm), jnp.where(lane < 4 * A_HEADS, g, 0.0))


def _inproj(x, mod, per_batch, ng, w, qng, wqm, wqs, kvng, wk, wv, tab, gsc, tm):
    b, t, d = x.shape
    tm = min(tm, t)
    bm = (lambda i, j: (i, 0, 0)) if per_batch else (lambda i, j: (0, 0, 0))
    tok = lambda width: pl.BlockSpec((1, tm, width), lambda i, j: (i, j, 0))
    full = lambda a: pl.BlockSpec(a.shape, lambda i, j: (0,) * a.ndim)
    widths = (CIN_W, A_W, LANE, B_HEADS * LANE, B_HEADS * LANE, B_W, B_W, C_W)
    dtypes = (BF16, BF16, F32, BF16, BF16, BF16, BF16, BF16)
    return pl.pallas_call(
        _inproj_kernel,
        grid=(b, t // tm),
        in_specs=[tok(d), pl.BlockSpec((1, 3, d), bm), full(ng), full(w), full(qng), full(wqm), full(wqs),
                  full(kvng), full(wk), full(wv),
                  pl.BlockSpec((tm, 4 * LANE), lambda i, j: (j, 0)), full(gsc)],
        out_specs=[tok(wd) for wd in widths],
        out_shape=[jax.ShapeDtypeStruct((b, t, wd), dt) for wd, dt in zip(widths, dtypes)],
        compiler_params=_params(("parallel", "parallel")),
        name="inproj",
    )(x, mod, ng, w, qng, wqm, wqs, kvng, wk, wv, tab, gsc)


def _conv_kernel(main_ref, prev_ref, next_ref, w_ref, zc_ref, qn_ref, kn_ref, vv_ref, cx_ref, *, tm):
    j = pl.program_id(1)
    nj = pl.num_programs(1)
    pad = A_CONV // 2

    def conv_cols(c0, c1):
        prev = jnp.where(j > 0, prev_ref[0, :, c0:c1].astype(F32), 0.0)
        nxt = jnp.where(j < nj - 1, next_ref[0, :, c0:c1].astype(F32), 0.0)
        ext = jnp.concatenate([prev, main_ref[0, :, c0:c1].astype(F32), nxt], axis=0)
        acc = None
        for k in range(A_CONV):
            term = ext[HALO - pad + k:HALO - pad + k + tm, :] * w_ref[k:k + 1, c0:c1]
            acc = term if acc is None else acc + term
        return acc

    for h in range(3 * A_HEADS):
        y = _silu(conv_cols(h * LANE, (h + 1) * LANE))
        if h < 2 * A_HEADS:
            y = y * jax.lax.rsqrt(jnp.sum(y * y, axis=-1, keepdims=True) + EPS)
        if h < A_HEADS:
            qn_ref[0, :, h * LANE:(h + 1) * LANE] = (y * (A_DK ** -0.5)).astype(BF16)
        elif h < 2 * A_HEADS:
            kn_ref[0, :, (h - A_HEADS) * LANE:(h - A_HEADS + 1) * LANE] = y.astype(BF16)
        else:
            vv_ref[0, :, (h - 2 * A_HEADS) * LANE:(h - 2 * A_HEADS + 1) * LANE] = y.astype(BF16)
    cx_ref[0] = (zc_ref[0].astype(F32) * conv_cols(3 * A_QK, CIN_W)).astype(BF16)


def _conv(cin, wconv, zc, tm):
    b, t, _ = cin.shape
    tm = min(tm, t)
    r = tm // HALO
    nh = t // HALO
    tok = lambda width: pl.BlockSpec((1, tm, width), lambda i, j: (i, j, 0))
    return pl.pallas_call(
        functools.partial(_conv_kernel, tm=tm),
        grid=(b, t // tm),
        in_specs=[tok(CIN_W),
                  pl.BlockSpec((1, HALO, CIN_W), lambda i, j: (i, jnp.maximum(j * r - 1, 0), 0)),
                  pl.BlockSpec((1, HALO, CIN_W), lambda i, j: (i, jnp.minimum((j + 1) * r, nh - 1), 0)),
                  pl.BlockSpec(wconv.shape, lambda i, j: (0, 0)),
                  tok(C_W)],
        out_specs=[tok(A_QK), tok(A_QK), tok(A_W), tok(C_W)],
        out_shape=[jax.ShapeDtypeStruct((b, t, wd), BF16) for wd in (A_QK, A_QK, A_W, C_W)],
        compiler_params=_params(("parallel", "parallel")),
        name="conv",
    )(cin, cin, cin, wconv, zc)


INV_BASE_LOG2 = 3


def _unit_triangular_inverse(a, c):
    row = jax.lax.broadcasted_iota(jnp.int32, (c, c), 0)
    col = jax.lax.broadcasted_iota(jnp.int32, (c, c), 1)
    same = lambda k: (row >> k) == (col >> k)
    a0 = jnp.where(same(INV_BASE_LOG2), a, 0.0)
    t = jnp.where(row == col, 1.0, 0.0) - a0
    apow = a0.astype(BF16)
    for _ in range(INV_BASE_LOG2 - 1):
        apow = _dot(apow, apow).astype(BF16)
        t = t + _dot(t.astype(BF16), apow)
    for k in range(INV_BASE_LOG2, int(math.log2(c))):
        e = jnp.where(same(k + 1) & jnp.logical_not(same(k)), a, 0.0).astype(BF16)
        tb = t.astype(BF16)
        t = t - _dot(_dot(tb, e).astype(BF16), tb)
    return t


def _scan_kernel(qf_ref, kf_ref, vf_ref, gf_ref, qb_ref, kb_ref, vb_ref, gb_ref, s0_ref,
                 of_ref, ob_ref, s_ref, *, c):
    n = pl.program_id(1)

    @pl.when(n == 0)
    def _():
        s_ref[...] = s0_ref[...]

    row = jax.lax.broadcasted_iota(jnp.int32, (c, c), 0)
    col = jax.lax.broadcasted_iota(jnp.int32, (c, c), 1)
    dirs = ((qf_ref, kf_ref, vf_ref, gf_ref, of_ref), (qb_ref, kb_ref, vb_ref, gb_ref, ob_ref))
    for d, (q_ref, k_ref, v_ref, g_ref, o_ref) in enumerate(dirs):
        incl = (col <= row) if d == 0 else (col >= row)
        strict = (col < row) if d == 0 else (col > row)
        gates = g_ref[0]
        tri = jnp.where(incl, 1.0, 0.0).astype(BF16)
        hi = gates.astype(BF16)
        r1 = gates - hi.astype(F32)
        mid = r1.astype(BF16)
        lo = (r1 - mid.astype(F32)).astype(BF16)
        gcb = _dot(tri, hi) + _dot(tri, mid) + _dot(tri, lo)
        gcb_t = gcb.T
        gl_row = gcb[c - 1:c, :] if d == 0 else gcb[0:1, :]
        for h in range(A_HEADS):
            jb = A_HEADS * d + h
            jg = 2 * A_HEADS + jb
            sl = slice(h * LANE, (h + 1) * LANE)
            beta = gates[:, jb:jb + 1]
            gc = gcb[:, jg:jg + 1]
            gcr = gcb_t[jg:jg + 1, :]
            gl = gl_row[:, jg:jg + 1]
            decay = jnp.where(incl, jnp.exp(jnp.where(incl, gc - gcr, 0.0)), 0.0)
            q = q_ref[0, :, sl]
            k = k_ref[0, :, sl]
            kf = k.astype(F32)
            kbeta = kf * beta
            egc = jnp.exp(gc)
            a = jnp.where(strict, _dot_nt(kbeta.astype(BF16), k) * decay, 0.0)
            qk = (_dot_nt(q, k) * decay).astype(BF16)
            tinv = _unit_triangular_inverse(a, c).astype(BF16)
            rhs = jnp.concatenate([v_ref[0, :, sl].astype(F32) * beta, kbeta * egc], axis=1).astype(BF16)
            uw = _dot(tinv, rhs).astype(BF16)
            x = _dot(qk, uw)
            qp = (q.astype(F32) * egc - x[:, A_DV:]).astype(BF16)
            kd_t = (kf * jnp.exp(gl - gc)).T.astype(BF16)
            y = _dot(kd_t, uw)
            s = s_ref[0, d, h]
            sb = s.astype(BF16)
            o_ref[0, :, sl] = _dot(qp, sb) + x[:, :A_DV]
            s_ref[0, d, h] = jnp.exp(gl) * s - _dot(y[:, A_DV:].astype(BF16), sb) + y[:, :A_DV]


def _scan(qn, kn, vv, gates, s0):
    b, t, _ = qn.shape
    c = GDN_CHUNK
    n = t // c
    fwd = lambda width: pl.BlockSpec((1, c, width), lambda i, j: (i, j, 0))
    bwd = lambda width: pl.BlockSpec((1, c, width), lambda i, j: (i, n - 1 - j, 0))
    st = pl.BlockSpec((1, 2, A_HEADS, A_DK, A_DV), lambda i, j: (i, 0, 0, 0, 0))
    return pl.pallas_call(
        functools.partial(_scan_kernel, c=c),
        grid=(b, n),
        in_specs=[fwd(A_QK), fwd(A_QK), fwd(A_W), fwd(LANE), bwd(A_QK), bwd(A_QK), bwd(A_W), bwd(LANE), st],
        out_specs=[fwd(A_W), bwd(A_W), st],
        out_shape=[jax.ShapeDtypeStruct((b, t, A_W), F32), jax.ShapeDtypeStruct((b, t, A_W), F32),
                   jax.ShapeDtypeStruct(s0.shape, F32)],
        compiler_params=_params(("parallel", "arbitrary")),
        name="gdn_scan",
    )(qn, kn, vv, gates, qn, kn, vv, gates, s0)


def _attn_kernel(*refs, n_src):
    q_ref = refs[0]
    kv_refs = refs[1:1 + 2 * n_src]
    szb_ref = refs[1 + 2 * n_src]
    o_ref = refs[2 + 2 * n_src]
    outs = []
    for h in range(B_HEADS):
        qh = q_ref[0, :, h * LANE:(h + 1) * LANE]
        scores = [_dot_nt(qh, kv_refs[2 * i][0, :, h * LANE:(h + 1) * LANE]) for i in range(n_src)]
        m = scores[0].max(axis=-1, keepdims=True)
        for s in scores[1:]:
            m = jnp.maximum(m, s.max(axis=-1, keepdims=True))
        l = None
        acc = None
        for i, s in enumerate(scores):
            p = jnp.exp(s - m)
            ps = p.sum(axis=-1, keepdims=True)
            pv = _dot(p.astype(BF16), kv_refs[2 * i + 1][0, :, h * B_V:(h + 1) * B_V])
            l = ps if l is None else l + ps
            acc = pv if acc is None else acc + pv
        outs.append(acc / l)
    o = jnp.concatenate(outs, axis=-1)
    o_ref[0] = (o * szb_ref[0].astype(F32)).astype(BF16)


def _attn(q, kvs, szb, tq):
    b, t, _ = q.shape
    tq = min(tq, t)
    tok = lambda width: pl.BlockSpec((1, tq, width), lambda i, j: (i, j, 0))
    in_specs = [tok(B_HEADS * LANE)]
    args = [q]
    for k, v in kvs:
        in_specs.append(pl.BlockSpec((1,) + k.shape[1:], lambda i, j: (i, 0, 0)))
        in_specs.append(pl.BlockSpec((1,) + v.shape[1:], lambda i, j: (i, 0, 0)))
        args += [k, v]
    in_specs.append(tok(B_W))
    args.append(szb)
    return pl.pallas_call(
        functools.partial(_attn_kernel, n_src=len(kvs)),
        grid=(b, t // tq),
        in_specs=in_specs,
        out_specs=tok(B_W),
        out_shape=jax.ShapeDtypeStruct((b, t, B_W), BF16),
        compiler_params=_params(("parallel", "parallel")),
        name="attn",
    )(*args)


def _outproj_kernel(x_ref, of_ref, ob_ref, sza_ref, bx_ref, cx_ref, gng_ref, mod_ref, w_ref, fg_ref, o_ref,
                    *, final):
    o = of_ref[0] + ob_ref[0]
    sza = sza_ref[0].astype(F32)
    acc = None
    for h in range(A_HEADS):
        sl = slice(h * LANE, (h + 1) * LANE)
        oh = o[:, sl]
        ah = oh * jax.lax.rsqrt(jnp.mean(oh * oh, axis=-1, keepdims=True) + EPS) * gng_ref[...]
        term = _dot((ah * sza[:, sl]).astype(BF16), w_ref[sl, :])
        acc = term if acc is None else acc + term
    acc = acc + _dot(bx_ref[0], w_ref[A_W:A_W + B_W, :]) + _dot(cx_ref[0], w_ref[A_W + B_W:, :])
    xn = x_ref[0] + mod_ref[0, 2:3, :] * acc
    if final:
        xn = xn * jax.lax.rsqrt(jnp.mean(xn * xn, axis=-1, keepdims=True) + EPS) * fg_ref[...]
    o_ref[0] = xn


def _outproj(x, of, ob, sza, bx, cx, gng, mod, per_batch, w, fg, final, tm):
    b, t, d = x.shape
    tm = min(tm, t)
    bm = (lambda i, j: (i, 0, 0)) if per_batch else (lambda i, j: (0, 0, 0))
    tok = lambda width: pl.BlockSpec((1, tm, width), lambda i, j: (i, j, 0))
    full = lambda a: pl.BlockSpec(a.shape, lambda i, j: (0,) * a.ndim)
    return pl.pallas_call(
        functools.partial(_outproj_kernel, final=final),
        grid=(b, t // tm),
        in_specs=[tok(d), tok(A_W), tok(A_W), tok(A_W), tok(B_W), tok(C_W), full(gng),
                  pl.BlockSpec((1, 3, d), bm), full(w), full(fg)],
        out_specs=tok(d),
        out_shape=jax.ShapeDtypeStruct((b, t, d), F32),
        compiler_params=_params(("parallel", "parallel")),
        name="outproj",
    )(x, of, ob, sza, bx, cx, gng, mod, w, fg)


def _deinterleave(w32):
    even, odd = w32[:, 0::2], w32[:, 1::2]
    return jnp.concatenate([even, odd], axis=1), jnp.concatenate([odd, even], axis=1)


def _layout_w_in(w):
    d = w.shape[0]
    z = lambda n: jnp.zeros((d, n), w.dtype)
    o_b, o_a, o_qa, o_kv, o_zb, o_h, o_bg, o_cg, o_zc = 2048, 2056, 2064, 2256, 2416, 2672, 2928, 3184, 3440
    kp_main, kp_swap = _deinterleave(w[:, o_kv + B_KV_LORA:o_zb])
    cols = [w[:, 0:2048], w[:, o_zb:o_h], w[:, o_h:o_bg], w[:, o_bg:o_cg], w[:, o_cg:o_zc], w[:, o_zc:o_zc + C_W],
            w[:, o_qa:o_kv], z(S_CKV - S_QA - B_Q_LORA), w[:, o_kv:o_kv + B_KV_LORA],
            w[:, o_b:o_a], w[:, o_a:o_qa], z(B_NOPE - 4 * A_HEADS), kp_main, z(LANE - B_NOPE - B_ROPE),
            z(B_NOPE), kp_swap, z(LANE - B_NOPE - B_ROPE)]
    out = jnp.concatenate(cols, axis=1).astype(BF16)
    assert out.shape[1] == NP_IN
    return out


def _layout_w_qb(w):
    r = w.shape[0]
    z = lambda n: jnp.zeros((r, n), w.dtype)
    hd = B_NOPE + B_ROPE
    main, swap = [], []
    for h in range(B_HEADS):
        rm, rs = _deinterleave(w[:, h * hd + B_NOPE:(h + 1) * hd])
        main += [w[:, h * hd:h * hd + B_NOPE], rm, z(LANE - hd)]
        swap += [z(B_NOPE), rs, z(LANE - hd)]
    return jnp.concatenate(main, axis=1).astype(BF16), jnp.concatenate(swap, axis=1).astype(BF16)


def _layout_w_kvb(w):
    r = w.shape[0]
    hd = B_NOPE + B_V
    wk, wv = [], []
    for h in range(B_HEADS):
        wk += [w[:, h * hd:h * hd + B_NOPE], jnp.zeros((r, LANE - B_NOPE), w.dtype)]
        wv.append(w[:, h * hd + B_NOPE:(h + 1) * hd])
    return jnp.concatenate(wk, axis=1).astype(BF16), jnp.concatenate(wv, axis=1).astype(BF16)


def _rope_tables(t, rotate):
    half = B_ROPE // 2
    if rotate:
        pos = jnp.arange(t)
        n_freq = B_ROPE // 4
        inv_freq = ROPE_THETA ** (-jnp.arange(n_freq, dtype=F32) / n_freq)
        ang = jnp.concatenate([(pos // GRID_W)[:, None] * inv_freq, (pos % GRID_W)[:, None] * inv_freq], axis=-1)
        cos, sin = jnp.cos(ang), jnp.sin(ang)
    else:
        cos, sin = jnp.ones((t, half), F32), jnp.zeros((t, half), F32)
    z = lambda n: jnp.zeros((t, n), F32)
    tail = LANE - B_NOPE - B_ROPE
    scale = (B_NOPE + B_ROPE) ** -0.5
    cq = jnp.concatenate([jnp.ones((t, B_NOPE), F32), cos, cos, z(tail)], axis=1) * scale
    sq = jnp.concatenate([z(B_NOPE), -sin, sin, z(tail)], axis=1) * scale
    ck = jnp.concatenate([z(B_NOPE), cos, cos, z(tail)], axis=1)
    sk = jnp.concatenate([z(B_NOPE), -sin, sin, z(tail)], axis=1)
    return jnp.concatenate([cq, sq, ck, sk], axis=1)


def kernel(x, c, ctx, c_ctx, w_ada, b_ada, norm_g, w_in, gdn_conv, gdn_a_log, gdn_dt_bias, gdn_norm_g,
           mla_q_norm_g, mla_w_qb, mla_kv_norm_g, mla_w_kvb, conv_w, w_out, final_norm_g):
    bsz, t, d = x.shape
    t_ctx = ctx.shape[1]
    depth = w_ada.shape[0]
    assert t % GDN_CHUNK == 0 and t_ctx % GDN_CHUNK == 0 and t % GRID_W == 0
    tm = 256

    rows = -(-(bsz + 1) // 8) * 8
    cc = jnp.concatenate([c, c_ctx[None, :], jnp.zeros((rows - bsz - 1, d), F32)], axis=0)
    mod = _ada(cc, w_ada, b_ada)

    tab_x = _rope_tables(t, True)
    tab_c = _rope_tables(t_ctx, False)
    fg = final_norm_g.reshape(1, d)

    for l in range(depth):
        mod_x = mod[l, :bsz].reshape(bsz, 3, d)
        mod_c = mod[l, bsz:bsz + 1].reshape(1, 3, d)
        ng = norm_g[l].reshape(1, d)
        w = _layout_w_in(w_in[l])
        wqm, wqs = _layout_w_qb(mla_w_qb[l])
        wk, wv = _layout_w_kvb(mla_w_kvb[l])
        qng = mla_q_norm_g[l].reshape(1, B_Q_LORA)
        kvng = mla_kv_norm_g[l].reshape(1, B_KV_LORA)
        gpad = lambda a: jnp.concatenate([jnp.zeros((2 * A_HEADS,), F32), a.reshape(-1),
                                          jnp.zeros((LANE - 4 * A_HEADS,), F32)])
        gsc = jnp.stack([gpad(gdn_a_log[l]), gpad(gdn_dt_bias[l])] + [jnp.zeros((LANE,), F32)] * 6)
        wconv = jnp.concatenate(
            [gdn_conv[l], jnp.pad(conv_w[l], (((A_CONV - C_CONV) // 2,) * 2, (0, 0)))], axis=1)
        wconv = jnp.concatenate([wconv, jnp.zeros((8 - A_CONV, CIN_W), F32)], axis=0)
        gng = gdn_norm_g[l].reshape(1, A_DV)
        wo = w_out[l].astype(BF16)
        last = l == depth - 1

        shared = (ng, w, qng, wqm, wqs, kvng, wk, wv)
        cin_c, sza_c, gates_c, qm_c, km_c, vm_c, szb_c, zc_c = _inproj(ctx, mod_c, False, *shared, tab_c, gsc, tm)
        cin_x, sza_x, gates_x, qm_x, km_x, vm_x, szb_x, zc_x = _inproj(x, mod_x, True, *shared, tab_x, gsc, tm)
        qn_c, kn_c, vv_c, cx_c = _conv(cin_c, wconv, zc_c, tm)
        qn_x, kn_x, vv_x, cx_x = _conv(cin_x, wconv, zc_x, tm)

        s_zero = jnp.zeros((bsz, 2, A_HEADS, A_DK, A_DV), F32)
        of_c, ob_c, s_ctx = _scan(qn_c, kn_c, vv_c, gates_c, s_zero)
        of_x, ob_x, _ = _scan(qn_x, kn_x, vv_x, gates_x, s_ctx)

        bx_x = _attn(qm_x, [(km_x, vm_x), (km_c, vm_c)], szb_x, tm)
        x_new = _outproj(x, of_x, ob_x, sza_x, bx_x, cx_x, gng, mod_x, True, wo, fg, last, tm)
        if not last:
            bx_c = _attn(qm_c, [(km_c, vm_c)], szb_c, tm)
            ctx = _outproj(ctx, of_c, ob_c, sza_c, bx_c, cx_c, gng, mod_c, False, wo, fg, False, tm)
        x = x_new
    return x
```

```python
import functools
import math

import jax
import jax.numpy as jnp
import numpy as np
from jax.experimental import pallas as pl
from jax.experimental.pallas import tpu as pltpu

F32 = jnp.float32
BF16 = jnp.bfloat16

EPS = 1e-6
GRID_W = 64
ROPE_THETA = 10000.0

A_HEADS = 4
A_DK = 128
A_DV = 128
A_QK = A_HEADS * A_DK
A_W = A_HEADS * A_DV
A_CONV = 5
B_HEADS = 4
B_NOPE = 64
B_ROPE = 32
B_V = 64
B_Q_LORA = 192
B_KV_LORA = 128
B_W = B_HEADS * B_V
C_W = 256
C_CONV = 3

LANE = 128
HALO = 16
GDN_CHUNK = 128
VMEM_LIMIT = 48 * 1024 * 1024

S_QKV = 0
S_ZA = 1536
S_ZB = 2048
S_H = 2304
S_BG = 2560
S_CG = 2816
S_ZC = 3072
S_QA = 3328
S_CKV = 3584
S_KPM = 3712
S_KPS = 3840
NP_IN = 3968
CIN_W = 3 * A_QK + C_W


def _dot(a, b):
    return jnp.dot(a, b, preferred_element_type=F32)


def _dot_nt(a, b):
    return jax.lax.dot_general(a, b, (((1,), (1,)), ((), ())), preferred_element_type=F32)


def _silu(x):
    return x * jax.nn.sigmoid(x)


def _params(sem):
    return pltpu.CompilerParams(dimension_semantics=sem, vmem_limit_bytes=VMEM_LIMIT)


def _ada_kernel(c_ref, w_ref, b_ref, o_ref):
    s = _silu(c_ref[...]).astype(BF16)
    o_ref[0] = _dot(s, w_ref[0].astype(BF16)) + b_ref[0]


def _ada(cc, w_ada, b_ada):
    depth, d, n3 = w_ada.shape
    r = cc.shape[0]
    tn = 768
    return pl.pallas_call(
        _ada_kernel,
        grid=(depth, n3 // tn),
        in_specs=[pl.BlockSpec((r, d), lambda l, j: (0, 0)),
                  pl.BlockSpec((1, d, tn), lambda l, j: (l, 0, j)),
                  pl.BlockSpec((1, 1, tn), lambda l, j: (l, 0, j))],
        out_specs=pl.BlockSpec((1, r, tn), lambda l, j: (l, 0, j)),
        out_shape=jax.ShapeDtypeStruct((depth, r, n3), F32),
        compiler_params=_params(("parallel", "parallel")),
        name="ada",
    )(cc, w_ada, b_ada.reshape(depth, 1, n3))


def _inproj_kernel(x_ref, mod_ref, ng_ref, w_ref, qng_ref, wqm_ref, wqs_ref, kvng_ref, wk_ref, wv_ref,
                   tab_ref, gsc_ref,
                   cin_ref, sza_ref, gates_ref, qm_ref, km_ref, vm_ref, szb_ref, zc_ref):
    x = x_ref[0]
    y = x * jax.lax.rsqrt(jnp.mean(x * x, axis=-1, keepdims=True) + EPS) * ng_ref[...]
    hn = (y * (1.0 + mod_ref[0, 1:2, :]) + mod_ref[0, 0:1, :]).astype(BF16)

    def seg(a, b):
        return _dot(hn, w_ref[:, a:b])

    cin_ref[0, :, 0:3 * A_QK] = seg(S_QKV, S_ZA).astype(BF16)
    sza_ref[0] = _silu(seg(S_ZA, S_ZB)).astype(BF16)
    szb_ref[0] = _silu(seg(S_ZB, S_H)).astype(BF16)
    cin_ref[0, :, 3 * A_QK:CIN_W] = (seg(S_CG, S_ZC) * seg(S_H, S_BG)).astype(BF16)
    zc_ref[0] = (seg(S_BG, S_CG) * _silu(seg(S_ZC, S_QA))).astype(BF16)

    tab = tab_ref[...]
    cq, sq = tab[:, 0:LANE], tab[:, LANE:2 * LANE]
    ck, sk = tab[:, 2 * LANE:3 * LANE], tab[:, 3 * LANE:4 * LANE]

    qa = seg(S_QA, S_QA + B_Q_LORA)
    nq = (qa * jax.lax.rsqrt(jnp.mean(qa * qa, axis=-1, keepdims=True) + EPS) * qng_ref[...]).astype(BF16)
    qmain = _dot(nq, wqm_ref[...])
    qswap = _dot(nq, wqs_ref[...])
    for h in range(B_HEADS):
        sl = slice(h * LANE, (h + 1) * LANE)
        qm_ref[0, :, sl] = (qmain[:, sl] * cq + qswap[:, sl] * sq).astype(BF16)

    ckv = seg(S_CKV, S_KPM)
    nkv = (ckv * jax.lax.rsqrt(jnp.mean(ckv * ckv, axis=-1, keepdims=True) + EPS) * kvng_ref[...]).astype(BF16)
    kk = _dot(nkv, wk_ref[...])
    vm_ref[0] = _dot(nkv, wv_ref[...]).astype(BF16)
    kpm = seg(S_KPM, S_KPS)
    kpe = kpm * ck + seg(S_KPS, NP_IN) * sk
    for h in range(B_HEADS):
        sl = slice(h * LANE, (h + 1) * LANE)
        km_ref[0, :, sl] = (kk[:, sl] + kpe).astype(BF16)

    lane = jax.lax.broadcasted_iota(jnp.int32, kpm.shape, 1)
    z = kpm + gsc_ref[1:2, :]
    sp = jnp.maximum(z, 0.0) + jnp.log1p(jnp.exp(-jnp.abs(z)))
    g = -jnp.exp(gsc_ref[0:1, :]) * sp
    gates_ref[0] = jnp.where(lane < 2 * A_HEADS, jax.nn.sigmoid(kpm), jnp.where(lane < 4 * A_HEADS, g, 0.0))


def _inproj(x, mod, per_batch, ng, w, qng, wqm, wqs, kvng, wk, wv, tab, gsc, tm):
    b, t, d = x.shape
    tm = min(tm, t)
    bm = (lambda i, j: (i, 0, 0)) if per_batch else (lambda i, j: (0, 0, 0))
    tok = lambda width: pl.BlockSpec((1, tm, width), lambda i, j: (i, j, 0))
    full = lambda a: pl.BlockSpec(a.shape, lambda i, j: (0,) * a.ndim)
    widths = (CIN_W, A_W, LANE, B_HEADS * LANE, B_HEADS * LANE, B_W, B_W, C_W)
    dtypes = (BF16, BF16, F32, BF16, BF16, BF16, BF16, BF16)
    return pl.pallas_call(
        _inproj_kernel,
        grid=(b, t // tm),
        in_specs=[tok(d), pl.BlockSpec((1, 3, d), bm), full(ng), full(w), full(qng), full(wqm), full(wqs),
                  full(kvng), full(wk), full(wv),
                  pl.BlockSpec((tm, 4 * LANE), lambda i, j: (j, 0)), full(gsc)],
        out_specs=[tok(wd) for wd in widths],
        out_shape=[jax.ShapeDtypeStruct((b, t, wd), dt) for wd, dt in zip(widths, dtypes)],
        compiler_params=_params(("parallel", "parallel")),
        name="inproj",
    )(x, mod, ng, w, qng, wqm, wqs, kvng, wk, wv, tab, gsc)


def _conv_kernel(main_ref, prev_ref, next_ref, w_ref, zc_ref, qn_ref, kn_ref, vv_ref, cx_ref, *, tm):
    j = pl.program_id(1)
    nj = pl.num_programs(1)
    pad = A_CONV // 2

    def conv_cols(c0, c1):
        prev = jnp.where(j > 0, prev_ref[0, :, c0:c1].astype(F32), 0.0)
        nxt = jnp.where(j < nj - 1, next_ref[0, :, c0:c1].astype(F32), 0.0)
        ext = jnp.concatenate([prev, main_ref[0, :, c0:c1].astype(F32), nxt], axis=0)
        acc = None
        for k in range(A_CONV):
            term = ext[HALO - pad + k:HALO - pad + k + tm, :] * w_ref[k:k + 1, c0:c1]
            acc = term if acc is None else acc + term
        return acc

    for h in range(3 * A_HEADS):
        y = _silu(conv_cols(h * LANE, (h + 1) * LANE))
        if h < 2 * A_HEADS:
            y = y * jax.lax.rsqrt(jnp.sum(y * y, axis=-1, keepdims=True) + EPS)
        if h < A_HEADS:
            qn_ref[0, :, h * LANE:(h + 1) * LANE] = (y * (A_DK ** -0.5)).astype(BF16)
        elif h < 2 * A_HEADS:
            kn_ref[0, :, (h - A_HEADS) * LANE:(h - A_HEADS + 1) * LANE] = y.astype(BF16)
        else:
            vv_ref[0, :, (h - 2 * A_HEADS) * LANE:(h - 2 * A_HEADS + 1) * LANE] = y.astype(BF16)
    cx_ref[0] = (zc_ref[0].astype(F32) * conv_cols(3 * A_QK, CIN_W)).astype(BF16)


def _conv(cin, wconv, zc, tm):
    b, t, _ = cin.shape
    tm = min(tm, t)
    r = tm // HALO
    nh = t // HALO
    tok = lambda width: pl.BlockSpec((1, tm, width), lambda i, j: (i, j, 0))
    return pl.pallas_call(
        functools.partial(_conv_kernel, tm=tm),
        grid=(b, t // tm),
        in_specs=[tok(CIN_W),
                  pl.BlockSpec((1, HALO, CIN_W), lambda i, j: (i, jnp.maximum(j * r - 1, 0), 0)),
                  pl.BlockSpec((1, HALO, CIN_W), lambda i, j: (i, jnp.minimum((j + 1) * r, nh - 1), 0)),
                  pl.BlockSpec(wconv.shape, lambda i, j: (0, 0)),
                  tok(C_W)],
        out_specs=[tok(A_QK), tok(A_QK), tok(A_W), tok(C_W)],
        out_shape=[jax.ShapeDtypeStruct((b, t, wd), BF16) for wd in (A_QK, A_QK, A_W, C_W)],
        compiler_params=_params(("parallel", "parallel")),
        name="conv",
    )(cin, cin, cin, wconv, zc)


INV_BASE_LOG2 = 3


def _unit_triangular_inverses(mats, c):
    row = jax.lax.broadcasted_iota(jnp.int32, (c, c), 0)
    col = jax.lax.broadcasted_iota(jnp.int32, (c, c), 1)
    same = lambda k: (row >> k) == (col >> k)
    eye = jnp.where(row == col, 1.0, 0.0)
    base = same(INV_BASE_LOG2)
    a0 = [jnp.where(base, a, 0.0) for a in mats]
    t = [eye - ai for ai in a0]
    apow = [ai.astype(BF16) for ai in a0]
    for _ in range(INV_BASE_LOG2 - 1):
        apow = [_dot(ap, ap).astype(BF16) for ap in apow]
        t = [ti + _dot(ti.astype(BF16), ap) for ti, ap in zip(t, apow)]
    for k in range(INV_BASE_LOG2, int(math.log2(c))):
        coupling = same(k + 1) & jnp.logical_not(same(k))
        e = [jnp.where(coupling, a, 0.0).astype(BF16) for a in mats]
        tb = [ti.astype(BF16) for ti in t]
        te = [_dot(tbi, ei).astype(BF16) for tbi, ei in zip(tb, e)]
        t = [ti - _dot(tei, tbi) for ti, tei, tbi in zip(t, te, tb)]
    return t


def _scan_kernel(qf_ref, kf_ref, vf_ref, gf_ref, qb_ref, kb_ref, vb_ref, gb_ref, s0_ref,
                 of_ref, ob_ref, s_ref, *, c):
    n = pl.program_id(1)

    @pl.when(n == 0)
    def _():
        s_ref[...] = s0_ref[...]

    row = jax.lax.broadcasted_iota(jnp.int32, (c, c), 0)
    col = jax.lax.broadcasted_iota(jnp.int32, (c, c), 1)
    dirs = ((qf_ref, kf_ref, vf_ref, gf_ref, of_ref), (qb_ref, kb_ref, vb_ref, gb_ref, ob_ref))
    incl = (col <= row, col >= row)
    strict = (col < row, col > row)
    gates, gcb, gcb_t, gl_row = [], [], [], []
    for d in range(2):
        g = dirs[d][3][0]
        tri = jnp.where(incl[d], 1.0, 0.0).astype(BF16)
        hi = g.astype(BF16)
        r1 = g - hi.astype(F32)
        mid = r1.astype(BF16)
        lo = (r1 - mid.astype(F32)).astype(BF16)
        gc_all = _dot(tri, hi) + _dot(tri, mid) + _dot(tri, lo)
        gates.append(g)
        gcb.append(gc_all)
        gcb_t.append(gc_all.T)
        gl_row.append(gc_all[c - 1:c, :] if d == 0 else gc_all[0:1, :])

    pairs = [(d, h) for d in range(2) for h in range(A_HEADS)]
    sls = [slice(h * LANE, (h + 1) * LANE) for _, h in pairs]
    jbs = [A_HEADS * d + h for d, h in pairs]
    jgs = [2 * A_HEADS + jb for jb in jbs]
    beta = [gates[d][:, jb:jb + 1] for (d, _), jb in zip(pairs, jbs)]
    gc = [gcb[d][:, jg:jg + 1] for (d, _), jg in zip(pairs, jgs)]
    gcr = [gcb_t[d][jg:jg + 1, :] for (d, _), jg in zip(pairs, jgs)]
    gl = [gl_row[d][:, jg:jg + 1] for (d, _), jg in zip(pairs, jgs)]
    q = [dirs[d][0][0, :, sl] for (d, _), sl in zip(pairs, sls)]
    k = [dirs[d][1][0, :, sl] for (d, _), sl in zip(pairs, sls)]
    v = [dirs[d][2][0, :, sl] for (d, _), sl in zip(pairs, sls)]
    decay = [jnp.where(incl[d], jnp.exp(jnp.where(incl[d], gci - gcri, 0.0)), 0.0)
             for (d, _), gci, gcri in zip(pairs, gc, gcr)]
    kbeta = [ki.astype(F32) * bi for ki, bi in zip(k, beta)]
    egc = [jnp.exp(gci) for gci in gc]
    a = [jnp.where(strict[d], _dot_nt(kbi.astype(BF16), ki) * di, 0.0)
         for (d, _), kbi, ki, di in zip(pairs, kbeta, k, decay)]
    qk = [(_dot_nt(qi, ki) * di).astype(BF16) for qi, ki, di in zip(q, k, decay)]
    rhs = [jnp.concatenate([vi.astype(F32) * bi, kbi * ei], axis=1).astype(BF16)
           for vi, bi, kbi, ei in zip(v, beta, kbeta, egc)]
    kd_t = [(ki.astype(F32) * jnp.exp(gli - gci)).T.astype(BF16) for ki, gli, gci in zip(k, gl, gc)]
    tinv = [ti.astype(BF16) for ti in _unit_triangular_inverses(a, c)]
    uw = [_dot(ti, ri).astype(BF16) for ti, ri in zip(tinv, rhs)]
    x = [_dot(qki, uwi) for qki, uwi in zip(qk, uw)]
    y = [_dot(kdi, uwi) for kdi, uwi in zip(kd_t, uw)]
    qp = [(qi.astype(F32) * ei - xi[:, A_DV:]).astype(BF16) for qi, ei, xi in zip(q, egc, x)]
    s = [s_ref[0, d, h] for d, h in pairs]
    sb = [si.astype(BF16) for si in s]
    for (d, h), sl, qpi, sbi, xi in zip(pairs, sls, qp, sb, x):
        dirs[d][4][0, :, sl] = _dot(qpi, sbi) + xi[:, :A_DV]
    for (d, h), gli, si, sbi, yi in zip(pairs, gl, s, sb, y):
        s_ref[0, d, h] = jnp.exp(gli) * si - _dot(yi[:, A_DV:].astype(BF16), sbi) + yi[:, :A_DV]


def _scan(qn, kn, vv, gates, s0):
    b, t, _ = qn.shape
    c = GDN_CHUNK
    n = t // c
    fwd = lambda width: pl.BlockSpec((1, c, width), lambda i, j: (i, j, 0))
    bwd = lambda width: pl.BlockSpec((1, c, width), lambda i, j: (i, n - 1 - j, 0))
    st = pl.BlockSpec((1, 2, A_HEADS, A_DK, A_DV), lambda i, j: (i, 0, 0, 0, 0))
    return pl.pallas_call(
        functools.partial(_scan_kernel, c=c),
        grid=(b, n),
        in_specs=[fwd(A_QK), fwd(A_QK), fwd(A_W), fwd(LANE), bwd(A_QK), bwd(A_QK), bwd(A_W), bwd(LANE), st],
        out_specs=[fwd(A_W), bwd(A_W), st],
        out_shape=[jax.ShapeDtypeStruct((b, t, A_W), F32), jax.ShapeDtypeStruct((b, t, A_W), F32),
                   jax.ShapeDtypeStruct(s0.shape, F32)],
        compiler_params=_params(("parallel", "arbitrary")),
        name="gdn_scan",
    )(qn, kn, vv, gates, qn, kn, vv, gates, s0)


def _attn_kernel(*refs, n_src):
    q_ref = refs[0]
    kv_refs = refs[1:1 + 2 * n_src]
    szb_ref = refs[1 + 2 * n_src]
    o_ref = refs[2 + 2 * n_src]
    outs = []
    for h in range(B_HEADS):
        qh = q_ref[0, :, h * LANE:(h + 1) * LANE]
        scores = [_dot_nt(qh, kv_refs[2 * i][0, :, h * LANE:(h + 1) * LANE]) for i in range(n_src)]
        m = scores[0].max(axis=-1, keepdims=True)
        for s in scores[1:]:
            m = jnp.maximum(m, s.max(axis=-1, keepdims=True))
        l = None
        acc = None
        for i, s in enumerate(scores):
            p = jnp.exp(s - m)
            ps = p.sum(axis=-1, keepdims=True)
            pv = _dot(p.astype(BF16), kv_refs[2 * i + 1][0, :, h * B_V:(h + 1) * B_V])
            l = ps if l is None else l + ps
            acc = pv if acc is None else acc + pv
        outs.append(acc / l)
    o = jnp.concatenate(outs, axis=-1)
    o_ref[0] = (o * szb_ref[0].astype(F32)).astype(BF16)


def _attn(q, kvs, szb, tq):
    b, t, _ = q.shape
    tq = min(tq, t)
    tok = lambda width: pl.BlockSpec((1, tq, width), lambda i, j: (i, j, 0))
    in_specs = [tok(B_HEADS * LANE)]
    args = [q]
    for k, v in kvs:
        in_specs.append(pl.BlockSpec((1,) + k.shape[1:], lambda i, j: (i, 0, 0)))
        in_specs.append(pl.BlockSpec((1,) + v.shape[1:], lambda i, j: (i, 0, 0)))
        args += [k, v]
    in_specs.append(tok(B_W))
    args.append(szb)
    return pl.pallas_call(
        functools.partial(_attn_kernel, n_src=len(kvs)),
        grid=(b, t // tq),
        in_specs=in_specs,
        out_specs=tok(B_W),
        out_shape=jax.ShapeDtypeStruct((b, t, B_W), BF16),
        compiler_params=_params(("parallel", "parallel")),
        name="attn",
    )(*args)


def _outproj_kernel(x_ref, of_ref, ob_ref, sza_ref, bx_ref, cx_ref, gng_ref, mod_ref, w_ref, fg_ref, o_ref,
                    *, final):
    o = of_ref[0] + ob_ref[0]
    sza = sza_ref[0].astype(F32)
    acc = None
    for h in range(A_HEADS):
        sl = slice(h * LANE, (h + 1) * LANE)
        oh = o[:, sl]
        ah = oh * jax.lax.rsqrt(jnp.mean(oh * oh, axis=-1, keepdims=True) + EPS) * gng_ref[...]
        term = _dot((ah * sza[:, sl]).astype(BF16), w_ref[sl, :])
        acc = term if acc is None else acc + term
    acc = acc + _dot(bx_ref[0], w_ref[A_W:A_W + B_W, :]) + _dot(cx_ref[0], w_ref[A_W + B_W:, :])
    xn = x_ref[0] + mod_ref[0, 2:3, :] * acc
    if final:
        xn = xn * jax.lax.rsqrt(jnp.mean(xn * xn, axis=-1, keepdims=True) + EPS) * fg_ref[...]
    o_ref[0] = xn


def _outproj(x, of, ob, sza, bx, cx, gng, mod, per_batch, w, fg, final, tm):
    b, t, d = x.shape
    tm = min(tm, t)
    bm = (lambda i, j: (i, 0, 0)) if per_batch else (lambda i, j: (0, 0, 0))
    tok = lambda width: pl.BlockSpec((1, tm, width), lambda i, j: (i, j, 0))
    full = lambda a: pl.BlockSpec(a.shape, lambda i, j: (0,) * a.ndim)
    return pl.pallas_call(
        functools.partial(_outproj_kernel, final=final),
        grid=(b, t // tm),
        in_specs=[tok(d), tok(A_W), tok(A_W), tok(A_W), tok(B_W), tok(C_W), full(gng),
                  pl.BlockSpec((1, 3, d), bm), full(w), full(fg)],
        out_specs=tok(d),
        out_shape=jax.ShapeDtypeStruct((b, t, d), F32),
        compiler_params=_params(("parallel", "parallel")),
        name="outproj",
    )(x, of, ob, sza, bx, cx, gng, mod, w, fg)


def _deinterleave(w32):
    even, odd = w32[:, 0::2], w32[:, 1::2]
    return jnp.concatenate([even, odd], axis=1), jnp.concatenate([odd, even], axis=1)


def _layout_w_in(w):
    d = w.shape[0]
    z = lambda n: jnp.zeros((d, n), w.dtype)
    o_b, o_a, o_qa, o_kv, o_zb, o_h, o_bg, o_cg, o_zc = 2048, 2056, 2064, 2256, 2416, 2672, 2928, 3184, 3440
    kp_main, kp_swap = _deinterleave(w[:, o_kv + B_KV_LORA:o_zb])
    cols = [w[:, 0:2048], w[:, o_zb:o_h], w[:, o_h:o_bg], w[:, o_bg:o_cg], w[:, o_cg:o_zc], w[:, o_zc:o_zc + C_W],
            w[:, o_qa:o_kv], z(S_CKV - S_QA - B_Q_LORA), w[:, o_kv:o_kv + B_KV_LORA],
            w[:, o_b:o_a], w[:, o_a:o_qa], z(B_NOPE - 4 * A_HEADS), kp_main, z(LANE - B_NOPE - B_ROPE),
            z(B_NOPE), kp_swap, z(LANE - B_NOPE - B_ROPE)]
    out = jnp.concatenate(cols, axis=1).astype(BF16)
    assert out.shape[1] == NP_IN
    return out


def _layout_w_qb(w):
    r = w.shape[0]
    z = lambda n: jnp.zeros((r, n), w.dtype)
    hd = B_NOPE + B_ROPE
    main, swap = [], []
    for h in range(B_HEADS):
        rm, rs = _deinterleave(w[:, h * hd + B_NOPE:(h + 1) * hd])
        main += [w[:, h * hd:h * hd + B_NOPE], rm, z(LANE - hd)]
        swap += [z(B_NOPE), rs, z(LANE - hd)]
    return jnp.concatenate(main, axis=1).astype(BF16), jnp.concatenate(swap, axis=1).astype(BF16)


def _layout_w_kvb(w):
    r = w.shape[0]
    hd = B_NOPE + B_V
    wk, wv = [], []
    for h in range(B_HEADS):
        wk += [w[:, h * hd:h * hd + B_NOPE], jnp.zeros((r, LANE - B_NOPE), w.dtype)]
        wv.append(w[:, h * hd + B_NOPE:(h + 1) * hd])
    return jnp.concatenate(wk, axis=1).astype(BF16), jnp.concatenate(wv, axis=1).astype(BF16)


def _rope_tables(t, rotate):
    half = B_ROPE // 2
    if rotate:
        pos = jnp.arange(t)
        n_freq = B_ROPE // 4
        inv_freq = ROPE_THETA ** (-jnp.arange(n_freq, dtype=F32) / n_freq)
        ang = jnp.concatenate([(pos // GRID_W)[:, None] * inv_freq, (pos % GRID_W)[:, None] * inv_freq], axis=-1)
        cos, sin = jnp.cos(ang), jnp.sin(ang)
    else:
        cos, sin = jnp.ones((t, half), F32), jnp.zeros((t, half), F32)
    z = lambda n: jnp.zeros((t, n), F32)
    tail = LANE - B_NOPE - B_ROPE
    scale = (B_NOPE + B_ROPE) ** -0.5
    cq = jnp.concatenate([jnp.ones((t, B_NOPE), F32), cos, cos, z(tail)], axis=1) * scale
    sq = jnp.concatenate([z(B_NOPE), -sin, sin, z(tail)], axis=1) * scale
    ck = jnp.concatenate([z(B_NOPE), cos, cos, z(tail)], axis=1)
    sk = jnp.concatenate([z(B_NOPE), -sin, sin, z(tail)], axis=1)
    return jnp.concatenate([cq, sq, ck, sk], axis=1)


def kernel(x, c, ctx, c_ctx, w_ada, b_ada, norm_g, w_in, gdn_conv, gdn_a_log, gdn_dt_bias, gdn_norm_g,
           mla_q_norm_g, mla_w_qb, mla_kv_norm_g, mla_w_kvb, conv_w, w_out, final_norm_g):
    bsz, t, d = x.shape
    t_ctx = ctx.shape[1]
    depth = w_ada.shape[0]
    assert t % GDN_CHUNK == 0 and t_ctx % GDN_CHUNK == 0 and t % GRID_W == 0
    tm = 256

    rows = -(-(bsz + 1) // 8) * 8
    cc = jnp.concatenate([c, c_ctx[None, :], jnp.zeros((rows - bsz - 1, d), F32)], axis=0)
    mod = _ada(cc, w_ada, b_ada)

    tab_x = _rope_tables(t, True)
    tab_c = _rope_tables(t_ctx, False)
    fg = final_norm_g.reshape(1, d)

    for l in range(depth):
        mod_x = mod[l, :bsz].reshape(bsz, 3, d)
        mod_c = mod[l, bsz:bsz + 1].reshape(1, 3, d)
        ng = norm_g[l].reshape(1, d)
        w = _layout_w_in(w_in[l])
        wqm, wqs = _layout_w_qb(mla_w_qb[l])
        wk, wv = _layout_w_kvb(mla_w_kvb[l])
        qng = mla_q_norm_g[l].reshape(1, B_Q_LORA)
        kvng = mla_kv_norm_g[l].reshape(1, B_KV_LORA)
        gpad = lambda a: jnp.concatenate([jnp.zeros((2 * A_HEADS,), F32), a.reshape(-1),
                                          jnp.zeros((LANE - 4 * A_HEADS,), F32)])
        gsc = jnp.stack([gpad(gdn_a_log[l]), gpad(gdn_dt_bias[l])] + [jnp.zeros((LANE,), F32)] * 6)
        wconv = jnp.concatenate(
            [gdn_conv[l], jnp.pad(conv_w[l], (((A_CONV - C_CONV) // 2,) * 2, (0, 0)))], axis=1)
        wconv = jnp.concatenate([wconv, jnp.zeros((8 - A_CONV, CIN_W), F32)], axis=0)
        gng = gdn_norm_g[l].reshape(1, A_DV)
        wo = w_out[l].astype(BF16)
        last = l == depth - 1

        shared = (ng, w, qng, wqm, wqs, kvng, wk, wv)
        cin_c, sza_c, gates_c, qm_c, km_c, vm_c, szb_c, zc_c = _inproj(ctx, mod_c, False, *shared, tab_c, gsc, tm)
        cin_x, sza_x, gates_x, qm_x, km_x, vm_x, szb_x, zc_x = _inproj(x, mod_x, True, *shared, tab_x, gsc, tm)
        qn_c, kn_c, vv_c, cx_c = _conv(cin_c, wconv, zc_c, tm)
        qn_x, kn_x, vv_x, cx_x = _conv(cin_x, wconv, zc_x, tm)

        s_zero = jnp.zeros((bsz, 2, A_HEADS, A_DK, A_DV), F32)
        of_c, ob_c, s_ctx = _scan(qn_c, kn_c, vv_c, gates_c, s_zero)
        of_x, ob_x, _ = _scan(qn_x, kn_x, vv_x, gates_x, s_ctx)

        bx_x = _attn(qm_x, [(km_x, vm_x), (km_c, vm_c)], szb_x, tm)
        x_new = _outproj(x, of_x, ob_x, sza_x, bx_x, cx_x, gng, mod_x, True, wo, fg, last, tm)
        if not last:
            bx_c = _attn(qm_c, [(km_c, vm_c)], szb_c, tm)
            ctx = _outproj(ctx, of_c, ob_c, sza_c, bx_c, cx_c, gng, mod_c, False, wo, fg, False, tm)
        x = x_new
    return x
```

```python
import functools
import math

import jax
import jax.numpy as jnp
import numpy as np
from jax.experimental import pallas as pl
from jax.experimental.pallas import tpu as pltpu

F32 = jnp.float32
BF16 = jnp.bfloat16

EPS = 1e-6
GRID_W = 64
ROPE_THETA = 10000.0

A_HEADS = 4
A_DK = 128
A_DV = 128
A_QK = A_HEADS * A_DK
A_W = A_HEADS * A_DV
A_CONV = 5
B_HEADS = 4
B_NOPE = 64
B_ROPE = 32
B_V = 64
B_Q_LORA = 192
B_KV_LORA = 128
B_W = B_HEADS * B_V
C_W = 256
C_CONV = 3

LANE = 128
HALO = 16
GDN_CHUNK = 128
VMEM_LIMIT = 48 * 1024 * 1024

S_QKV = 0
S_ZA = 1536
S_ZB = 2048
S_H = 2304
S_BG = 2560
S_CG = 2816
S_ZC = 3072
S_QA = 3328
S_CKV = 3584
S_KPM = 3712
S_KPS = 3840
NP_IN = 3968
CIN_W = 3 * A_QK + C_W


def _dot(a, b):
    return jnp.dot(a, b, preferred_element_type=F32)


def _dot_nt(a, b):
    return jax.lax.dot_general(a, b, (((1,), (1,)), ((), ())), preferred_element_type=F32)


def _silu(x):
    return x * jax.nn.sigmoid(x)


def _params(sem):
    return pltpu.CompilerParams(dimension_semantics=sem, vmem_limit_bytes=VMEM_LIMIT)


def _ada_kernel(c_ref, w_ref, b_ref, o_ref):
    s = _silu(c_ref[...]).astype(BF16)
    o_ref[0] = _dot(s, w_ref[0].astype(BF16)) + b_ref[0]


def _ada(cc, w_ada, b_ada):
    depth, d, n3 = w_ada.shape
    r = cc.shape[0]
    tn = 768
    return pl.pallas_call(
        _ada_kernel,
        grid=(depth, n3 // tn),
        in_specs=[pl.BlockSpec((r, d), lambda l, j: (0, 0)),
                  pl.BlockSpec((1, d, tn), lambda l, j: (l, 0, j)),
                  pl.BlockSpec((1, 1, tn), lambda l, j: (l, 0, j))],
        out_specs=pl.BlockSpec((1, r, tn), lambda l, j: (l, 0, j)),
        out_shape=jax.ShapeDtypeStruct((depth, r, n3), F32),
        compiler_params=_params(("parallel", "parallel")),
        name="ada",
    )(cc, w_ada, b_ada.reshape(depth, 1, n3))


def _inproj_kernel(x_ref, xp_ref, xn_ref, mod_ref, ng_ref, w_ref, qng_ref, wqm_ref, wqs_ref, kvng_ref, wk_ref,
                   wv_ref, tab_ref, gsc_ref, wc_ref,
                   qn_ref, kn_ref, vv_ref, cx_ref, sza_ref, gates_ref, qm_ref, km_ref, vm_ref, szb_ref, *, tm):
    j = pl.program_id(1)
    nj = pl.num_programs(1)

    def normed(xv):
        y = xv * jax.lax.rsqrt(jnp.mean(xv * xv, axis=-1, keepdims=True) + EPS) * ng_ref[...]
        return y * (1.0 + mod_ref[0, 1:2, :]) + mod_ref[0, 0:1, :]

    hn = normed(x_ref[0]).astype(BF16)
    h_prev = jnp.where(j > 0, normed(xp_ref[0]), 0.0).astype(BF16)
    h_next = jnp.where(j < nj - 1, normed(xn_ref[0]), 0.0).astype(BF16)
    h_ext = jnp.concatenate([h_prev, hn, h_next], axis=0)

    def seg(a, b):
        return _dot(hn, w_ref[:, a:b])

    def conv(u_ext, c0, c1):
        pad = A_CONV // 2
        acc = None
        for k in range(A_CONV):
            term = u_ext[HALO - pad + k:HALO - pad + k + tm, :] * wc_ref[k:k + 1, c0:c1]
            acc = term if acc is None else acc + term
        return acc

    blk = 2 * LANE
    for i in range(3 * A_QK // blk):
        y2 = _silu(conv(_dot(h_ext, w_ref[:, i * blk:(i + 1) * blk]), i * blk, (i + 1) * blk))
        for hh in range(2):
            h = 2 * i + hh
            y = y2[:, hh * LANE:(hh + 1) * LANE]
            if h < 2 * A_HEADS:
                y = y * jax.lax.rsqrt(jnp.sum(y * y, axis=-1, keepdims=True) + EPS)
            if h < A_HEADS:
                qn_ref[0, :, h * LANE:(h + 1) * LANE] = (y * (A_DK ** -0.5)).astype(BF16)
            elif h < 2 * A_HEADS:
                kn_ref[0, :, (h - A_HEADS) * LANE:(h - A_HEADS + 1) * LANE] = y.astype(BF16)
            else:
                vv_ref[0, :, (h - 2 * A_HEADS) * LANE:(h - 2 * A_HEADS + 1) * LANE] = y.astype(BF16)

    hc_ext = _dot(h_ext, w_ref[:, S_CG:S_ZC]) * _dot(h_ext, w_ref[:, S_H:S_BG])
    cx_ref[0] = (seg(S_BG, S_CG) * _silu(seg(S_ZC, S_QA)) * conv(hc_ext, 3 * A_QK, CIN_W)).astype(BF16)

    sza_ref[0] = _silu(seg(S_ZA, S_ZB)).astype(BF16)
    szb_ref[0] = _silu(seg(S_ZB, S_H)).astype(BF16)

    tab = tab_ref[...]
    cq, sq = tab[:, 0:LANE], tab[:, LANE:2 * LANE]
    ck, sk = tab[:, 2 * LANE:3 * LANE], tab[:, 3 * LANE:4 * LANE]

    qa = seg(S_QA, S_QA + B_Q_LORA)
    nq = (qa * jax.lax.rsqrt(jnp.mean(qa * qa, axis=-1, keepdims=True) + EPS) * qng_ref[...]).astype(BF16)
    qmain = _dot(nq, wqm_ref[...])
    qswap = _dot(nq, wqs_ref[...])
    for h in range(B_HEADS):
        sl = slice(h * LANE, (h + 1) * LANE)
        qm_ref[0, :, sl] = (qmain[:, sl] * cq + qswap[:, sl] * sq).astype(BF16)

    ckv = seg(S_CKV, S_KPM)
    nkv = (ckv * jax.lax.rsqrt(jnp.mean(ckv * ckv, axis=-1, keepdims=True) + EPS) * kvng_ref[...]).astype(BF16)
    kk = _dot(nkv, wk_ref[...])
    vm_ref[0] = _dot(nkv, wv_ref[...]).astype(BF16)
    kpm = seg(S_KPM, S_KPS)
    kpe = kpm * ck + seg(S_KPS, NP_IN) * sk
    for h in range(B_HEADS):
        sl = slice(h * LANE, (h + 1) * LANE)
        km_ref[0, :, sl] = (kk[:, sl] + kpe).astype(BF16)

    lane = jax.lax.broadcasted_iota(jnp.int32, kpm.shape, 1)
    z = kpm + gsc_ref[1:2, :]
    sp = jnp.maximum(z, 0.0) + jnp.log1p(jnp.exp(-jnp.abs(z)))
    g = -jnp.exp(gsc_ref[0:1, :]) * sp
    gates_ref[0] = jnp.where(lane < 2 * A_HEADS, jax.nn.sigmoid(kpm), jnp.where(lane < 4 * A_HEADS, g, 0.0))


def _layer_spec(a, layer):
    return pl.BlockSpec((None,) + a.shape[1:], lambda i, j: (layer,) + (0,) * (a.ndim - 1))


def _inproj(x, mod, per_batch, layer, ng, w, qng, wqm, wqs, kvng, wk, wv, tab, gsc, wconv, tm):
    b, t, d = x.shape
    tm = min(tm, t)
    r = tm // HALO
    nh = t // HALO
    bm = (lambda i, j: (i, 0, 0)) if per_batch else (lambda i, j: (0, 0, 0))
    tok = lambda width: pl.BlockSpec((1, tm, width), lambda i, j: (i, j, 0))
    full = lambda a: pl.BlockSpec(a.shape, lambda i, j: (0,) * a.ndim)
    lay = lambda a: _layer_spec(a, layer)
    widths = (A_QK, A_QK, A_W, C_W, A_W, LANE, B_HEADS * LANE, B_HEADS * LANE, B_W, B_W)
    dtypes = (BF16, BF16, BF16, BF16, BF16, F32, BF16, BF16, BF16, BF16)
    return pl.pallas_call(
        functools.partial(_inproj_kernel, tm=tm),
        grid=(b, t // tm),
        in_specs=[tok(d),
                  pl.BlockSpec((1, HALO, d), lambda i, j: (i, jnp.maximum(j * r - 1, 0), 0)),
                  pl.BlockSpec((1, HALO, d), lambda i, j: (i, jnp.minimum((j + 1) * r, nh - 1), 0)),
                  pl.BlockSpec((1, 3, d), bm), full(ng), lay(w), full(qng), lay(wqm), lay(wqs),
                  full(kvng), lay(wk), lay(wv),
                  pl.BlockSpec((tm, 4 * LANE), lambda i, j: (j, 0)), full(gsc), full(wconv)],
        out_specs=[tok(wd) for wd in widths],
        out_shape=[jax.ShapeDtypeStruct((b, t, wd), dt) for wd, dt in zip(widths, dtypes)],
        compiler_params=_params(("parallel", "parallel")),
        name="inproj",
    )(x, x, x, mod, ng, w, qng, wqm, wqs, kvng, wk, wv, tab, gsc, wconv)


INV_BASE_LOG2 = 3


def _unit_triangular_inverses(mats, c):
    row = jax.lax.broadcasted_iota(jnp.int32, (c, c), 0)
    col = jax.lax.broadcasted_iota(jnp.int32, (c, c), 1)
    same = lambda k: (row >> k) == (col >> k)
    eye = jnp.where(row == col, 1.0, 0.0)
    base = same(INV_BASE_LOG2)
    a0 = [jnp.where(base, a, 0.0) for a in mats]
    t = [eye - ai for ai in a0]
    apow = [ai.astype(BF16) for ai in a0]
    for _ in range(INV_BASE_LOG2 - 1):
        apow = [_dot(ap, ap).astype(BF16) for ap in apow]
        t = [ti + _dot(ti.astype(BF16), ap) for ti, ap in zip(t, apow)]
    for k in range(INV_BASE_LOG2, int(math.log2(c))):
        coupling = same(k + 1) & jnp.logical_not(same(k))
        e = [jnp.where(coupling, a, 0.0).astype(BF16) for a in mats]
        tb = [ti.astype(BF16) for ti in t]
        te = [_dot(tbi, ei).astype(BF16) for tbi, ei in zip(tb, e)]
        t = [ti - _dot(tei, tbi) for ti, tei, tbi in zip(t, te, tb)]
    return t


def _scan_kernel(qf_ref, kf_ref, vf_ref, gf_ref, qb_ref, kb_ref, vb_ref, gb_ref, s0_ref,
                 o_ref, s_ref, *, c, bb):
    n = pl.program_id(1)
    nc = pl.num_programs(1)

    @pl.when(n == 0)
    def _():
        s_ref[...] = s0_ref[...]
        o_ref[...] = jnp.zeros_like(o_ref)

    row = jax.lax.broadcasted_iota(jnp.int32, (c, c), 0)
    col = jax.lax.broadcasted_iota(jnp.int32, (c, c), 1)
    dirs = ((qf_ref, kf_ref, vf_ref, gf_ref), (qb_ref, kb_ref, vb_ref, gb_ref))
    starts = (pl.multiple_of(n * c, c), pl.multiple_of((nc - 1 - n) * c, c))
    incl = (col <= row, col >= row)
    strict = (col < row, col > row)
    gates, gcb, gcb_t, gl_row = {}, {}, {}, {}
    for e in range(bb):
        for d in range(2):
            g = dirs[d][3][e]
            tri = jnp.where(incl[d], 1.0, 0.0).astype(BF16)
            hi = g.astype(BF16)
            r1 = g - hi.astype(F32)
            mid = r1.astype(BF16)
            lo = (r1 - mid.astype(F32)).astype(BF16)
            gc_all = _dot(tri, hi) + _dot(tri, mid) + _dot(tri, lo)
            gates[e, d] = g
            gcb[e, d] = gc_all
            gcb_t[e, d] = gc_all.T
            gl_row[e, d] = gc_all[c - 1:c, :] if d == 0 else gc_all[0:1, :]

    chains = [(e, d, h) for e in range(bb) for d in range(2) for h in range(A_HEADS)]
    sls = [slice(h * LANE, (h + 1) * LANE) for _, _, h in chains]
    jbs = [A_HEADS * d + h for _, d, h in chains]
    jgs = [2 * A_HEADS + jb for jb in jbs]
    beta = [gates[e, d][:, jb:jb + 1] for (e, d, _), jb in zip(chains, jbs)]
    gc = [gcb[e, d][:, jg:jg + 1] for (e, d, _), jg in zip(chains, jgs)]
    gcr = [gcb_t[e, d][jg:jg + 1, :] for (e, d, _), jg in zip(chains, jgs)]
    gl = [gl_row[e, d][:, jg:jg + 1] for (e, d, _), jg in zip(chains, jgs)]
    q = [dirs[d][0][e, :, sl] for (e, d, _), sl in zip(chains, sls)]
    k = [dirs[d][1][e, :, sl] for (e, d, _), sl in zip(chains, sls)]
    v = [dirs[d][2][e, :, sl] for (e, d, _), sl in zip(chains, sls)]
    decay = [jnp.where(incl[d], jnp.exp(jnp.where(incl[d], gci - gcri, 0.0)), 0.0)
             for (_, d, _), gci, gcri in zip(chains, gc, gcr)]
    kbeta = [ki.astype(F32) * bi for ki, bi in zip(k, beta)]
    egc = [jnp.exp(gci) for gci in gc]
    a = [jnp.where(strict[d], _dot_nt(kbi.astype(BF16), ki) * di, 0.0)
         for (_, d, _), kbi, ki, di in zip(chains, kbeta, k, decay)]
    qk = [(_dot_nt(qi, ki) * di).astype(BF16) for qi, ki, di in zip(q, k, decay)]
    rhs = [jnp.concatenate([vi.astype(F32) * bi, kbi * ei], axis=1).astype(BF16)
           for vi, bi, kbi, ei in zip(v, beta, kbeta, egc)]
    kd_t = [(ki.astype(F32) * jnp.exp(gli - gci)).T.astype(BF16) for ki, gli, gci in zip(k, gl, gc)]
    tinv = [ti.astype(BF16) for ti in _unit_triangular_inverses(a, c)]
    uw = [_dot(ti, ri) for ti, ri in zip(tinv, rhs)]
    qdk = [jnp.concatenate([(qi.astype(F32) * ei).astype(BF16), qki], axis=1) for qi, ei, qki in zip(q, egc, qk)]
    s = [s_ref[e, d, h] for e, d, h in chains]
    sb = [si.astype(BF16) for si in s]
    v_new = [(uwi[:, :A_DV] - _dot(uwi[:, A_DV:].astype(BF16), sbi)).astype(BF16) for uwi, sbi in zip(uw, sb)]
    o = [_dot(qi, jnp.concatenate([sbi, vi], axis=0)) for qi, sbi, vi in zip(qdk, sb, v_new)]
    for (e, d, h), sl, oi in zip(chains, sls, o):
        o_ref[e, pl.ds(starts[d], c), sl] += oi
    for (e, d, h), gli, si, kdi, vi in zip(chains, gl, s, kd_t, v_new):
        s_ref[e, d, h] = jnp.exp(gli) * si + _dot(kdi, vi)


def _scan(qn, kn, vv, gates, s0, bb):
    b, t, _ = qn.shape
    c = GDN_CHUNK
    n = t // c
    assert b % bb == 0
    fwd = lambda width: pl.BlockSpec((bb, c, width), lambda i, j: (i, j, 0))
    bwd = lambda width: pl.BlockSpec((bb, c, width), lambda i, j: (i, n - 1 - j, 0))
    st = pl.BlockSpec((bb, 2, A_HEADS, A_DK, A_DV), lambda i, j: (i, 0, 0, 0, 0))
    return pl.pallas_call(
        functools.partial(_scan_kernel, c=c, bb=bb),
        grid=(b // bb, n),
        in_specs=[fwd(A_QK), fwd(A_QK), fwd(A_W), fwd(LANE), bwd(A_QK), bwd(A_QK), bwd(A_W), bwd(LANE), st],
        out_specs=[pl.BlockSpec((bb, t, A_W), lambda i, j: (i, 0, 0)), st],
        out_shape=[jax.ShapeDtypeStruct((b, t, A_W), F32), jax.ShapeDtypeStruct(s0.shape, F32)],
        compiler_params=_params(("parallel", "arbitrary")),
        name="gdn_scan",
    )(qn, kn, vv, gates, qn, kn, vv, gates, s0)


def _attn_kernel(*refs, n_src):
    q_ref = refs[0]
    kv_refs = refs[1:1 + 2 * n_src]
    szb_ref = refs[1 + 2 * n_src]
    o_ref = refs[2 + 2 * n_src]
    outs = []
    for h in range(B_HEADS):
        qh = q_ref[0, :, h * LANE:(h + 1) * LANE]
        scores = [_dot_nt(qh, kv_refs[2 * i][0, :, h * LANE:(h + 1) * LANE]) for i in range(n_src)]
        m = scores[0].max(axis=-1, keepdims=True)
        for s in scores[1:]:
            m = jnp.maximum(m, s.max(axis=-1, keepdims=True))
        l = None
        acc = None
        for i, s in enumerate(scores):
            p = jnp.exp(s - m)
            ps = p.sum(axis=-1, keepdims=True)
            pv = _dot(p.astype(BF16), kv_refs[2 * i + 1][0, :, h * B_V:(h + 1) * B_V])
            l = ps if l is None else l + ps
            acc = pv if acc is None else acc + pv
        outs.append(acc / l)
    o = jnp.concatenate(outs, axis=-1)
    o_ref[0] = (o * szb_ref[0].astype(F32)).astype(BF16)


def _attn(q, kvs, szb, tq):
    b, t, _ = q.shape
    tq = min(tq, t)
    tok = lambda width: pl.BlockSpec((1, tq, width), lambda i, j: (i, j, 0))
    in_specs = [tok(B_HEADS * LANE)]
    args = [q]
    for k, v in kvs:
        in_specs.append(pl.BlockSpec((1,) + k.shape[1:], lambda i, j: (i, 0, 0)))
        in_specs.append(pl.BlockSpec((1,) + v.shape[1:], lambda i, j: (i, 0, 0)))
        args += [k, v]
    in_specs.append(tok(B_W))
    args.append(szb)
    return pl.pallas_call(
        functools.partial(_attn_kernel, n_src=len(kvs)),
        grid=(b, t // tq),
        in_specs=in_specs,
        out_specs=tok(B_W),
        out_shape=jax.ShapeDtypeStruct((b, t, B_W), BF16),
        compiler_params=_params(("parallel", "parallel")),
        name="attn",
    )(*args)


def _outproj_kernel(x_ref, oa_ref, sza_ref, bx_ref, cx_ref, gng_ref, mod_ref, w_ref, fg_ref, o_ref, *, final):
    o = oa_ref[0]
    sza = sza_ref[0].astype(F32)
    acc = None
    for h in range(A_HEADS):
        sl = slice(h * LANE, (h + 1) * LANE)
        oh = o[:, sl]
        ah = oh * jax.lax.rsqrt(jnp.mean(oh * oh, axis=-1, keepdims=True) + EPS) * gng_ref[...]
        term = _dot((ah * sza[:, sl]).astype(BF16), w_ref[sl, :])
        acc = term if acc is None else acc + term
    acc = acc + _dot(bx_ref[0], w_ref[A_W:A_W + B_W, :]) + _dot(cx_ref[0], w_ref[A_W + B_W:, :])
    xn = x_ref[0] + mod_ref[0, 2:3, :] * acc
    if final:
        xn = xn * jax.lax.rsqrt(jnp.mean(xn * xn, axis=-1, keepdims=True) + EPS) * fg_ref[...]
    o_ref[0] = xn


def _outproj(x, oa, sza, bx, cx, gng, mod, per_batch, layer, w, fg, final, tm):
    b, t, d = x.shape
    tm = min(tm, t)
    bm = (lambda i, j: (i, 0, 0)) if per_batch else (lambda i, j: (0, 0, 0))
    tok = lambda width: pl.BlockSpec((1, tm, width), lambda i, j: (i, j, 0))
    full = lambda a: pl.BlockSpec(a.shape, lambda i, j: (0,) * a.ndim)
    return pl.pallas_call(
        functools.partial(_outproj_kernel, final=final),
        grid=(b, t // tm),
        in_specs=[tok(d), tok(A_W), tok(A_W), tok(B_W), tok(C_W), full(gng),
                  pl.BlockSpec((1, 3, d), bm), _layer_spec(w, layer), full(fg)],
        out_specs=tok(d),
        out_shape=jax.ShapeDtypeStruct((b, t, d), F32),
        compiler_params=_params(("parallel", "parallel")),
        name="outproj",
    )(x, oa, sza, bx, cx, gng, mod, w, fg)


def _deinterleave(w32):
    even, odd = w32[..., 0::2], w32[..., 1::2]
    return jnp.concatenate([even, odd], axis=-1), jnp.concatenate([odd, even], axis=-1)


def _layout_w_in(w):
    w = w.astype(BF16)
    z = lambda n: jnp.zeros(w.shape[:-1] + (n,), w.dtype)
    o_b, o_a, o_qa, o_kv, o_zb, o_h, o_bg, o_cg, o_zc = 2048, 2056, 2064, 2256, 2416, 2672, 2928, 3184, 3440
    kp_main, kp_swap = _deinterleave(w[..., o_kv + B_KV_LORA:o_zb])
    cols = [w[..., 0:2048], w[..., o_zb:o_zc + C_W],
            w[..., o_qa:o_kv], z(S_CKV - S_QA - B_Q_LORA), w[..., o_kv:o_kv + B_KV_LORA],
            w[..., o_b:o_qa], z(B_NOPE - 4 * A_HEADS), kp_main, z(LANE - B_NOPE - B_ROPE),
            z(B_NOPE), kp_swap, z(LANE - B_NOPE - B_ROPE)]
    out = jnp.concatenate(cols, axis=-1)
    assert out.shape[-1] == NP_IN
    return out


def _layout_w_qb(w):
    w = w.astype(BF16)
    z = lambda n: jnp.zeros(w.shape[:-1] + (n,), w.dtype)
    hd = B_NOPE + B_ROPE
    main, swap = [], []
    for h in range(B_HEADS):
        rm, rs = _deinterleave(w[..., h * hd + B_NOPE:(h + 1) * hd])
        main += [w[..., h * hd:h * hd + B_NOPE], rm, z(LANE - hd)]
        swap += [z(B_NOPE), rs, z(LANE - hd)]
    return jnp.concatenate(main, axis=-1), jnp.concatenate(swap, axis=-1)


def _layout_w_kvb(w):
    w = w.astype(BF16)
    hd = B_NOPE + B_V
    wk, wv = [], []
    for h in range(B_HEADS):
        wk += [w[..., h * hd:h * hd + B_NOPE], jnp.zeros(w.shape[:-1] + (LANE - B_NOPE,), w.dtype)]
        wv.append(w[..., h * hd + B_NOPE:(h + 1) * hd])
    return jnp.concatenate(wk, axis=-1), jnp.concatenate(wv, axis=-1)


def _rope_tables(t, rotate):
    half = B_ROPE // 2
    if rotate:
        pos = np.arange(t)
        n_freq = B_ROPE // 4
        inv_freq = ROPE_THETA ** (-np.arange(n_freq, dtype=np.float64) / n_freq)
        ang = np.concatenate([(pos // GRID_W)[:, None] * inv_freq, (pos % GRID_W)[:, None] * inv_freq], axis=-1)
        cos, sin = np.cos(ang).astype(np.float32), np.sin(ang).astype(np.float32)
    else:
        cos, sin = np.ones((t, half), np.float32), np.zeros((t, half), np.float32)
    z = lambda n: np.zeros((t, n), np.float32)
    tail = LANE - B_NOPE - B_ROPE
    scale = np.float32((B_NOPE + B_ROPE) ** -0.5)
    cq = np.concatenate([np.ones((t, B_NOPE), np.float32), cos, cos, z(tail)], axis=1) * scale
    sq = np.concatenate([z(B_NOPE), -sin, sin, z(tail)], axis=1) * scale
    ck = np.concatenate([z(B_NOPE), cos, cos, z(tail)], axis=1)
    sk = np.concatenate([z(B_NOPE), -sin, sin, z(tail)], axis=1)
    return jnp.asarray(np.concatenate([cq, sq, ck, sk], axis=1), dtype=F32)


def kernel(x, c, ctx, c_ctx, w_ada, b_ada, norm_g, w_in, gdn_conv, gdn_a_log, gdn_dt_bias, gdn_norm_g,
           mla_q_norm_g, mla_w_qb, mla_kv_norm_g, mla_w_kvb, conv_w, w_out, final_norm_g):
    bsz, t, d = x.shape
    t_ctx = ctx.shape[1]
    depth = w_ada.shape[0]
    assert t % GDN_CHUNK == 0 and t_ctx % GDN_CHUNK == 0 and t % GRID_W == 0
    tm = 512
    bb = 2 if bsz % 2 == 0 else 1

    rows = -(-(bsz + 1) // 8) * 8
    cc = jnp.concatenate([c, c_ctx[None, :], jnp.zeros((rows - bsz - 1, d), F32)], axis=0)
    mod = _ada(cc, w_ada, b_ada)

    tab_x = _rope_tables(t, True)
    tab_c = _rope_tables(t_ctx, False)
    fg = final_norm_g.reshape(1, d)
    w_all = _layout_w_in(w_in)
    wqm_all, wqs_all = _layout_w_qb(mla_w_qb)
    wk_all, wv_all = _layout_w_kvb(mla_w_kvb)
    wo_all = w_out.astype(BF16)

    for l in range(depth):
        mod_x = mod[l, :bsz].reshape(bsz, 3, d)
        mod_c = mod[l, bsz:bsz + 1].reshape(1, 3, d)
        ng = norm_g[l].reshape(1, d)
        qng = mla_q_norm_g[l].reshape(1, B_Q_LORA)
        kvng = mla_kv_norm_g[l].reshape(1, B_KV_LORA)
        gpad = lambda a: jnp.concatenate([jnp.zeros((2 * A_HEADS,), F32), a.reshape(-1),
                                          jnp.zeros((LANE - 4 * A_HEADS,), F32)])
        gsc = jnp.stack([gpad(gdn_a_log[l]), gpad(gdn_dt_bias[l])] + [jnp.zeros((LANE,), F32)] * 6)
        wconv = jnp.concatenate(
            [gdn_conv[l], jnp.pad(conv_w[l], (((A_CONV - C_CONV) // 2,) * 2, (0, 0)))], axis=1)
        wconv = jnp.concatenate([wconv, jnp.zeros((8 - A_CONV, CIN_W), F32)], axis=0)
        gng = gdn_norm_g[l].reshape(1, A_DV)
        last = l == depth - 1

        shared = (l, ng, w_all, qng, wqm_all, wqs_all, kvng, wk_all, wv_all)
        (qn_c, kn_c, vv_c, cx_c, sza_c, gates_c, qm_c, km_c, vm_c, szb_c) = _inproj(
            ctx, mod_c, False, *shared, tab_c, gsc, wconv, tm)
        (qn_x, kn_x, vv_x, cx_x, sza_x, gates_x, qm_x, km_x, vm_x, szb_x) = _inproj(
            x, mod_x, True, *shared, tab_x, gsc, wconv, tm)

        s_zero = jnp.zeros((bsz, 2, A_HEADS, A_DK, A_DV), F32)
        oa_c, s_ctx = _scan(qn_c, kn_c, vv_c, gates_c, s_zero, bb)
        oa_x, _ = _scan(qn_x, kn_x, vv_x, gates_x, s_ctx, bb)

        bx_x = _attn(qm_x, [(km_x, vm_x), (km_c, vm_c)], szb_x, tm)
        x_new = _outproj(x, oa_x, sza_x, bx_x, cx_x, gng, mod_x, True, l, wo_all, fg, last, tm)
        if not last:
            bx_c = _attn(qm_c, [(km_c, vm_c)], szb_c, tm)
            ctx = _outproj(ctx, oa_c, sza_c, bx_c, cx_c, gng, mod_c, False, l, wo_all, fg, False, tm)
        x = x_new
    return x
```

```python
import functools
import math

import jax
import jax.numpy as jnp
import numpy as np
from jax.experimental import pallas as pl
from jax.experimental.pallas import tpu as pltpu

F32 = jnp.float32
BF16 = jnp.bfloat16

EPS = 1e-6
GRID_W = 64
ROPE_THETA = 10000.0

A_HEADS = 4
A_DK = 128
A_DV = 128
A_QK = A_HEADS * A_DK
A_W = A_HEADS * A_DV
A_CONV = 5
B_HEADS = 4
B_NOPE = 64
B_ROPE = 32
B_V = 64
B_Q_LORA = 192
B_KV_LORA = 128
B_W = B_HEADS * B_V
B_VA = B_V + 16
B_WA = B_HEADS * B_VA
C_W = 256
C_CONV = 3

LANE = 128
HALO = 16
GDN_CHUNK = 128
ATTN_KEY_BLOCK = 256
VMEM_LIMIT = 48 * 1024 * 1024

S_QKV = 0
S_ZA = 1536
S_ZB = 2048
S_H = 2304
S_BG = 2560
S_CG = 2816
S_ZC = 3072
S_QA = 3328
S_CKV = 3584
S_KPM = 3712
S_KPS = 3840
NP_IN = 3968
CIN_W = 3 * A_QK + C_W


def _dot(a, b):
    return jnp.dot(a, b, preferred_element_type=F32)


def _dot_nt(a, b):
    return jax.lax.dot_general(a, b, (((1,), (1,)), ((), ())), preferred_element_type=F32)


def _silu(x):
    return x * jax.nn.sigmoid(x)


def _params(sem):
    return pltpu.CompilerParams(dimension_semantics=sem, vmem_limit_bytes=VMEM_LIMIT)


def _ada_kernel(c_ref, w_ref, b_ref, o_ref):
    s = _silu(c_ref[...]).astype(BF16)
    o_ref[0] = _dot(s, w_ref[0].astype(BF16)) + b_ref[0]


def _ada(cc, w_ada, b_ada):
    depth, d, n3 = w_ada.shape
    r = cc.shape[0]
    tn = 768
    return pl.pallas_call(
        _ada_kernel,
        grid=(depth, n3 // tn),
        in_specs=[pl.BlockSpec((r, d), lambda l, j: (0, 0)),
                  pl.BlockSpec((1, d, tn), lambda l, j: (l, 0, j)),
                  pl.BlockSpec((1, 1, tn), lambda l, j: (l, 0, j))],
        out_specs=pl.BlockSpec((1, r, tn), lambda l, j: (l, 0, j)),
        out_shape=jax.ShapeDtypeStruct((depth, r, n3), F32),
        compiler_params=_params(("parallel", "parallel")),
        name="ada",
    )(cc, w_ada, b_ada.reshape(depth, 1, n3))


def _inproj_kernel(x_ref, xp_ref, xn_ref, mod_ref, ng_ref, w_ref, qng_ref, wqm_ref, wqs_ref, kvng_ref, wk_ref,
                   wv_ref, tab_ref, gsc_ref, wc_ref,
                   qn_ref, kn_ref, vv_ref, cx_ref, sza_ref, gates_ref, qm_ref, km_ref, vm_ref, szb_ref, *, tm):
    j = pl.program_id(1)
    nj = pl.num_programs(1)

    def normed(xv):
        y = xv * jax.lax.rsqrt(jnp.mean(xv * xv, axis=-1, keepdims=True) + EPS) * ng_ref[...]
        return y * (1.0 + mod_ref[0, 1:2, :]) + mod_ref[0, 0:1, :]

    hn = normed(x_ref[0]).astype(BF16)
    h_prev = jnp.where(j > 0, normed(xp_ref[0]), 0.0).astype(BF16)
    h_next = jnp.where(j < nj - 1, normed(xn_ref[0]), 0.0).astype(BF16)
    h_ext = jnp.concatenate([h_prev, hn, h_next], axis=0)

    def seg(a, b):
        return _dot(hn, w_ref[:, a:b])

    def conv(u_ext, c0, c1):
        pad = A_CONV // 2
        rows = u_ext.shape[0]
        acc = None
        for k in range(A_CONV):
            shifted = u_ext if k == pad else pltpu.roll(u_ext, (pad - k) % rows, 0)
            term = shifted[HALO:HALO + tm, :] * wc_ref[k:k + 1, c0:c1]
            acc = term if acc is None else acc + term
        return acc

    blk = 2 * LANE
    for i in range(3 * A_QK // blk):
        y2 = _silu(conv(_dot(h_ext, w_ref[:, i * blk:(i + 1) * blk]), i * blk, (i + 1) * blk))
        for hh in range(2):
            h = 2 * i + hh
            y = y2[:, hh * LANE:(hh + 1) * LANE]
            if h < 2 * A_HEADS:
                y = y * jax.lax.rsqrt(jnp.sum(y * y, axis=-1, keepdims=True) + EPS)
            if h < A_HEADS:
                qn_ref[0, :, h * LANE:(h + 1) * LANE] = (y * (A_DK ** -0.5)).astype(BF16)
            elif h < 2 * A_HEADS:
                kn_ref[0, :, (h - A_HEADS) * LANE:(h - A_HEADS + 1) * LANE] = y.astype(BF16)
            else:
                vv_ref[0, :, (h - 2 * A_HEADS) * LANE:(h - 2 * A_HEADS + 1) * LANE] = y.astype(BF16)

    hc_ext = _dot(h_ext, w_ref[:, S_CG:S_ZC]) * _dot(h_ext, w_ref[:, S_H:S_BG])
    cx_ref[0] = (seg(S_BG, S_CG) * _silu(seg(S_ZC, S_QA)) * conv(hc_ext, 3 * A_QK, CIN_W)).astype(BF16)

    sza_ref[0] = _silu(seg(S_ZA, S_ZB)).astype(BF16)
    szb_ref[0] = _silu(seg(S_ZB, S_H)).astype(BF16)

    tab = tab_ref[...]
    cq, sq = tab[:, 0:LANE], tab[:, LANE:2 * LANE]
    ck, sk = tab[:, 2 * LANE:3 * LANE], tab[:, 3 * LANE:4 * LANE]

    qa = seg(S_QA, S_QA + B_Q_LORA)
    nq = (qa * jax.lax.rsqrt(jnp.mean(qa * qa, axis=-1, keepdims=True) + EPS) * qng_ref[...]).astype(BF16)
    qmain = _dot(nq, wqm_ref[...])
    qswap = _dot(nq, wqs_ref[...])
    for h in range(B_HEADS):
        sl = slice(h * LANE, (h + 1) * LANE)
        qm_ref[0, :, sl] = (qmain[:, sl] * cq + qswap[:, sl] * sq).astype(BF16)

    ckv = seg(S_CKV, S_KPM)
    nkv = (ckv * jax.lax.rsqrt(jnp.mean(ckv * ckv, axis=-1, keepdims=True) + EPS) * kvng_ref[...]).astype(BF16)
    kk = _dot(nkv, wk_ref[...])
    vt = _dot_nt(wv_ref[...], nkv)
    vrow = jax.lax.broadcasted_iota(jnp.int32, vt.shape, 0)
    ones_row = functools.reduce(jnp.logical_or, [vrow == h * B_VA + B_V for h in range(B_HEADS)])
    vm_ref[0] = jnp.where(ones_row, 1.0, vt).astype(BF16)
    kpm = seg(S_KPM, S_KPS)
    kpe = kpm * ck + seg(S_KPS, NP_IN) * sk
    for h in range(B_HEADS):
        sl = slice(h * LANE, (h + 1) * LANE)
        km_ref[0, :, sl] = (kk[:, sl] + kpe).astype(BF16)

    lane = jax.lax.broadcasted_iota(jnp.int32, kpm.shape, 1)
    z = kpm + gsc_ref[1:2, :]
    sp = jnp.maximum(z, 0.0) + jnp.log1p(jnp.exp(-jnp.abs(z)))
    g = -jnp.exp(gsc_ref[0:1, :]) * sp
    gates_ref[0] = jnp.where(lane < 2 * A_HEADS, jax.nn.sigmoid(kpm), jnp.where(lane < 4 * A_HEADS, g, 0.0))


def _layer_spec(a, layer):
    return pl.BlockSpec((None,) + a.shape[1:], lambda i, j: (layer,) + (0,) * (a.ndim - 1))


def _inproj(x, mod, per_batch, layer, ng, w, qng, wqm, wqs, kvng, wk, wv, tab, gsc, wconv, tm):
    b, t, d = x.shape
    tm = min(tm, t)
    r = tm // HALO
    nh = t // HALO
    bm = (lambda i, j: (i, 0, 0)) if per_batch else (lambda i, j: (0, 0, 0))
    tok = lambda width: pl.BlockSpec((1, tm, width), lambda i, j: (i, j, 0))
    full = lambda a: pl.BlockSpec(a.shape, lambda i, j: (0,) * a.ndim)
    lay = lambda a: _layer_spec(a, layer)
    widths = (A_QK, A_QK, A_W, C_W, A_W, LANE, B_HEADS * LANE, B_HEADS * LANE, B_W, B_W)
    dtypes = (BF16, BF16, BF16, BF16, BF16, F32, BF16, BF16, BF16, BF16)
    V_OUT = 8
    return pl.pallas_call(
        functools.partial(_inproj_kernel, tm=tm),
        grid=(b, t // tm),
        in_specs=[tok(d),
                  pl.BlockSpec((1, HALO, d), lambda i, j: (i, jnp.maximum(j * r - 1, 0), 0)),
                  pl.BlockSpec((1, HALO, d), lambda i, j: (i, jnp.minimum((j + 1) * r, nh - 1), 0)),
                  pl.BlockSpec((1, 3, d), bm), full(ng), lay(w), full(qng), lay(wqm), lay(wqs),
                  full(kvng), lay(wk), lay(wv),
                  pl.BlockSpec((tm, 4 * LANE), lambda i, j: (j, 0)), full(gsc), full(wconv)],
        out_specs=[pl.BlockSpec((1, B_WA, tm), lambda i, j: (i, 0, j)) if n == V_OUT else tok(wd)
                   for n, wd in enumerate(widths)],
        out_shape=[jax.ShapeDtypeStruct((b, B_WA, t) if n == V_OUT else (b, t, wd), dt)
                   for n, (wd, dt) in enumerate(zip(widths, dtypes))],
        compiler_params=_params(("parallel", "parallel")),
        name="inproj",
    )(x, x, x, mod, ng, w, qng, wqm, wqs, kvng, wk, wv, tab, gsc, wconv)


INV_BASE_LOG2 = 3


def _unit_triangular_inverses(mats, c):
    row = jax.lax.broadcasted_iota(jnp.int32, (c, c), 0)
    col = jax.lax.broadcasted_iota(jnp.int32, (c, c), 1)
    same = lambda k: (row >> k) == (col >> k)
    eye = jnp.where(row == col, 1.0, 0.0)
    base = same(INV_BASE_LOG2)
    a0 = [jnp.where(base, a, 0.0) for a in mats]
    t = [eye - ai for ai in a0]
    apow = [ai.astype(BF16) for ai in a0]
    for _ in range(INV_BASE_LOG2 - 1):
        apow = [_dot(ap, ap).astype(BF16) for ap in apow]
        t = [ti + _dot(ti.astype(BF16), ap) for ti, ap in zip(t, apow)]
    for k in range(INV_BASE_LOG2, int(math.log2(c))):
        coupling = same(k + 1) & jnp.logical_not(same(k))
        e = [jnp.where(coupling, a, 0.0).astype(BF16) for a in mats]
        tb = [ti.astype(BF16) for ti in t]
        te = [_dot(tbi, ei).astype(BF16) for tbi, ei in zip(tb, e)]
        t = [ti - _dot(tei, tbi) for ti, tei, tbi in zip(t, te, tb)]
    return t


def _scan_kernel(qf_ref, kf_ref, vf_ref, gf_ref, qb_ref, kb_ref, vb_ref, gb_ref, s0_ref,
                 of_ref, ob_ref, s_ref, *, c, bb):
    n = pl.program_id(1)

    @pl.when(n == 0)
    def _():
        s_ref[...] = s0_ref[...]

    row = jax.lax.broadcasted_iota(jnp.int32, (c, c), 0)
    col = jax.lax.broadcasted_iota(jnp.int32, (c, c), 1)
    dirs = ((qf_ref, kf_ref, vf_ref, gf_ref), (qb_ref, kb_ref, vb_ref, gb_ref))
    o_refs = (of_ref, ob_ref)
    incl = (col <= row, col >= row)
    strict = (col < row, col > row)
    gates, gcb, gcb_t, gl_row = {}, {}, {}, {}
    for e in range(bb):
        for d in range(2):
            g = dirs[d][3][e]
            tri = jnp.where(incl[d], 1.0, 0.0).astype(BF16)
            hi = g.astype(BF16)
            r1 = g - hi.astype(F32)
            mid = r1.astype(BF16)
            lo = (r1 - mid.astype(F32)).astype(BF16)
            gc_all = _dot(tri, hi) + _dot(tri, mid) + _dot(tri, lo)
            gates[e, d] = g
            gcb[e, d] = gc_all
            gcb_t[e, d] = gc_all.T
            gl_row[e, d] = gc_all[c - 1:c, :] if d == 0 else gc_all[0:1, :]

    chains = [(e, d, h) for e in range(bb) for d in range(2) for h in range(A_HEADS)]
    sls = [slice(h * LANE, (h + 1) * LANE) for _, _, h in chains]
    jbs = [A_HEADS * d + h for _, d, h in chains]
    jgs = [2 * A_HEADS + jb for jb in jbs]
    beta = [gates[e, d][:, jb:jb + 1] for (e, d, _), jb in zip(chains, jbs)]
    gc = [gcb[e, d][:, jg:jg + 1] for (e, d, _), jg in zip(chains, jgs)]
    gcr = [gcb_t[e, d][jg:jg + 1, :] for (e, d, _), jg in zip(chains, jgs)]
    gl = [gl_row[e, d][:, jg:jg + 1] for (e, d, _), jg in zip(chains, jgs)]
    q = [dirs[d][0][e, :, sl] for (e, d, _), sl in zip(chains, sls)]
    k = [dirs[d][1][e, :, sl] for (e, d, _), sl in zip(chains, sls)]
    v = [dirs[d][2][e, :, sl] for (e, d, _), sl in zip(chains, sls)]
    decay = [jnp.where(incl[d], jnp.exp(jnp.where(incl[d], gci - gcri, 0.0)), 0.0)
             for (_, d, _), gci, gcri in zip(chains, gc, gcr)]
    kbeta = [ki.astype(F32) * bi for ki, bi in zip(k, beta)]
    egc = [jnp.exp(gci) for gci in gc]
    a = [jnp.where(strict[d], _dot_nt(kbi.astype(BF16), ki) * di, 0.0)
         for (_, d, _), kbi, ki, di in zip(chains, kbeta, k, decay)]
    qk = [(_dot_nt(qi, ki) * di).astype(BF16) for qi, ki, di in zip(q, k, decay)]
    rhs = [jnp.concatenate([vi.astype(F32) * bi, kbi * ei], axis=1).astype(BF16)
           for vi, bi, kbi, ei in zip(v, beta, kbeta, egc)]
    kd_t = [(ki.astype(F32) * jnp.exp(gli - gci)).T.astype(BF16) for ki, gli, gci in zip(k, gl, gc)]
    tinv = [ti.astype(BF16) for ti in _unit_triangular_inverses(a, c)]
    uw = [_dot(ti, ri) for ti, ri in zip(tinv, rhs)]
    qdk = [jnp.concatenate([(qi.astype(F32) * ei).astype(BF16), qki], axis=1) for qi, ei, qki in zip(q, egc, qk)]
    s = [s_ref[e, d, h] for e, d, h in chains]
    sb = [si.astype(BF16) for si in s]
    v_new = [(uwi[:, :A_DV] - _dot(uwi[:, A_DV:].astype(BF16), sbi)).astype(BF16) for uwi, sbi in zip(uw, sb)]
    o = [_dot(qi, jnp.concatenate([sbi, vi], axis=0)) for qi, sbi, vi in zip(qdk, sb, v_new)]
    for (e, d, h), sl, oi in zip(chains, sls, o):
        o_refs[d][e, :, sl] = oi.astype(BF16)
    for (e, d, h), gli, si, kdi, vi in zip(chains, gl, s, kd_t, v_new):
        s_ref[e, d, h] = jnp.exp(gli) * si + _dot(kdi, vi)


def _scan(qn, kn, vv, gates, s0, bb):
    b, t, _ = qn.shape
    c = GDN_CHUNK
    n = t // c
    assert b % bb == 0
    fwd = lambda width: pl.BlockSpec((bb, c, width), lambda i, j: (i, j, 0))
    bwd = lambda width: pl.BlockSpec((bb, c, width), lambda i, j: (i, n - 1 - j, 0))
    st = pl.BlockSpec((bb, 2, A_HEADS, A_DK, A_DV), lambda i, j: (i, 0, 0, 0, 0))
    return pl.pallas_call(
        functools.partial(_scan_kernel, c=c, bb=bb),
        grid=(b // bb, n),
        in_specs=[fwd(A_QK), fwd(A_QK), fwd(A_W), fwd(LANE), bwd(A_QK), bwd(A_QK), bwd(A_W), bwd(LANE), st],
        out_specs=[fwd(A_W), bwd(A_W), st],
        out_shape=[jax.ShapeDtypeStruct((b, t, A_W), BF16), jax.ShapeDtypeStruct((b, t, A_W), BF16),
                   jax.ShapeDtypeStruct(s0.shape, F32)],
        compiler_params=_params(("parallel", "arbitrary")),
        name="gdn_scan",
    )(qn, kn, vv, gates, qn, kn, vv, gates, s0)


def _attn_kernel(*refs, n_src):
    q_ref = refs[0]
    kv_refs = refs[1:1 + 2 * n_src]
    szb_ref = refs[1 + 2 * n_src]
    o_ref = refs[2 + 2 * n_src]
    heads = range(B_HEADS)
    qh = [q_ref[0, :, h * LANE:(h + 1) * LANE] for h in heads]
    blocks = [(kv_refs[2 * i], kv_refs[2 * i + 1], slice(j * ATTN_KEY_BLOCK, (j + 1) * ATTN_KEY_BLOCK))
              for i in range(n_src) for j in range(kv_refs[2 * i].shape[1] // ATTN_KEY_BLOCK)]

    def scores(blk):
        k_ref, _, ks = blk
        return [_dot_nt(k_ref[0, ks, h * LANE:(h + 1) * LANE], qh[h]) for h in heads]

    m = acc = None
    s_next = scores(blocks[0])
    for n, (_, vt_ref, ks) in enumerate(blocks):
        s, s_next = s_next, (scores(blocks[n + 1]) if n + 1 < len(blocks) else None)
        mb = [si.max(axis=0, keepdims=True) for si in s]
        m_new = mb if m is None else [jnp.maximum(mi, mbi) for mi, mbi in zip(m, mb)]
        p = [jnp.exp2(si - mn).astype(BF16) for si, mn in zip(s, m_new)]
        pv = [_dot(vt_ref[0, h * B_VA:(h + 1) * B_VA, ks], p[h]) for h in heads]
        if m is None:
            acc = pv
        else:
            acc = [jnp.exp2(mi - mn) * ai + pvi for mi, mn, ai, pvi in zip(m, m_new, acc, pv)]
        m = m_new
    o = jnp.concatenate([a[:B_V] / a[B_V:B_V + 1] for a in acc], axis=0).T
    o_ref[0] = (o * szb_ref[0].astype(F32)).astype(BF16)


def _attn(q, kvs, szb, tq):
    b, t, _ = q.shape
    tq = min(tq, t)
    tok = lambda width: pl.BlockSpec((1, tq, width), lambda i, j: (i, j, 0))
    in_specs = [tok(B_HEADS * LANE)]
    args = [q]
    for k, v in kvs:
        in_specs.append(pl.BlockSpec((1,) + k.shape[1:], lambda i, j: (i, 0, 0)))
        in_specs.append(pl.BlockSpec((1,) + v.shape[1:], lambda i, j: (i, 0, 0)))
        args += [k, v]
    in_specs.append(tok(B_W))
    args.append(szb)
    return pl.pallas_call(
        functools.partial(_attn_kernel, n_src=len(kvs)),
        grid=(b, t // tq),
        in_specs=in_specs,
        out_specs=tok(B_W),
        out_shape=jax.ShapeDtypeStruct((b, t, B_W), BF16),
        compiler_params=_params(("parallel", "parallel")),
        name="attn",
    )(*args)


def _outproj_kernel(x_ref, of_ref, ob_ref, sza_ref, bx_ref, cx_ref, gng_ref, mod_ref, w_ref, fg_ref, o_ref,
                    *, final):
    o = of_ref[0].astype(F32) + ob_ref[0].astype(F32)
    sza = sza_ref[0].astype(F32)
    acc = None
    for h in range(A_HEADS):
        sl = slice(h * LANE, (h + 1) * LANE)
        oh = o[:, sl]
        ah = oh * jax.lax.rsqrt(jnp.mean(oh * oh, axis=-1, keepdims=True) + EPS) * gng_ref[...]
        term = _dot((ah * sza[:, sl]).astype(BF16), w_ref[sl, :])
        acc = term if acc is None else acc + term
    acc = acc + _dot(bx_ref[0], w_ref[A_W:A_W + B_W, :]) + _dot(cx_ref[0], w_ref[A_W + B_W:, :])
    xn = x_ref[0] + mod_ref[0, 2:3, :] * acc
    if final:
        xn = xn * jax.lax.rsqrt(jnp.mean(xn * xn, axis=-1, keepdims=True) + EPS) * fg_ref[...]
    o_ref[0] = xn


def _outproj(x, of, ob, sza, bx, cx, gng, mod, per_batch, layer, w, fg, final, tm):
    b, t, d = x.shape
    tm = min(tm, t)
    bm = (lambda i, j: (i, 0, 0)) if per_batch else (lambda i, j: (0, 0, 0))
    tok = lambda width: pl.BlockSpec((1, tm, width), lambda i, j: (i, j, 0))
    full = lambda a: pl.BlockSpec(a.shape, lambda i, j: (0,) * a.ndim)
    return pl.pallas_call(
        functools.partial(_outproj_kernel, final=final),
        grid=(b, t // tm),
        in_specs=[tok(d), tok(A_W), tok(A_W), tok(A_W), tok(B_W), tok(C_W), full(gng),
                  pl.BlockSpec((1, 3, d), bm), _layer_spec(w, layer), full(fg)],
        out_specs=tok(d),
        out_shape=jax.ShapeDtypeStruct((b, t, d), F32),
        compiler_params=_params(("parallel", "parallel")),
        name="outproj",
    )(x, of, ob, sza, bx, cx, gng, mod, w, fg)


def _deinterleave(w32):
    even, odd = w32[..., 0::2], w32[..., 1::2]
    return jnp.concatenate([even, odd], axis=-1), jnp.concatenate([odd, even], axis=-1)


def _layout_w_in(w):
    w = w.astype(BF16)
    z = lambda n: jnp.zeros(w.shape[:-1] + (n,), w.dtype)
    o_b, o_a, o_qa, o_kv, o_zb, o_h, o_bg, o_cg, o_zc = 2048, 2056, 2064, 2256, 2416, 2672, 2928, 3184, 3440
    kp_main, kp_swap = _deinterleave(w[..., o_kv + B_KV_LORA:o_zb])
    cols = [w[..., 0:2048], w[..., o_zb:o_zc + C_W],
            w[..., o_qa:o_kv], z(S_CKV - S_QA - B_Q_LORA), w[..., o_kv:o_kv + B_KV_LORA],
            w[..., o_b:o_qa], z(B_NOPE - 4 * A_HEADS), kp_main, z(LANE - B_NOPE - B_ROPE),
            z(B_NOPE), kp_swap, z(LANE - B_NOPE - B_ROPE)]
    out = jnp.concatenate(cols, axis=-1)
    assert out.shape[-1] == NP_IN
    return out


def _layout_w_qb(w):
    w = w.astype(BF16)
    z = lambda n: jnp.zeros(w.shape[:-1] + (n,), w.dtype)
    hd = B_NOPE + B_ROPE
    main, swap = [], []
    for h in range(B_HEADS):
        rm, rs = _deinterleave(w[..., h * hd + B_NOPE:(h + 1) * hd])
        main += [w[..., h * hd:h * hd + B_NOPE], rm, z(LANE - hd)]
        swap += [z(B_NOPE), rs, z(LANE - hd)]
    return jnp.concatenate(main, axis=-1), jnp.concatenate(swap, axis=-1)


def _layout_w_kvb(w):
    w = w.astype(BF16)
    hd = B_NOPE + B_V
    wk, wv = [], []
    for h in range(B_HEADS):
        wk += [w[..., h * hd:h * hd + B_NOPE], jnp.zeros(w.shape[:-1] + (LANE - B_NOPE,), w.dtype)]
        wv += [w[..., h * hd + B_NOPE:(h + 1) * hd], jnp.zeros(w.shape[:-1] + (B_VA - B_V,), w.dtype)]
    return jnp.concatenate(wk, axis=-1), jnp.swapaxes(jnp.concatenate(wv, axis=-1), -1, -2)


def _rope_tables(t, rotate):
    half = B_ROPE // 2
    if rotate:
        pos = np.arange(t)
        n_freq = B_ROPE // 4
        inv_freq = ROPE_THETA ** (-np.arange(n_freq, dtype=np.float64) / n_freq)
        ang = np.concatenate([(pos // GRID_W)[:, None] * inv_freq, (pos % GRID_W)[:, None] * inv_freq], axis=-1)
        cos, sin = np.cos(ang).astype(np.float32), np.sin(ang).astype(np.float32)
    else:
        cos, sin = np.ones((t, half), np.float32), np.zeros((t, half), np.float32)
    z = lambda n: np.zeros((t, n), np.float32)
    tail = LANE - B_NOPE - B_ROPE
    scale = np.float32((B_NOPE + B_ROPE) ** -0.5 * math.log2(math.e))
    cq = np.concatenate([np.ones((t, B_NOPE), np.float32), cos, cos, z(tail)], axis=1) * scale
    sq = np.concatenate([z(B_NOPE), -sin, sin, z(tail)], axis=1) * scale
    ck = np.concatenate([z(B_NOPE), cos, cos, z(tail)], axis=1)
    sk = np.concatenate([z(B_NOPE), -sin, sin, z(tail)], axis=1)
    return jnp.asarray(np.concatenate([cq, sq, ck, sk], axis=1), dtype=F32)


def kernel(x, c, ctx, c_ctx, w_ada, b_ada, norm_g, w_in, gdn_conv, gdn_a_log, gdn_dt_bias, gdn_norm_g,
           mla_q_norm_g, mla_w_qb, mla_kv_norm_g, mla_w_kvb, conv_w, w_out, final_norm_g):
    bsz, t, d = x.shape
    t_ctx = ctx.shape[1]
    depth = w_ada.shape[0]
    assert t % GDN_CHUNK == 0 and t_ctx % GDN_CHUNK == 0 and t % GRID_W == 0
    tm = 512
    bb = 2 if bsz % 2 == 0 else 1

    rows = -(-(bsz + 1) // 8) * 8
    cc = jnp.concatenate([c, c_ctx[None, :], jnp.zeros((rows - bsz - 1, d), F32)], axis=0)
    mod = _ada(cc, w_ada, b_ada)

    tab_x = _rope_tables(t, True)
    tab_c = _rope_tables(t_ctx, False)
    fg = final_norm_g.reshape(1, d)
    w_all = _layout_w_in(w_in)
    wqm_all, wqs_all = _layout_w_qb(mla_w_qb)
    wk_all, wv_all = _layout_w_kvb(mla_w_kvb)
    wo_all = w_out.astype(BF16)

    for l in range(depth):
        mod_x = mod[l, :bsz].reshape(bsz, 3, d)
        mod_c = mod[l, bsz:bsz + 1].reshape(1, 3, d)
        ng = norm_g[l].reshape(1, d)
        qng = mla_q_norm_g[l].reshape(1, B_Q_LORA)
        kvng = mla_kv_norm_g[l].reshape(1, B_KV_LORA)
        gpad = lambda a: jnp.concatenate([jnp.zeros((2 * A_HEADS,), F32), a.reshape(-1),
                                          jnp.zeros((LANE - 4 * A_HEADS,), F32)])
        gsc = jnp.stack([gpad(gdn_a_log[l]), gpad(gdn_dt_bias[l])] + [jnp.zeros((LANE,), F32)] * 6)
        wconv = jnp.concatenate(
            [gdn_conv[l], jnp.pad(conv_w[l], (((A_CONV - C_CONV) // 2,) * 2, (0, 0)))], axis=1)
        wconv = jnp.concatenate([wconv, jnp.zeros((8 - A_CONV, CIN_W), F32)], axis=0)
        gng = gdn_norm_g[l].reshape(1, A_DV)
        last = l == depth - 1

        shared = (l, ng, w_all, qng, wqm_all, wqs_all, kvng, wk_all, wv_all)
        (qn_c, kn_c, vv_c, cx_c, sza_c, gates_c, qm_c, km_c, vm_c, szb_c) = _inproj(
            ctx, mod_c, False, *shared, tab_c, gsc, wconv, tm)
        (qn_x, kn_x, vv_x, cx_x, sza_x, gates_x, qm_x, km_x, vm_x, szb_x) = _inproj(
            x, mod_x, True, *shared, tab_x, gsc, wconv, tm)

        s_zero = jnp.zeros((bsz, 2, A_HEADS, A_DK, A_DV), F32)
        of_c, ob_c, s_ctx = _scan(qn_c, kn_c, vv_c, gates_c, s_zero, bb)
        of_x, ob_x, _ = _scan(qn_x, kn_x, vv_x, gates_x, s_ctx, bb)

        bx_x = _attn(qm_x, [(km_x, vm_x), (km_c, vm_c)], szb_x, tm)
        x_new = _outproj(x, of_x, ob_x, sza_x, bx_x, cx_x, gng, mod_x, True, l, wo_all, fg, last, tm)
        if not last:
            bx_c = _attn(qm_c, [(km_c, vm_c)], szb_c, tm)
            ctx = _outproj(ctx, of_c, ob_c, sza_c, bx_c, cx_c, gng, mod_c, False, l, wo_all, fg, False, tm)
        x = x_new
    return x
```

```python
import functools
import math

import jax
import jax.numpy as jnp
import numpy as np
from jax.experimental import pallas as pl
from jax.experimental.pallas import tpu as pltpu

F32 = jnp.float32
BF16 = jnp.bfloat16

EPS = 1e-6
GRID_W = 64
ROPE_THETA = 10000.0

A_HEADS = 4
A_DK = 128
A_DV = 128
A_QK = A_HEADS * A_DK
A_W = A_HEADS * A_DV
A_CONV = 5
B_HEADS = 4
B_NOPE = 64
B_ROPE = 32
B_V = 64
B_Q_LORA = 192
B_KV_LORA = 128
B_W = B_HEADS * B_V
B_VA = B_V + 16
B_WA = B_HEADS * B_VA
C_W = 256
C_CONV = 3

LANE = 128
HALO = 16
GDN_CHUNK = 128
ATTN_KEY_BLOCK = 256
VMEM_LIMIT = 48 * 1024 * 1024

W_HEAD = 2048
S_ZA = 1536
T_ZB = 0
T_H = 256
T_BG = 512
T_CG = 768
T_ZC = 1024
T_QA = 1280
T_CKV = 1536
W_TAIL = 1792
CIN_W = 3 * A_QK + C_W


def _dot(a, b):
    return jnp.dot(a, b, preferred_element_type=F32)


def _dot_nt(a, b):
    return jax.lax.dot_general(a, b, (((1,), (1,)), ((), ())), preferred_element_type=F32)


def _silu(x):
    return x * jax.nn.sigmoid(x)


def _params(sem):
    return pltpu.CompilerParams(dimension_semantics=sem, vmem_limit_bytes=VMEM_LIMIT)


def _ada_kernel(c_ref, w_ref, b_ref, o_ref):
    s = _silu(c_ref[...]).astype(BF16)
    o_ref[0] = _dot(s, w_ref[0].astype(BF16)) + b_ref[0]


def _ada(cc, w_ada, b_ada):
    depth, d, n3 = w_ada.shape
    r = cc.shape[0]
    tn = 768
    return pl.pallas_call(
        _ada_kernel,
        grid=(depth, n3 // tn),
        in_specs=[pl.BlockSpec((r, d), lambda l, j: (0, 0)),
                  pl.BlockSpec((1, d, tn), lambda l, j: (l, 0, j)),
                  pl.BlockSpec((1, 1, tn), lambda l, j: (l, 0, j))],
        out_specs=pl.BlockSpec((1, r, tn), lambda l, j: (l, 0, j)),
        out_shape=jax.ShapeDtypeStruct((depth, r, n3), F32),
        compiler_params=_params(("parallel", "parallel")),
        name="ada",
    )(cc, w_ada, b_ada.reshape(depth, 1, n3))


def _inproj_kernel(x_ref, xp_ref, xn_ref, mod_ref, ng_ref, w_ref, wt_ref, qng_ref, wqm_ref, wqs_ref, kvng_ref,
                   wk_ref, wv_ref, tab_ref, gsc_ref, wc_ref,
                   qn_ref, kn_ref, vv_ref, cx_ref, sza_ref, gates_ref, qm_ref, km_ref, vm_ref, szb_ref, *, tm):
    j = pl.program_id(1)
    nj = pl.num_programs(1)

    def normed(xv):
        y = xv * jax.lax.rsqrt(jnp.mean(xv * xv, axis=-1, keepdims=True) + EPS) * ng_ref[...]
        return y * (1.0 + mod_ref[0, 1:2, :]) + mod_ref[0, 0:1, :]

    hn = normed(x_ref[0]).astype(BF16)
    h_prev = jnp.where(j > 0, normed(xp_ref[0]), 0.0).astype(BF16)
    h_next = jnp.where(j < nj - 1, normed(xn_ref[0]), 0.0).astype(BF16)
    h_ext = jnp.concatenate([h_prev, hn, h_next], axis=0)

    def tail(a, b):
        return _dot(hn, wt_ref[:, a:b])

    def conv(u_ext, c0, c1):
        pad = A_CONV // 2
        rows = u_ext.shape[0]
        acc = None
        for k in range(A_CONV):
            shifted = u_ext if k == pad else pltpu.roll(u_ext, (pad - k) % rows, 0)
            term = shifted[HALO:HALO + tm, :] * wc_ref[k:k + 1, c0:c1]
            acc = term if acc is None else acc + term
        return acc

    blk = 2 * LANE
    for i in range(3 * A_QK // blk):
        y2 = _silu(conv(_dot(h_ext, w_ref[:, i * blk:(i + 1) * blk]), i * blk, (i + 1) * blk))
        for hh in range(2):
            h = 2 * i + hh
            y = y2[:, hh * LANE:(hh + 1) * LANE]
            if h < 2 * A_HEADS:
                y = y * jax.lax.rsqrt(jnp.sum(y * y, axis=-1, keepdims=True) + EPS)
            if h < A_HEADS:
                qn_ref[0, :, h * LANE:(h + 1) * LANE] = (y * (A_DK ** -0.5)).astype(BF16)
            elif h < 2 * A_HEADS:
                kn_ref[0, :, (h - A_HEADS) * LANE:(h - A_HEADS + 1) * LANE] = y.astype(BF16)
            else:
                vv_ref[0, :, (h - 2 * A_HEADS) * LANE:(h - 2 * A_HEADS + 1) * LANE] = y.astype(BF16)

    hc_ext = _dot(h_ext, wt_ref[:, T_CG:T_ZC]) * _dot(h_ext, wt_ref[:, T_H:T_BG])
    cx_ref[0] = (tail(T_BG, T_CG) * _silu(tail(T_ZC, T_QA)) * conv(hc_ext, 3 * A_QK, CIN_W)).astype(BF16)

    sza_ref[0] = _silu(_dot(hn, w_ref[:, S_ZA:W_HEAD])).astype(BF16)
    szb_ref[0] = _silu(tail(T_ZB, T_H)).astype(BF16)

    tab = tab_ref[...]
    cq, sq = tab[:, 0:LANE], tab[:, LANE:2 * LANE]
    ck, sk = tab[:, 2 * LANE:3 * LANE], tab[:, 3 * LANE:4 * LANE]

    qa_blk = tail(T_QA, T_CKV)
    qa = qa_blk[:, :B_Q_LORA]
    nq =(qa * jax.lax.rsqrt(jnp.mean(qa * qa, axis=-1, keepdims=True) + EPS) * qng_ref[...]).astype(BF16)
    qmain = _dot(nq, wqm_ref[...])
    qswap = _dot(nq, wqs_ref[...])
    for h in range(B_HEADS):
        sl = slice(h * LANE, (h + 1) * LANE)
        qm_ref[0, :, sl] = (qmain[:, sl] * cq + qswap[:, sl] * sq).astype(BF16)

    ckv_blk = tail(T_CKV, W_TAIL)
    ckv = ckv_blk[:, :B_KV_LORA]
    nkv =(ckv * jax.lax.rsqrt(jnp.mean(ckv * ckv, axis=-1, keepdims=True) + EPS) * kvng_ref[...]).astype(BF16)
    kk = _dot(nkv, wk_ref[...])
    vt = _dot_nt(wv_ref[...], nkv)
    vrow = jax.lax.broadcasted_iota(jnp.int32, vt.shape, 0)
    ones_row = functools.reduce(jnp.logical_or, [vrow == h * B_VA + B_V for h in range(B_HEADS)])
    vm_ref[0] = jnp.where(ones_row, 1.0, vt).astype(BF16)
    kpm = ckv_blk[:, B_KV_LORA:]
    kpe = kpm * ck + qa_blk[:, LANE:] * sk
    for h in range(B_HEADS):
        sl = slice(h * LANE, (h + 1) * LANE)
        km_ref[0, :, sl] = (kk[:, sl] + kpe).astype(BF16)

    lane = jax.lax.broadcasted_iota(jnp.int32, kpm.shape, 1)
    z = kpm + gsc_ref[1:2, :]
    sp = jnp.maximum(z, 0.0) + jnp.log1p(jnp.exp(-jnp.abs(z)))
    g = -jnp.exp(gsc_ref[0:1, :]) * sp
    gates_ref[0] = jnp.where(lane < 2 * A_HEADS, jax.nn.sigmoid(kpm), jnp.where(lane < 4 * A_HEADS, g, 0.0))


def _layer_spec(a, layer):
    return pl.BlockSpec((None,) + a.shape[1:], lambda i, j: (layer,) + (0,) * (a.ndim - 1))


def _inproj(x, mod, per_batch, layer, ng, w, wt, qng, wqm, wqs, kvng, wk, wv, tab, gsc, wconv, tm):
    b, t, d = x.shape
    tm = min(tm, t)
    r = tm // HALO
    nh = t // HALO
    bm = (lambda i, j: (i, 0, 0)) if per_batch else (lambda i, j: (0, 0, 0))
    tok = lambda width: pl.BlockSpec((1, tm, width), lambda i, j: (i, j, 0))
    full = lambda a: pl.BlockSpec(a.shape, lambda i, j: (0,) * a.ndim)
    lay = lambda a: _layer_spec(a, layer)
    widths = (A_QK, A_QK, A_W, C_W, A_W, LANE, B_HEADS * LANE, B_HEADS * LANE, B_W, B_W)
    dtypes = (BF16, BF16, BF16, BF16, BF16, F32, BF16, BF16, BF16, BF16)
    V_OUT = 8
    return pl.pallas_call(
        functools.partial(_inproj_kernel, tm=tm),
        grid=(b, t // tm),
        in_specs=[tok(d),
                  pl.BlockSpec((1, HALO, d), lambda i, j: (i, jnp.maximum(j * r - 1, 0), 0)),
                  pl.BlockSpec((1, HALO, d), lambda i, j: (i, jnp.minimum((j + 1) * r, nh - 1), 0)),
                  pl.BlockSpec((1, 3, d), bm), full(ng),
                  pl.BlockSpec((None, d, W_HEAD), lambda i, j: (layer, 0, 0)), lay(wt), full(qng), lay(wqm), lay(wqs),
                  full(kvng), lay(wk), lay(wv),
                  pl.BlockSpec((tm, 4 * LANE), lambda i, j: (j, 0)), full(gsc), full(wconv)],
        out_specs=[pl.BlockSpec((1, B_WA, tm), lambda i, j: (i, 0, j)) if n == V_OUT else tok(wd)
                   for n, wd in enumerate(widths)],
        out_shape=[jax.ShapeDtypeStruct((b, B_WA, t) if n == V_OUT else (b, t, wd), dt)
                   for n, (wd, dt) in enumerate(zip(widths, dtypes))],
        compiler_params=_params(("parallel", "parallel")),
        name="inproj",
    )(x, x, x, mod, ng, w, wt, qng, wqm, wqs, kvng, wk, wv, tab, gsc, wconv)


INV_BASE_LOG2 = 3


def _unit_triangular_inverses(mats, c):
    row = jax.lax.broadcasted_iota(jnp.int32, (c, c), 0)
    col = jax.lax.broadcasted_iota(jnp.int32, (c, c), 1)
    same = lambda k: (row >> k) == (col >> k)
    eye = jnp.where(row == col, 1.0, 0.0)
    base = same(INV_BASE_LOG2)
    a0 = [jnp.where(base, a, 0.0) for a in mats]
    t = [eye - ai for ai in a0]
    apow = [ai.astype(BF16) for ai in a0]
    for _ in range(INV_BASE_LOG2 - 1):
        apow = [_dot(ap, ap).astype(BF16) for ap in apow]
        t = [ti + _dot(ti.astype(BF16), ap) for ti, ap in zip(t, apow)]
    for k in range(INV_BASE_LOG2, int(math.log2(c))):
        coupling = same(k + 1) & jnp.logical_not(same(k))
        e = [jnp.where(coupling, a, 0.0).astype(BF16) for a in mats]
        tb = [ti.astype(BF16) for ti in t]
        te = [_dot(tbi, ei).astype(BF16) for tbi, ei in zip(tb, e)]
        t = [ti - _dot(tei, tbi) for ti, tei, tbi in zip(t, te, tb)]
    return t


def _scan_kernel(qf_ref, kf_ref, vf_ref, gf_ref, qb_ref, kb_ref, vb_ref, gb_ref, s0_ref,
                 of_ref, ob_ref, s_ref, *, c, bb):
    n = pl.program_id(1)

    @pl.when(n == 0)
    def _():
        s_ref[...] = s0_ref[...]

    row = jax.lax.broadcasted_iota(jnp.int32, (c, c), 0)
    col = jax.lax.broadcasted_iota(jnp.int32, (c, c), 1)
    dirs = ((qf_ref, kf_ref, vf_ref, gf_ref), (qb_ref, kb_ref, vb_ref, gb_ref))
    o_refs = (of_ref, ob_ref)
    incl = (col <= row, col >= row)
    strict = (col < row, col > row)
    gates, gcb, gcb_t, gl_row = {}, {}, {}, {}
    for e in range(bb):
        for d in range(2):
            g = dirs[d][3][e]
            tri = jnp.where(incl[d], 1.0, 0.0).astype(BF16)
            hi = g.astype(BF16)
            r1 = g - hi.astype(F32)
            mid = r1.astype(BF16)
            lo = (r1 - mid.astype(F32)).astype(BF16)
            gc_all = _dot(tri, hi) + _dot(tri, mid) + _dot(tri, lo)
            gates[e, d] = g
            gcb[e, d] = gc_all
            gcb_t[e, d] = gc_all.T
            gl_row[e, d] = gc_all[c - 1:c, :] if d == 0 else gc_all[0:1, :]

    chains = [(e, d, h) for e in range(bb) for d in range(2) for h in range(A_HEADS)]
    sls = [slice(h * LANE, (h + 1) * LANE) for _, _, h in chains]
    jbs = [A_HEADS * d + h for _, d, h in chains]
    jgs = [2 * A_HEADS + jb for jb in jbs]
    beta = [gates[e, d][:, jb:jb + 1] for (e, d, _), jb in zip(chains, jbs)]
    gc = [gcb[e, d][:, jg:jg + 1] for (e, d, _), jg in zip(chains, jgs)]
    gcr = [gcb_t[e, d][jg:jg + 1, :] for (e, d, _), jg in zip(chains, jgs)]
    gl = [gl_row[e, d][:, jg:jg + 1] for (e, d, _), jg in zip(chains, jgs)]
    q = [dirs[d][0][e, :, sl] for (e, d, _), sl in zip(chains, sls)]
    k = [dirs[d][1][e, :, sl] for (e, d, _), sl in zip(chains, sls)]
    v = [dirs[d][2][e, :, sl] for (e, d, _), sl in zip(chains, sls)]
    decay = [jnp.where(incl[d], jnp.exp(jnp.where(incl[d], gci - gcri, 0.0)), 0.0)
             for (_, d, _), gci, gcri in zip(chains, gc, gcr)]
    kbeta = [ki.astype(F32) * bi for ki, bi in zip(k, beta)]
    egc = [jnp.exp(gci) for gci in gc]
    a = [jnp.where(strict[d], _dot_nt(kbi.astype(BF16), ki) * di, 0.0)
         for (_, d, _), kbi, ki, di in zip(chains, kbeta, k, decay)]
    qk = [(_dot_nt(qi, ki) * di).astype(BF16) for qi, ki, di in zip(q, k, decay)]
    rhs = [jnp.concatenate([vi.astype(F32) * bi, kbi * ei], axis=1).astype(BF16)
           for vi, bi, kbi, ei in zip(v, beta, kbeta, egc)]
    kd_t = [(ki.astype(F32) * jnp.exp(gli - gci)).T.astype(BF16) for ki, gli, gci in zip(k, gl, gc)]
    tinv = [ti.astype(BF16) for ti in _unit_triangular_inverses(a, c)]
    uw = [_dot(ti, ri) for ti, ri in zip(tinv, rhs)]
    qdk = [jnp.concatenate([(qi.astype(F32) * ei).astype(BF16), qki], axis=1) for qi, ei, qki in zip(q, egc, qk)]
    s = [s_ref[e, d, h] for e, d, h in chains]
    sb = [si.astype(BF16) for si in s]
    v_new = [(uwi[:, :A_DV] - _dot(uwi[:, A_DV:].astype(BF16), sbi)).astype(BF16) for uwi, sbi in zip(uw, sb)]
    o = [_dot(qi, jnp.concatenate([sbi, vi], axis=0)) for qi, sbi, vi in zip(qdk, sb, v_new)]
    for (e, d, h), sl, oi in zip(chains, sls, o):
        o_refs[d][e, :, sl] = oi.astype(BF16)
    for (e, d, h), gli, si, kdi, vi in zip(chains, gl, s, kd_t, v_new):
        s_ref[e, d, h] = jnp.exp(gli) * si + _dot(kdi, vi)


def _scan(qn, kn, vv, gates, s0, bb):
    b, t, _ = qn.shape
    c = GDN_CHUNK
    n = t // c
    assert b % bb == 0
    fwd = lambda width: pl.BlockSpec((bb, c, width), lambda i, j: (i, j, 0))
    bwd = lambda width: pl.BlockSpec((bb, c, width), lambda i, j: (i, n - 1 - j, 0))
    st = pl.BlockSpec((bb, 2, A_HEADS, A_DK, A_DV), lambda i, j: (i, 0, 0, 0, 0))
    return pl.pallas_call(
        functools.partial(_scan_kernel, c=c, bb=bb),
        grid=(b // bb, n),
        in_specs=[fwd(A_QK), fwd(A_QK), fwd(A_W), fwd(LANE), bwd(A_QK), bwd(A_QK), bwd(A_W), bwd(LANE), st],
        out_specs=[fwd(A_W), bwd(A_W), st],
        out_shape=[jax.ShapeDtypeStruct((b, t, A_W), BF16), jax.ShapeDtypeStruct((b, t, A_W), BF16),
                   jax.ShapeDtypeStruct(s0.shape, F32)],
        compiler_params=_params(("parallel", "arbitrary")),
        name="gdn_scan",
    )(qn, kn, vv, gates, qn, kn, vv, gates, s0)


def _attention_tile(q_ref, kv_refs, szb_ref):
    n_src = len(kv_refs) // 2
    heads = range(B_HEADS)
    qh = [q_ref[0, :, h * LANE:(h + 1) * LANE] for h in heads]
    blocks = [(kv_refs[2 * i], kv_refs[2 * i + 1], slice(j * ATTN_KEY_BLOCK, (j + 1) * ATTN_KEY_BLOCK))
              for i in range(n_src) for j in range(kv_refs[2 * i].shape[1] // ATTN_KEY_BLOCK)]

    def scores(blk):
        k_ref, _, ks = blk
        return [_dot_nt(k_ref[0, ks, h * LANE:(h + 1) * LANE], qh[h]) for h in heads]

    m = acc = None
    s_next = scores(blocks[0])
    for n, (_, vt_ref, ks) in enumerate(blocks):
        s, s_next = s_next, (scores(blocks[n + 1]) if n + 1 < len(blocks) else None)
        mb = [si.max(axis=0, keepdims=True) for si in s]
        m_new = mb if m is None else [jnp.maximum(mi, mbi) for mi, mbi in zip(m, mb)]
        p = [jnp.exp2(si - mn).astype(BF16) for si, mn in zip(s, m_new)]
        pv = [_dot(vt_ref[0, h * B_VA:(h + 1) * B_VA, ks], p[h]) for h in heads]
        if m is None:
            acc = pv
        else:
            acc = [jnp.exp2(mi - mn) * ai + pvi for mi, mn, ai, pvi in zip(m, m_new, acc, pv)]
        m = m_new
    o = jnp.concatenate([a[:B_V] / a[B_V:B_V + 1] for a in acc], axis=0).T
    return (o * szb_ref[0].astype(F32)).astype(BF16)


def _attn_out_kernel(*refs, n_src, final):
    q_ref = refs[0]
    kv_refs = refs[1:1 + 2 * n_src]
    (szb_ref, x_ref, of_ref, ob_ref, sza_ref, cx_ref, gng_ref, mod_ref, w_ref, fg_ref, o_ref) = refs[1 + 2 * n_src:]
    bx = _attention_tile(q_ref, kv_refs, szb_ref)
    o = of_ref[0].astype(F32) + ob_ref[0].astype(F32)
    sza = sza_ref[0].astype(F32)
    acc = None
    for h in range(A_HEADS):
        sl = slice(h * LANE, (h + 1) * LANE)
        oh = o[:, sl]
        ah = oh * jax.lax.rsqrt(jnp.mean(oh * oh, axis=-1, keepdims=True) + EPS) * gng_ref[...]
        term = _dot((ah * sza[:, sl]).astype(BF16), w_ref[sl, :])
        acc = term if acc is None else acc + term
    acc = acc + _dot(bx, w_ref[A_W:A_W + B_W, :]) + _dot(cx_ref[0], w_ref[A_W + B_W:, :])
    xn = x_ref[0] + mod_ref[0, 2:3, :] * acc
    if final:
        xn = xn * jax.lax.rsqrt(jnp.mean(xn * xn, axis=-1, keepdims=True) + EPS) * fg_ref[...]
    o_ref[0] = xn


def _attn_out(x, q, kvs, szb, of, ob, sza, cx, gng, mod, per_batch, layer, w, fg, final, tm):
    b, t, d = x.shape
    tm = min(tm, t)
    bm = (lambda i, j: (i, 0, 0)) if per_batch else (lambda i, j: (0, 0, 0))
    tok = lambda width: pl.BlockSpec((1, tm, width), lambda i, j: (i, j, 0))
    full = lambda a: pl.BlockSpec(a.shape, lambda i, j: (0,) * a.ndim)
    in_specs = [tok(B_HEADS * LANE)]
    args = [q]
    for k, v in kvs:
        in_specs.append(pl.BlockSpec((1,) + k.shape[1:], lambda i, j: (i, 0, 0)))
        in_specs.append(pl.BlockSpec((1,) + v.shape[1:], lambda i, j: (i, 0, 0)))
        args += [k, v]
    in_specs += [tok(B_W), tok(d), tok(A_W), tok(A_W), tok(A_W), tok(C_W), full(gng),
                 pl.BlockSpec((1, 3, d), bm), _layer_spec(w, layer), full(fg)]
    args += [szb, x, of, ob, sza, cx, gng, mod, w, fg]
    return pl.pallas_call(
        functools.partial(_attn_out_kernel, n_src=len(kvs), final=final),
        grid=(b, t // tm),
        in_specs=in_specs,
        out_specs=tok(d),
        out_shape=jax.ShapeDtypeStruct((b, t, d), F32),
        compiler_params=_params(("parallel", "parallel")),
        name="attn_out",
    )(*args)


def _deinterleave(w32):
    even, odd = w32[..., 0::2], w32[..., 1::2]
    return jnp.concatenate([even, odd], axis=-1), jnp.concatenate([odd, even], axis=-1)


def _layout_w_in(w):
    w = w.astype(BF16)
    z = lambda n: jnp.zeros(w.shape[:-1] + (n,), w.dtype)
    o_b, o_qa, o_kv, o_zb, o_end = 2048, 2064, 2256, 2416, 3696
    kp_main, kp_swap = _deinterleave(w[..., o_kv + B_KV_LORA:o_zb])
    cols = [w[..., o_zb:o_end],
            w[..., o_qa:o_kv], kp_swap, z(2 * LANE - B_Q_LORA - B_ROPE),
            w[..., o_kv:o_kv + B_KV_LORA], w[..., o_b:o_qa], z(B_NOPE - 4 * A_HEADS), kp_main,
            z(LANE - B_NOPE - B_ROPE)]
    tail = jnp.concatenate(cols, axis=-1)
    assert tail.shape[-1] == W_TAIL
    return w, tail


def _layout_w_qb(w):
    w = w.astype(BF16)
    z = lambda n: jnp.zeros(w.shape[:-1] + (n,), w.dtype)
    hd = B_NOPE + B_ROPE
    main, swap = [], []
    for h in range(B_HEADS):
        rm, rs = _deinterleave(w[..., h * hd + B_NOPE:(h + 1) * hd])
        main += [w[..., h * hd:h * hd + B_NOPE], rm, z(LANE - hd)]
        swap += [z(B_NOPE), rs, z(LANE - hd)]
    return jnp.concatenate(main, axis=-1), jnp.concatenate(swap, axis=-1)


def _layout_w_kvb(w):
    w = w.astype(BF16)
    hd = B_NOPE + B_V
    wk, wv = [], []
    for h in range(B_HEADS):
        wk += [w[..., h * hd:h * hd + B_NOPE], jnp.zeros(w.shape[:-1] + (LANE - B_NOPE,), w.dtype)]
        wv += [w[..., h * hd + B_NOPE:(h + 1) * hd], jnp.zeros(w.shape[:-1] + (B_VA - B_V,), w.dtype)]
    return jnp.concatenate(wk, axis=-1), jnp.swapaxes(jnp.concatenate(wv, axis=-1), -1, -2)


def _rope_tables(t, rotate):
    half = B_ROPE // 2
    if rotate:
        pos = np.arange(t)
        n_freq = B_ROPE // 4
        inv_freq = ROPE_THETA ** (-np.arange(n_freq, dtype=np.float64) / n_freq)
        ang = np.concatenate([(pos // GRID_W)[:, None] * inv_freq, (pos % GRID_W)[:, None] * inv_freq], axis=-1)
        cos, sin = np.cos(ang).astype(np.float32), np.sin(ang).astype(np.float32)
    else:
        cos, sin = np.ones((t, half), np.float32), np.zeros((t, half), np.float32)
    z = lambda n: np.zeros((t, n), np.float32)
    tail = LANE - B_NOPE - B_ROPE
    scale = np.float32((B_NOPE + B_ROPE) ** -0.5 * math.log2(math.e))
    cq = np.concatenate([np.ones((t, B_NOPE), np.float32), cos, cos, z(tail)], axis=1) * scale
    sq = np.concatenate([z(B_NOPE), -sin, sin, z(tail)], axis=1) * scale
    ck = np.concatenate([z(B_NOPE), cos, cos, z(tail)], axis=1)
    sk = np.concatenate([z(B_NOPE), -sin, sin, z(tail)], axis=1)
    return jnp.asarray(np.concatenate([cq, sq, ck, sk], axis=1), dtype=F32)


def kernel(x, c, ctx, c_ctx, w_ada, b_ada, norm_g, w_in, gdn_conv, gdn_a_log, gdn_dt_bias, gdn_norm_g,
           mla_q_norm_g, mla_w_qb, mla_kv_norm_g, mla_w_kvb, conv_w, w_out, final_norm_g):
    bsz, t, d = x.shape
    t_ctx = ctx.shape[1]
    depth = w_ada.shape[0]
    assert t % GDN_CHUNK == 0 and t_ctx % GDN_CHUNK == 0 and t % GRID_W == 0
    tm = 512
    bb = 2 if bsz % 2 == 0 else 1

    rows = -(-(bsz + 1) // 8) * 8
    cc = jnp.concatenate([c, c_ctx[None, :], jnp.zeros((rows - bsz - 1, d), F32)], axis=0)
    mod = _ada(cc, w_ada, b_ada)

    tab_x = _rope_tables(t, True)
    tab_c = _rope_tables(t_ctx, False)
    fg = final_norm_g.reshape(1, d)
    w_all, wt_all = _layout_w_in(w_in)
    wqm_all, wqs_all = _layout_w_qb(mla_w_qb)
    wk_all, wv_all = _layout_w_kvb(mla_w_kvb)
    wo_all = w_out.astype(BF16)

    for l in range(depth):
        mod_x = mod[l, :bsz].reshape(bsz, 3, d)
        mod_c = mod[l, bsz:bsz + 1].reshape(1, 3, d)
        ng = norm_g[l].reshape(1, d)
        qng = mla_q_norm_g[l].reshape(1, B_Q_LORA)
        kvng = mla_kv_norm_g[l].reshape(1, B_KV_LORA)
        gpad = lambda a: jnp.concatenate([jnp.zeros((2 * A_HEADS,), F32), a.reshape(-1),
                                          jnp.zeros((LANE - 4 * A_HEADS,), F32)])
        gsc = jnp.stack([gpad(gdn_a_log[l]), gpad(gdn_dt_bias[l])] + [jnp.zeros((LANE,), F32)] * 6)
        wconv = jnp.concatenate(
            [gdn_conv[l], jnp.pad(conv_w[l], (((A_CONV - C_CONV) // 2,) * 2, (0, 0)))], axis=1)
        wconv = jnp.concatenate([wconv, jnp.zeros((8 - A_CONV, CIN_W), F32)], axis=0)
        gng = gdn_norm_g[l].reshape(1, A_DV)
        last = l == depth - 1

        shared = (l, ng, w_all, wt_all, qng, wqm_all, wqs_all, kvng, wk_all, wv_all)
        (qn_c, kn_c, vv_c, cx_c, sza_c, gates_c, qm_c, km_c, vm_c, szb_c) = _inproj(
            ctx, mod_c, False, *shared, tab_c, gsc, wconv, tm)
        (qn_x, kn_x, vv_x, cx_x, sza_x, gates_x, qm_x, km_x, vm_x, szb_x) = _inproj(
            x, mod_x, True, *shared, tab_x, gsc, wconv, tm)

        s_zero = jnp.zeros((bsz, 2, A_HEADS, A_DK, A_DV), F32)
        of_c, ob_c, s_ctx = _scan(qn_c, kn_c, vv_c, gates_c, s_zero, bb)
        of_x, ob_x, _ = _scan(qn_x, kn_x, vv_x, gates_x, s_ctx, bb)

        x_new = _attn_out(x, qm_x, [(km_x, vm_x), (km_c, vm_c)], szb_x, of_x, ob_x, sza_x, cx_x, gng, mod_x, True,
                          l, wo_all, fg, last, tm)
        if not last:
            ctx = _attn_out(ctx, qm_c, [(km_c, vm_c)], szb_c, of_c, ob_c, sza_c, cx_c, gng, mod_c, False,
                            l, wo_all, fg, False, tm)
        x = x_new
    return x
```

```python
import functools
import math

import jax
import jax.numpy as jnp
import numpy as np
from jax.experimental import pallas as pl
from jax.experimental.pallas import tpu as pltpu

F32 = jnp.float32
BF16 = jnp.bfloat16

EPS = 1e-6
GRID_W = 64
ROPE_THETA = 10000.0

A_HEADS = 4
A_DK = 128
A_DV = 128
A_QK = A_HEADS * A_DK
A_W = A_HEADS * A_DV
A_CONV = 5
B_HEADS = 4
B_NOPE = 64
B_ROPE = 32
B_V = 64
B_Q_LORA = 192
B_KV_LORA = 128
B_W = B_HEADS * B_V
B_VA = B_V + 16
B_WA = B_HEADS * B_VA
C_W = 256
C_CONV = 3

LANE = 128
HALO = 16
GDN_CHUNK = 128
ATTN_KEY_BLOCK = 256
VMEM_LIMIT = 48 * 1024 * 1024

W_HEAD = 2048
S_ZA = 1536
T_ZB = 0
T_H = 256
T_BG = 512
T_CG = 768
T_ZC = 1024
T_QA = 1280
T_CKV = 1536
W_TAIL = 1792
CIN_W = 3 * A_QK + C_W


def _dot(a, b):
    return jnp.dot(a, b, preferred_element_type=F32)


def _dot_nt(a, b):
    return jax.lax.dot_general(a, b, (((1,), (1,)), ((), ())), preferred_element_type=F32)


def _silu(x):
    return x * jax.nn.sigmoid(x)


def _params(sem):
    return pltpu.CompilerParams(dimension_semantics=sem, vmem_limit_bytes=VMEM_LIMIT)


def _ada_kernel(c_ref, w_ref, b_ref, o_ref):
    s = _silu(c_ref[...]).astype(BF16)
    o_ref[0] = _dot(s, w_ref[0].astype(BF16)) + b_ref[0]


def _ada(cc, w_ada, b_ada):
    depth, d, n3 = w_ada.shape
    r = cc.shape[0]
    tn = 768
    return pl.pallas_call(
        _ada_kernel,
        grid=(depth, n3 // tn),
        in_specs=[pl.BlockSpec((r, d), lambda l, j: (0, 0)),
                  pl.BlockSpec((1, d, tn), lambda l, j: (l, 0, j)),
                  pl.BlockSpec((1, 1, tn), lambda l, j: (l, 0, j))],
        out_specs=pl.BlockSpec((1, r, tn), lambda l, j: (l, 0, j)),
        out_shape=jax.ShapeDtypeStruct((depth, r, n3), F32),
        compiler_params=_params(("parallel", "parallel")),
        name="ada",
    )(cc, w_ada, b_ada.reshape(depth, 1, n3))


def _inproj_kernel(x_ref, xp_ref, xn_ref, mod_ref, ng_ref, w_ref, wt_ref, qng_ref, wqm_ref, wqs_ref, kvng_ref,
                   wk_ref, wv_ref, tab_ref, gsc_ref, wc_ref,
                   qn_ref, kn_ref, vv_ref, cx_ref, sza_ref, gates_ref, qm_ref, km_ref, vm_ref, szb_ref, *, tm):
    j = pl.program_id(1)
    nj = pl.num_programs(1)

    def normed(xv):
        y = xv * jax.lax.rsqrt(jnp.mean(xv * xv, axis=-1, keepdims=True) + EPS) * ng_ref[...]
        return y * (1.0 + mod_ref[0, 1:2, :]) + mod_ref[0, 0:1, :]

    hn = normed(x_ref[0]).astype(BF16)
    h_prev = jnp.where(j > 0, normed(xp_ref[0]), 0.0).astype(BF16)
    h_next = jnp.where(j < nj - 1, normed(xn_ref[0]), 0.0).astype(BF16)
    h_ext = jnp.concatenate([h_prev, hn, h_next], axis=0)

    def tail(a, b):
        return _dot(hn, wt_ref[:, a:b])

    def conv(u_ext, c0, c1, taps=A_CONV):
        pad = A_CONV // 2
        rows = u_ext.shape[0]
        acc = None
        for k in range(pad - taps // 2, pad + taps // 2 + 1):
            shifted = u_ext if k == pad else pltpu.roll(u_ext, (pad - k) % rows, 0)
            term = shifted[HALO:HALO + tm, :] * wc_ref[k:k + 1, c0:c1]
            acc = term if acc is None else acc + term
        return acc

    blk = 2 * LANE
    for i in range(3 * A_QK // blk):
        y2 = _silu(conv(_dot(h_ext, w_ref[:, i * blk:(i + 1) * blk]), i * blk, (i + 1) * blk))
        for hh in range(2):
            h = 2 * i + hh
            y = y2[:, hh * LANE:(hh + 1) * LANE]
            if h < 2 * A_HEADS:
                inv = jax.lax.rsqrt(jnp.sum(y * y, axis=-1, keepdims=True) + EPS)
                y = y * (inv * (A_DK ** -0.5) if h < A_HEADS else inv)
            if h < A_HEADS:
                qn_ref[0, :, h * LANE:(h + 1) * LANE] = y.astype(BF16)
            elif h < 2 * A_HEADS:
                kn_ref[0, :, (h - A_HEADS) * LANE:(h - A_HEADS + 1) * LANE] = y.astype(BF16)
            else:
                vv_ref[0, :, (h - 2 * A_HEADS) * LANE:(h - 2 * A_HEADS + 1) * LANE] = y.astype(BF16)

    hc_ext = _dot(h_ext, wt_ref[:, T_CG:T_ZC]) * _dot(h_ext, wt_ref[:, T_H:T_BG])
    cx_ref[0] = (tail(T_BG, T_CG) * _silu(tail(T_ZC, T_QA)) * conv(hc_ext, 3 * A_QK, CIN_W, C_CONV)).astype(BF16)

    sza_ref[0] = _silu(_dot(hn, w_ref[:, S_ZA:W_HEAD])).astype(BF16)
    szb_ref[0] = _silu(tail(T_ZB, T_H)).astype(BF16)

    tab = tab_ref[...]
    cq, sq = tab[:, 0:LANE], tab[:, LANE:2 * LANE]
    ck, sk = tab[:, 2 * LANE:3 * LANE], tab[:, 3 * LANE:4 * LANE]

    qa_blk = tail(T_QA, T_CKV)
    qa = qa_blk[:, :B_Q_LORA]
    nq =(qa * jax.lax.rsqrt(jnp.mean(qa * qa, axis=-1, keepdims=True) + EPS) * qng_ref[...]).astype(BF16)
    qmain = _dot(nq, wqm_ref[...])
    qswap = _dot(nq, wqs_ref[...])
    for h in range(B_HEADS):
        sl = slice(h * LANE, (h + 1) * LANE)
        qm_ref[0, :, sl] = (qmain[:, sl] * cq + qswap[:, sl] * sq).astype(BF16)

    ckv_blk = tail(T_CKV, W_TAIL)
    ckv = ckv_blk[:, :B_KV_LORA]
    nkv =(ckv * jax.lax.rsqrt(jnp.mean(ckv * ckv, axis=-1, keepdims=True) + EPS) * kvng_ref[...]).astype(BF16)
    kk = _dot(nkv, wk_ref[...])
    vt = _dot_nt(wv_ref[...], nkv)
    vrow = jax.lax.broadcasted_iota(jnp.int32, vt.shape, 0)
    ones_row = functools.reduce(jnp.logical_or, [vrow == h * B_VA + B_V for h in range(B_HEADS)])
    vm_ref[0] = jnp.where(ones_row, 1.0, vt).astype(BF16)
    kpm = ckv_blk[:, B_KV_LORA:]
    kpe = kpm * ck + qa_blk[:, LANE:] * sk
    for h in range(B_HEADS):
        sl = slice(h * LANE, (h + 1) * LANE)
        km_ref[0, :, sl] = (kk[:, sl] + kpe).astype(BF16)

    lane = jax.lax.broadcasted_iota(jnp.int32, kpm.shape, 1)
    z = kpm + gsc_ref[1:2, :]
    sp = jnp.maximum(z, 0.0) + jnp.log1p(jnp.exp(-jnp.abs(z)))
    g = -jnp.exp(gsc_ref[0:1, :]) * sp
    gates_ref[0] = jnp.where(lane < 2 * A_HEADS, jax.nn.sigmoid(kpm), jnp.where(lane < 4 * A_HEADS, g, 0.0))


def _layer_spec(a, layer):
    return pl.BlockSpec((None,) + a.shape[1:], lambda i, j: (layer,) + (0,) * (a.ndim - 1))


def _inproj(x, mod, per_batch, layer, ng, w, wt, qng, wqm, wqs, kvng, wk, wv, tab, gsc, wconv, tm):
    b, t, d = x.shape
    tm = min(tm, t)
    r = tm // HALO
    nh = t // HALO
    bm = (lambda i, j: (layer, i, 0, 0)) if per_batch else (lambda i, j: (layer, mod.shape[1] - 1, 0, 0))
    tok = lambda width: pl.BlockSpec((1, tm, width), lambda i, j: (i, j, 0))
    full = lambda a: pl.BlockSpec(a.shape, lambda i, j: (0,) * a.ndim)
    lay = lambda a: _layer_spec(a, layer)
    widths = (A_QK, A_QK, A_W, C_W, A_W, LANE, B_HEADS * LANE, B_HEADS * LANE, B_W, B_W)
    dtypes = (BF16, BF16, BF16, BF16, BF16, F32, BF16, BF16, BF16, BF16)
    V_OUT = 8
    return pl.pallas_call(
        functools.partial(_inproj_kernel, tm=tm),
        grid=(b, t // tm),
        in_specs=[tok(d),
                  pl.BlockSpec((1, HALO, d), lambda i, j: (i, jnp.maximum(j * r - 1, 0), 0)),
                  pl.BlockSpec((1, HALO, d), lambda i, j: (i, jnp.minimum((j + 1) * r, nh - 1), 0)),
                  pl.BlockSpec((None, 1, 3, d), bm), lay(ng),
                  pl.BlockSpec((None, d, W_HEAD), lambda i, j: (layer, 0, 0)), lay(wt), lay(qng), lay(wqm), lay(wqs),
                  lay(kvng), lay(wk), lay(wv),
                  pl.BlockSpec((tm, 4 * LANE), lambda i, j: (j, 0)), lay(gsc), lay(wconv)],
        out_specs=[pl.BlockSpec((1, B_WA, tm), lambda i, j: (i, 0, j)) if n == V_OUT else tok(wd)
                   for n, wd in enumerate(widths)],
        out_shape=[jax.ShapeDtypeStruct((b, B_WA, t) if n == V_OUT else (b, t, wd), dt)
                   for n, (wd, dt) in enumerate(zip(widths, dtypes))],
        compiler_params=_params(("parallel", "parallel")),
        name="inproj",
    )(x, x, x, mod, ng, w, wt, qng, wqm, wqs, kvng, wk, wv, tab, gsc, wconv)


INV_BASE_LOG2 = 3


def _unit_triangular_inverses(mats, c):
    row = jax.lax.broadcasted_iota(jnp.int32, (c, c), 0)
    col = jax.lax.broadcasted_iota(jnp.int32, (c, c), 1)
    same = lambda k: (row >> k) == (col >> k)
    eye = jnp.where(row == col, 1.0, 0.0)
    base = same(INV_BASE_LOG2)
    a0 = [jnp.where(base, a, 0.0) for a in mats]
    t = [eye - ai for ai in a0]
    apow = [ai.astype(BF16) for ai in a0]
    for _ in range(INV_BASE_LOG2 - 1):
        apow = [_dot(ap, ap).astype(BF16) for ap in apow]
        t = [ti + _dot(ti.astype(BF16), ap) for ti, ap in zip(t, apow)]
    for k in range(INV_BASE_LOG2, int(math.log2(c))):
        coupling = same(k + 1) & jnp.logical_not(same(k))
        e = [jnp.where(coupling, a, 0.0).astype(BF16) for a in mats]
        tb = [ti.astype(BF16) for ti in t]
        te = [_dot(tbi, ei).astype(BF16) for tbi, ei in zip(tb, e)]
        t = [ti - _dot(tei, tbi) for ti, tei, tbi in zip(t, te, tb)]
    return t


def _scan_kernel(qf_ref, kf_ref, vf_ref, gf_ref, qb_ref, kb_ref, vb_ref, gb_ref, s0_ref,
                 of_ref, ob_ref, s_ref, *, c, bb):
    n = pl.program_id(1)

    @pl.when(n == 0)
    def _():
        s_ref[...] = s0_ref[...]

    row = jax.lax.broadcasted_iota(jnp.int32, (c, c), 0)
    col = jax.lax.broadcasted_iota(jnp.int32, (c, c), 1)
    dirs = ((qf_ref, kf_ref, vf_ref, gf_ref), (qb_ref, kb_ref, vb_ref, gb_ref))
    o_refs = (of_ref, ob_ref)
    incl = (col <= row, col >= row)
    strict = (col < row, col > row)
    gates, gcb, gcb_t, gl_row = {}, {}, {}, {}
    for e in range(bb):
        for d in range(2):
            g = dirs[d][3][e]
            tri = jnp.where(incl[d], 1.0, 0.0).astype(BF16)
            hi = g.astype(BF16)
            r1 = g - hi.astype(F32)
            mid = r1.astype(BF16)
            lo = (r1 - mid.astype(F32)).astype(BF16)
            gc_all = _dot(tri, hi) + _dot(tri, mid) + _dot(tri, lo)
            gates[e, d] = g
            gcb[e, d] = gc_all
            gcb_t[e, d] = gc_all.T
            gl_row[e, d] = gc_all[c - 1:c, :] if d == 0 else gc_all[0:1, :]

    chains = [(e, d, h) for e in range(bb) for d in range(2) for h in range(A_HEADS)]
    sls = [slice(h * LANE, (h + 1) * LANE) for _, _, h in chains]
    jbs = [A_HEADS * d + h for _, d, h in chains]
    jgs = [2 * A_HEADS + jb for jb in jbs]
    beta = [gates[e, d][:, jb:jb + 1] for (e, d, _), jb in zip(chains, jbs)]
    gc = [gcb[e, d][:, jg:jg + 1] for (e, d, _), jg in zip(chains, jgs)]
    gcr = [gcb_t[e, d][jg:jg + 1, :] for (e, d, _), jg in zip(chains, jgs)]
    gl = [gl_row[e, d][:, jg:jg + 1] for (e, d, _), jg in zip(chains, jgs)]
    q = [dirs[d][0][e, :, sl] for (e, d, _), sl in zip(chains, sls)]
    k = [dirs[d][1][e, :, sl] for (e, d, _), sl in zip(chains, sls)]
    v = [dirs[d][2][e, :, sl] for (e, d, _), sl in zip(chains, sls)]
    decay = [jnp.where(incl[d], jnp.exp(jnp.where(incl[d], gci - gcri, 0.0)), 0.0)
             for (_, d, _), gci, gcri in zip(chains, gc, gcr)]
    kbeta = [ki.astype(F32) * bi for ki, bi in zip(k, beta)]
    egc = [jnp.exp(gci) for gci in gc]
    a = [jnp.where(strict[d], _dot_nt(kbi.astype(BF16), ki) * di, 0.0)
         for (_, d, _), kbi, ki, di in zip(chains, kbeta, k, decay)]
    qk = [(_dot_nt(qi, ki) * di).astype(BF16) for qi, ki, di in zip(q, k, decay)]
    rhs = [jnp.concatenate([vi.astype(F32) * bi, kbi * ei], axis=1).astype(BF16)
           for vi, bi, kbi, ei in zip(v, beta, kbeta, egc)]
    kd_t = [(ki.astype(F32) * jnp.exp(gli - gci)).T.astype(BF16) for ki, gli, gci in zip(k, gl, gc)]
    tinv = [ti.astype(BF16) for ti in _unit_triangular_inverses(a, c)]
    uw = [_dot(ti, ri) for ti, ri in zip(tinv, rhs)]
    qdk = [jnp.concatenate([(qi.astype(F32) * ei).astype(BF16), qki], axis=1) for qi, ei, qki in zip(q, egc, qk)]
    s = [s_ref[e, d, h] for e, d, h in chains]
    sb = [si.astype(BF16) for si in s]
    v_new = [(uwi[:, :A_DV] - _dot(uwi[:, A_DV:].astype(BF16), sbi)).astype(BF16) for uwi, sbi in zip(uw, sb)]
    o = [_dot(qi, jnp.concatenate([sbi, vi], axis=0)) for qi, sbi, vi in zip(qdk, sb, v_new)]
    for (e, d, h), sl, oi in zip(chains, sls, o):
        o_refs[d][e, :, sl] = oi.astype(BF16)
    for (e, d, h), gli, si, kdi, vi in zip(chains, gl, s, kd_t, v_new):
        s_ref[e, d, h] = jnp.exp(gli) * si + _dot(kdi, vi)


def _scan(qn, kn, vv, gates, s0, bb):
    b, t, _ = qn.shape
    c = GDN_CHUNK
    n = t // c
    assert b % bb == 0
    fwd = lambda width: pl.BlockSpec((bb, c, width), lambda i, j: (i, j, 0))
    bwd = lambda width: pl.BlockSpec((bb, c, width), lambda i, j: (i, n - 1 - j, 0))
    st = pl.BlockSpec((bb, 2, A_HEADS, A_DK, A_DV), lambda i, j: (i, 0, 0, 0, 0))
    return pl.pallas_call(
        functools.partial(_scan_kernel, c=c, bb=bb),
        grid=(b // bb, n),
        in_specs=[fwd(A_QK), fwd(A_QK), fwd(A_W), fwd(LANE), bwd(A_QK), bwd(A_QK), bwd(A_W), bwd(LANE), st],
        out_specs=[fwd(A_W), bwd(A_W), st],
        out_shape=[jax.ShapeDtypeStruct((b, t, A_W), BF16), jax.ShapeDtypeStruct((b, t, A_W), BF16),
                   jax.ShapeDtypeStruct(s0.shape, F32)],
        compiler_params=_params(("parallel", "arbitrary")),
        name="gdn_scan",
    )(qn, kn, vv, gates, qn, kn, vv, gates, s0)


def _attention_tile(q_ref, kv_refs, szb_ref):
    n_src = len(kv_refs) // 2
    heads = range(B_HEADS)
    qh = [q_ref[0, :, h * LANE:(h + 1) * LANE] for h in heads]
    blocks = [(kv_refs[2 * i], kv_refs[2 * i + 1], slice(j * ATTN_KEY_BLOCK, (j + 1) * ATTN_KEY_BLOCK))
              for i in range(n_src) for j in range(kv_refs[2 * i].shape[1] // ATTN_KEY_BLOCK)]

    def scores(blk):
        k_ref, _, ks = blk
        return [_dot_nt(k_ref[0, ks, h * LANE:(h + 1) * LANE], qh[h]) for h in heads]

    m = acc = None
    s_next = scores(blocks[0])
    for n, (_, vt_ref, ks) in enumerate(blocks):
        s, s_next = s_next, (scores(blocks[n + 1]) if n + 1 < len(blocks) else None)
        mb = [si.max(axis=0, keepdims=True) for si in s]
        m_new = mb if m is None else [jnp.maximum(mi, mbi) for mi, mbi in zip(m, mb)]
        p = [jnp.exp2(si - mn).astype(BF16) for si, mn in zip(s, m_new)]
        pv = [_dot(vt_ref[0, h * B_VA:(h + 1) * B_VA, ks], p[h]) for h in heads]
        if m is None:
            acc = pv
        else:
            acc = [jnp.exp2(mi - mn) * ai + pvi for mi, mn, ai, pvi in zip(m, m_new, acc, pv)]
        m = m_new
    o = jnp.concatenate([a[:B_V] / a[B_V:B_V + 1] for a in acc], axis=0).T
    return (o * szb_ref[0].astype(F32)).astype(BF16)


def _attn_out_kernel(*refs, n_src, final):
    q_ref = refs[0]
    kv_refs = refs[1:1 + 2 * n_src]
    (szb_ref, x_ref, of_ref, ob_ref, sza_ref, cx_ref, gng_ref, mod_ref, w_ref, fg_ref, o_ref) = refs[1 + 2 * n_src:]
    bx = _attention_tile(q_ref, kv_refs, szb_ref)
    o = of_ref[0].astype(F32) + ob_ref[0].astype(F32)
    sza = sza_ref[0].astype(F32)
    acc = None
    for h in range(A_HEADS):
        sl = slice(h * LANE, (h + 1) * LANE)
        oh = o[:, sl]
        ah = oh * jax.lax.rsqrt(jnp.mean(oh * oh, axis=-1, keepdims=True) + EPS) * gng_ref[...]
        term = _dot((ah * sza[:, sl]).astype(BF16), w_ref[sl, :])
        acc = term if acc is None else acc + term
    acc = acc + _dot(bx, w_ref[A_W:A_W + B_W, :]) + _dot(cx_ref[0], w_ref[A_W + B_W:, :])
    xn = x_ref[0] + mod_ref[0, 2:3, :] * acc
    if final:
        xn = xn * jax.lax.rsqrt(jnp.mean(xn * xn, axis=-1, keepdims=True) + EPS) * fg_ref[...]
    o_ref[0] = xn


def _attn_out(x, q, kvs, szb, of, ob, sza, cx, gng, mod, per_batch, layer, w, fg, final, tm):
    b, t, d = x.shape
    tm = min(tm, t)
    bm = (lambda i, j: (layer, i, 0, 0)) if per_batch else (lambda i, j: (layer, mod.shape[1] - 1, 0, 0))
    tok = lambda width: pl.BlockSpec((1, tm, width), lambda i, j: (i, j, 0))
    full = lambda a: pl.BlockSpec(a.shape, lambda i, j: (0,) * a.ndim)
    in_specs = [tok(B_HEADS * LANE)]
    args = [q]
    for k, v in kvs:
        in_specs.append(pl.BlockSpec((1,) + k.shape[1:], lambda i, j: (i, 0, 0)))
        in_specs.append(pl.BlockSpec((1,) + v.shape[1:], lambda i, j: (i, 0, 0)))
        args += [k, v]
    in_specs += [tok(B_W), tok(d), tok(A_W), tok(A_W), tok(A_W), tok(C_W), _layer_spec(gng, layer),
                 pl.BlockSpec((None, 1, 3, d), bm), _layer_spec(w, layer), full(fg)]
    args += [szb, x, of, ob, sza, cx, gng, mod, w, fg]
    return pl.pallas_call(
        functools.partial(_attn_out_kernel, n_src=len(kvs), final=final),
        grid=(b, t // tm),
        in_specs=in_specs,
        out_specs=tok(d),
        out_shape=jax.ShapeDtypeStruct((b, t, d), F32),
        compiler_params=_params(("parallel", "parallel")),
        name="attn_out",
    )(*args)


def _deinterleave(w32):
    even, odd = w32[..., 0::2], w32[..., 1::2]
    return jnp.concatenate([even, odd], axis=-1), jnp.concatenate([odd, even], axis=-1)


def _layout_w_in(w):
    head = w[..., :W_HEAD].astype(BF16)
    w = w[..., W_HEAD:].astype(BF16)
    z = lambda n: jnp.zeros(w.shape[:-1] + (n,), w.dtype)
    o_b, o_qa, o_kv, o_zb, o_end = (o - W_HEAD for o in (2048, 2064, 2256, 2416, 3696))
    kp_main, kp_swap = _deinterleave(w[..., o_kv + B_KV_LORA:o_zb])
    cols = [w[..., o_zb:o_end],
            w[..., o_qa:o_kv], kp_swap, z(2 * LANE - B_Q_LORA - B_ROPE),
            w[..., o_kv:o_kv + B_KV_LORA], w[..., o_b:o_qa], z(B_NOPE - 4 * A_HEADS), kp_main,
            z(LANE - B_NOPE - B_ROPE)]
    tail = jnp.concatenate(cols, axis=-1)
    assert tail.shape[-1] == W_TAIL
    return head, tail


def _layout_w_qb(w):
    w = w.astype(BF16)
    z = lambda n: jnp.zeros(w.shape[:-1] + (n,), w.dtype)
    hd = B_NOPE + B_ROPE
    main, swap = [], []
    for h in range(B_HEADS):
        rm, rs = _deinterleave(w[..., h * hd + B_NOPE:(h + 1) * hd])
        main += [w[..., h * hd:h * hd + B_NOPE], rm, z(LANE - hd)]
        swap += [z(B_NOPE), rs, z(LANE - hd)]
    return jnp.concatenate(main, axis=-1), jnp.concatenate(swap, axis=-1)


def _layout_w_kvb(w):
    w = w.astype(BF16)
    hd = B_NOPE + B_V
    wk, wv = [], []
    for h in range(B_HEADS):
        wk += [w[..., h * hd:h * hd + B_NOPE], jnp.zeros(w.shape[:-1] + (LANE - B_NOPE,), w.dtype)]
        wv += [w[..., h * hd + B_NOPE:(h + 1) * hd], jnp.zeros(w.shape[:-1] + (B_VA - B_V,), w.dtype)]
    return jnp.concatenate(wk, axis=-1), jnp.swapaxes(jnp.concatenate(wv, axis=-1), -1, -2)


def _rope_tables(t, rotate):
    half = B_ROPE // 2
    if rotate:
        pos = np.arange(t)
        n_freq = B_ROPE // 4
        inv_freq = ROPE_THETA ** (-np.arange(n_freq, dtype=np.float64) / n_freq)
        ang = np.concatenate([(pos // GRID_W)[:, None] * inv_freq, (pos % GRID_W)[:, None] * inv_freq], axis=-1)
        cos, sin = np.cos(ang).astype(np.float32), np.sin(ang).astype(np.float32)
    else:
        cos, sin = np.ones((t, half), np.float32), np.zeros((t, half), np.float32)
    z = lambda n: np.zeros((t, n), np.float32)
    tail = LANE - B_NOPE - B_ROPE
    scale = np.float32((B_NOPE + B_ROPE) ** -0.5 * math.log2(math.e))
    cq = np.concatenate([np.ones((t, B_NOPE), np.float32), cos, cos, z(tail)], axis=1) * scale
    sq = np.concatenate([z(B_NOPE), -sin, sin, z(tail)], axis=1) * scale
    ck = np.concatenate([z(B_NOPE), cos, cos, z(tail)], axis=1)
    sk = np.concatenate([z(B_NOPE), -sin, sin, z(tail)], axis=1)
    return jnp.asarray(np.concatenate([cq, sq, ck, sk], axis=1), dtype=F32)


def kernel(x, c, ctx, c_ctx, w_ada, b_ada, norm_g, w_in, gdn_conv, gdn_a_log, gdn_dt_bias, gdn_norm_g,
           mla_q_norm_g, mla_w_qb, mla_kv_norm_g, mla_w_kvb, conv_w, w_out, final_norm_g):
    bsz, t, d = x.shape
    t_ctx = ctx.shape[1]
    depth = w_ada.shape[0]
    assert t % GDN_CHUNK == 0 and t_ctx % GDN_CHUNK == 0 and t % GRID_W == 0
    tm = 512
    bb = 2 if bsz % 2 == 0 else 1

    rows = -(-(bsz + 1) // 8) * 8
    cc = jnp.concatenate([c, jnp.zeros((rows - bsz - 1, d), F32), c_ctx[None, :]], axis=0)
    mod = _ada(cc, w_ada, b_ada).reshape(depth, rows, 3, d)

    tab_x = _rope_tables(t, True)
    tab_c = _rope_tables(t_ctx, False)
    fg = final_norm_g.reshape(1, d)
    w_all, wt_all = _layout_w_in(w_in)
    wqm_all, wqs_all = _layout_w_qb(mla_w_qb)
    wk_all, wv_all = _layout_w_kvb(mla_w_kvb)
    wo_all = w_out.astype(BF16)
    ng = norm_g.reshape(depth, 1, d)
    qng = mla_q_norm_g.reshape(depth, 1, B_Q_LORA)
    kvng = mla_kv_norm_g.reshape(depth, 1, B_KV_LORA)
    gng = gdn_norm_g.reshape(depth, 1, A_DV)
    gsc = jnp.pad(jnp.stack([gdn_a_log.reshape(depth, -1), gdn_dt_bias.reshape(depth, -1)], axis=1),
                  ((0, 0), (0, 6), (2 * A_HEADS, LANE - 4 * A_HEADS)))
    tap_pad = (A_CONV - C_CONV) // 2
    wconv = jnp.pad(jnp.concatenate([gdn_conv, jnp.pad(conv_w, ((0, 0), (tap_pad, tap_pad), (0, 0)))], axis=2),
                    ((0, 0), (0, 8 - A_CONV), (0, 0)))
    s_zero = jnp.zeros((bsz, 2, A_HEADS, A_DK, A_DV), F32)

    for l in range(depth):
        last = l == depth - 1
        shared = (l, ng, w_all, wt_all, qng, wqm_all, wqs_all, kvng, wk_all, wv_all)
        (qn_c, kn_c, vv_c, cx_c, sza_c, gates_c, qm_c, km_c, vm_c, szb_c) = _inproj(
            ctx, mod, False, *shared, tab_c, gsc, wconv, tm)
        (qn_x, kn_x, vv_x, cx_x, sza_x, gates_x, qm_x, km_x, vm_x, szb_x) = _inproj(
            x, mod, True, *shared, tab_x, gsc, wconv, tm)

        of_c, ob_c, s_ctx = _scan(qn_c, kn_c, vv_c, gates_c, s_zero, bb)
        of_x, ob_x, _ = _scan(qn_x, kn_x, vv_x, gates_x, s_ctx, bb)

        x_new = _attn_out(x, qm_x, [(km_x, vm_x), (km_c, vm_c)], szb_x, of_x, ob_x, sza_x, cx_x, gng, mod, True,
                          l, wo_all, fg, last, tm)
        if not last:
            ctx = _attn_out(ctx, qm_c, [(km_c, vm_c)], szb_c, of_c, ob_c, sza_c, cx_c, gng, mod, False,
                            l, wo_all, fg, False, tm)
        x = x_new
    return x
```

```python
import functools
import math

import jax
import jax.numpy as jnp
import numpy as np
from jax.experimental import pallas as pl
from jax.experimental.pallas import tpu as pltpu

F32 = jnp.float32
BF16 = jnp.bfloat16

EPS = 1e-6
GRID_W = 64
ROPE_THETA = 10000.0

A_HEADS = 4
A_DK = 128
A_DV = 128
A_QK = A_HEADS * A_DK
A_W = A_HEADS * A_DV
A_CONV = 5
B_HEADS = 4
B_NOPE = 64
B_ROPE = 32
B_V = 64
B_Q_LORA = 192
B_KV_LORA = 128
B_W = B_HEADS * B_V
B_VA = B_V + 16
B_WA = B_HEADS * B_VA
C_W = 256
C_CONV = 3

LANE = 128
HALO = 16
GDN_CHUNK = 128
ATTN_KEY_BLOCK = 256
VMEM_LIMIT = 48 * 1024 * 1024

W_HEAD = 2048
S_ZA = 1536
T_ZB = 0
T_H = 256
T_BG = 512
T_CG = 768
T_ZC = 1024
T_QA = 1280
T_CKV = 1536
W_TAIL = 1792
CIN_W = 3 * A_QK + C_W


def _dot(a, b):
    return jnp.dot(a, b, preferred_element_type=F32)


def _dot_nt(a, b):
    return jax.lax.dot_general(a, b, (((1,), (1,)), ((), ())), preferred_element_type=F32)


def _silu(x):
    return x * jax.nn.sigmoid(x)


def _params(sem):
    return pltpu.CompilerParams(dimension_semantics=sem, vmem_limit_bytes=VMEM_LIMIT)


def _ada_kernel(c_ref, w_ref, b_ref, o_ref):
    s = _silu(c_ref[...]).astype(BF16)
    o_ref[0] = _dot(s, w_ref[0].astype(BF16)) + b_ref[0]


def _ada(cc, w_ada, b_ada):
    depth, d, n3 = w_ada.shape
    r = cc.shape[0]
    tn = 768
    return pl.pallas_call(
        _ada_kernel,
        grid=(depth, n3 // tn),
        in_specs=[pl.BlockSpec((r, d), lambda l, j: (0, 0)),
                  pl.BlockSpec((1, d, tn), lambda l, j: (l, 0, j)),
                  pl.BlockSpec((1, 1, tn), lambda l, j: (l, 0, j))],
        out_specs=pl.BlockSpec((1, r, tn), lambda l, j: (l, 0, j)),
        out_shape=jax.ShapeDtypeStruct((depth, r, n3), F32),
        compiler_params=_params(("parallel", "parallel")),
        name="ada",
    )(cc, w_ada, b_ada.reshape(depth, 1, n3))


def _inproj_kernel(x_ref, xp_ref, xn_ref, mod_ref, ng_ref, w_ref, wt_ref, qng_ref, wqm_ref, wqs_ref, kvng_ref,
                   wk_ref, wv_ref, tab_ref, gsc_ref, wc_ref,
                   qn_ref, kn_ref, vv_ref, cx_ref, sza_ref, gates_ref, qm_ref, km_ref, vm_ref, szb_ref, *, tm):
    j = pl.program_id(1)
    nj = pl.num_programs(1)

    def normed(xv):
        y = xv * jax.lax.rsqrt(jnp.mean(xv * xv, axis=-1, keepdims=True) + EPS) * ng_ref[...]
        return y * (1.0 + mod_ref[0, 1:2, :]) + mod_ref[0, 0:1, :]

    hn = normed(x_ref[0]).astype(BF16)
    h_prev = jnp.where(j > 0, normed(xp_ref[0]), 0.0).astype(BF16)
    h_next = jnp.where(j < nj - 1, normed(xn_ref[0]), 0.0).astype(BF16)
    h_ext = jnp.concatenate([h_prev, hn, h_next], axis=0)

    def tail(a, b):
        return _dot(hn, wt_ref[:, a:b])

    def conv(u_ext, c0, c1, taps=A_CONV):
        pad = A_CONV // 2
        rows = u_ext.shape[0]
        acc = None
        for k in range(pad - taps // 2, pad + taps // 2 + 1):
            shifted = u_ext if k == pad else pltpu.roll(u_ext, (pad - k) % rows, 0)
            term = shifted[HALO:HALO + tm, :] * wc_ref[k:k + 1, c0:c1]
            acc = term if acc is None else acc + term
        return acc

    blk = 2 * LANE
    for i in range(3 * A_QK // blk):
        y2 = _silu(conv(_dot(h_ext, w_ref[:, i * blk:(i + 1) * blk]), i * blk, (i + 1) * blk))
        for hh in range(2):
            h = 2 * i + hh
            y = y2[:, hh * LANE:(hh + 1) * LANE]
            if h < 2 * A_HEADS:
                inv = jax.lax.rsqrt(jnp.sum(y * y, axis=-1, keepdims=True) + EPS)
                y = y * (inv * (A_DK ** -0.5) if h < A_HEADS else inv)
            if h < A_HEADS:
                qn_ref[0, :, h * LANE:(h + 1) * LANE] = y.astype(BF16)
            elif h < 2 * A_HEADS:
                kn_ref[0, :, (h - A_HEADS) * LANE:(h - A_HEADS + 1) * LANE] = y.astype(BF16)
            else:
                vv_ref[0, :, (h - 2 * A_HEADS) * LANE:(h - 2 * A_HEADS + 1) * LANE] = y.astype(BF16)

    hc_ext = _dot(h_ext, wt_ref[:, T_CG:T_ZC]) * _dot(h_ext, wt_ref[:, T_H:T_BG])
    cx_ref[0] = (tail(T_BG, T_CG) * _silu(tail(T_ZC, T_QA)) * conv(hc_ext, 3 * A_QK, CIN_W, C_CONV)).astype(BF16)

    sza_ref[0] = _silu(_dot(hn, w_ref[:, S_ZA:W_HEAD])).astype(BF16)
    szb_ref[0] = _silu(tail(T_ZB, T_H)).astype(BF16)

    tab = tab_ref[...]
    cq, sq = tab[:, 0:LANE], tab[:, LANE:2 * LANE]
    ck, sk = tab[:, 2 * LANE:3 * LANE], tab[:, 3 * LANE:4 * LANE]

    qa_blk = tail(T_QA, T_CKV)
    qa = qa_blk[:, :B_Q_LORA]
    nq =(qa * jax.lax.rsqrt(jnp.mean(qa * qa, axis=-1, keepdims=True) + EPS) * qng_ref[...]).astype(BF16)
    qmain = _dot(nq, wqm_ref[...])
    qswap = _dot(nq, wqs_ref[...])
    for h in range(B_HEADS):
        sl = slice(h * LANE, (h + 1) * LANE)
        qm_ref[0, :, sl] = (qmain[:, sl] * cq + qswap[:, sl] * sq).astype(BF16)

    ckv_blk = tail(T_CKV, W_TAIL)
    ckv = ckv_blk[:, :B_KV_LORA]
    nkv =(ckv * jax.lax.rsqrt(jnp.mean(ckv * ckv, axis=-1, keepdims=True) + EPS) * kvng_ref[...]).astype(BF16)
    kk = _dot(nkv, wk_ref[...])
    vt = _dot_nt(wv_ref[...], nkv)
    vrow = jax.lax.broadcasted_iota(jnp.int32, vt.shape, 0)
    ones_row = functools.reduce(jnp.logical_or, [vrow == h * B_VA + B_V for h in range(B_HEADS)])
    vm_ref[0] = jnp.where(ones_row, 1.0, vt).astype(BF16)
    kpm = ckv_blk[:, B_KV_LORA:]
    kpe = kpm * ck + qa_blk[:, LANE:] * sk
    for h in range(B_HEADS):
        sl = slice(h * LANE, (h + 1) * LANE)
        km_ref[0, :, sl] = (kk[:, sl] + kpe).astype(BF16)

    lane = jax.lax.broadcasted_iota(jnp.int32, kpm.shape, 1)
    z = kpm + gsc_ref[1:2, :]
    sp = jnp.maximum(z, 0.0) + jnp.log1p(jnp.exp(-jnp.abs(z)))
    g = -jnp.exp(gsc_ref[0:1, :]) * sp
    gates_ref[0] = jnp.where(lane < 2 * A_HEADS, jax.nn.sigmoid(kpm), jnp.where(lane < 4 * A_HEADS, g, 0.0))


def _layer_spec(a, layer):
    return pl.BlockSpec((None,) + a.shape[1:], lambda i, j: (layer,) + (0,) * (a.ndim - 1))


def _inproj(x, mod, per_batch, layer, ng, w, wt, qng, wqm, wqs, kvng, wk, wv, tab, gsc, wconv, tm):
    b, t, d = x.shape
    tm = min(tm, t)
    r = tm // HALO
    nh = t // HALO
    bm = (lambda i, j: (layer, i, 0, 0)) if per_batch else (lambda i, j: (layer, mod.shape[1] - 1, 0, 0))
    tok = lambda width: pl.BlockSpec((1, tm, width), lambda i, j: (i, j, 0))
    full = lambda a: pl.BlockSpec(a.shape, lambda i, j: (0,) * a.ndim)
    lay = lambda a: _layer_spec(a, layer)
    widths = (A_QK, A_QK, A_W, C_W, A_W, LANE, B_HEADS * LANE, B_HEADS * LANE, B_W, B_W)
    dtypes = (BF16, BF16, BF16, BF16, BF16, F32, BF16, BF16, BF16, BF16)
    V_OUT = 8
    return pl.pallas_call(
        functools.partial(_inproj_kernel, tm=tm),
        grid=(b, t // tm),
        in_specs=[tok(d),
                  pl.BlockSpec((1, HALO, d), lambda i, j: (i, jnp.maximum(j * r - 1, 0), 0)),
                  pl.BlockSpec((1, HALO, d), lambda i, j: (i, jnp.minimum((j + 1) * r, nh - 1), 0)),
                  pl.BlockSpec((None, 1, 3, d), bm), lay(ng),
                  pl.BlockSpec((None, d, W_HEAD), lambda i, j: (layer, 0, 0)), lay(wt), lay(qng), lay(wqm), lay(wqs),
                  lay(kvng), lay(wk), lay(wv),
                  pl.BlockSpec((tm, 4 * LANE), lambda i, j: (j, 0)), lay(gsc), lay(wconv)],
        out_specs=[pl.BlockSpec((1, B_WA, tm), lambda i, j: (i, 0, j)) if n == V_OUT else tok(wd)
                   for n, wd in enumerate(widths)],
        out_shape=[jax.ShapeDtypeStruct((b, B_WA, t) if n == V_OUT else (b, t, wd), dt)
                   for n, (wd, dt) in enumerate(zip(widths, dtypes))],
        compiler_params=_params(("parallel", "parallel")),
        name="inproj",
    )(x, x, x, mod, ng, w, wt, qng, wqm, wqs, kvng, wk, wv, tab, gsc, wconv)


INV_BASE_LOG2 = 3

def _unit_triangular_inverses(mats, c):
    row = jax.lax.broadcasted_iota(jnp.int32, (c, c), 0)
    col = jax.lax.broadcasted_iota(jnp.int32, (c, c), 1)
    same = lambda k: (row >> k) == (col >> k)
    eye = jnp.where(row == col, 1.0, 0.0)
    base = same(INV_BASE_LOG2)
    a0 = [jnp.where(base, a, 0.0) for a in mats]
    t = [eye - ai for ai in a0]
    apow = [ai.astype(BF16) for ai in a0]
    for _ in range(INV_BASE_LOG2 - 1):
        apow = [_dot(ap, ap).astype(BF16) for ap in apow]
        t = [ti + _dot(ti.astype(BF16), ap) for ti, ap in zip(t, apow)]
    for k in range(INV_BASE_LOG2, int(math.log2(c))):
        coupling = same(k + 1) & jnp.logical_not(same(k))
        e = [jnp.where(coupling, a, 0.0).astype(BF16) for a in mats]
        tb = [ti.astype(BF16) for ti in t]
        te = [_dot(tbi, ei).astype(BF16) for tbi, ei in zip(tb, e)]
        t = [ti - _dot(tei, tbi) for ti, tei, tbi in zip(t, te, tb)]
    return t


def _scan_kernel(qf_ref, kf_ref, vf_ref, gf_ref, qb_ref, kb_ref, vb_ref, gb_ref, s0_ref,
                 of_ref, ob_ref, s_ref, *, c, bb):
    n = pl.program_id(1)

    @pl.when(n == 0)
    def _():
        s_ref[...] = s0_ref[...]

    row = jax.lax.broadcasted_iota(jnp.int32, (c, c), 0)
    col = jax.lax.broadcasted_iota(jnp.int32, (c, c), 1)
    dirs = ((qf_ref, kf_ref, vf_ref, gf_ref), (qb_ref, kb_ref, vb_ref, gb_ref))
    o_refs = (of_ref, ob_ref)
    incl = (col <= row, col >= row)
    strict = (col < row, col > row)
    gates, gcb, gcb_t, gl_row = {}, {}, {}, {}
    for e in range(bb):
        for d in range(2):
            g = dirs[d][3][e]
            tri = jnp.where(incl[d], 1.0, 0.0).astype(BF16)
            hi = g.astype(BF16)
            r1 = g - hi.astype(F32)
            mid = r1.astype(BF16)
            lo = (r1 - mid.astype(F32)).astype(BF16)
            gc_all = _dot(tri, hi) + _dot(tri, mid) + _dot(tri, lo)
            gates[e, d] = g
            gcb[e, d] = gc_all
            gcb_t[e, d] = gc_all.T
            gl_row[e, d] = gc_all[c - 1:c, :] if d == 0 else gc_all[0:1, :]

    chains = [(e, d, h) for e in range(bb) for d in range(2) for h in range(A_HEADS)]
    sls = [slice(h * LANE, (h + 1) * LANE) for _, _, h in chains]
    jbs = [A_HEADS * d + h for _, d, h in chains]
    jgs = [2 * A_HEADS + jb for jb in jbs]
    beta = [gates[e, d][:, jb:jb + 1] for (e, d, _), jb in zip(chains, jbs)]
    gc = [gcb[e, d][:, jg:jg + 1] for (e, d, _), jg in zip(chains, jgs)]
    gcr = [gcb_t[e, d][jg:jg + 1, :] for (e, d, _), jg in zip(chains, jgs)]
    gl = [gl_row[e, d][:, jg:jg + 1] for (e, d, _), jg in zip(chains, jgs)]
    q = [dirs[d][0][e, :, sl] for (e, d, _), sl in zip(chains, sls)]
    k = [dirs[d][1][e, :, sl] for (e, d, _), sl in zip(chains, sls)]
    v = [dirs[d][2][e, :, sl] for (e, d, _), sl in zip(chains, sls)]
    decay = [jnp.where(incl[d], jnp.exp(jnp.where(incl[d], gci - gcri, 0.0)), 0.0)
             for (_, d, _), gci, gcri in zip(chains, gc, gcr)]
    kbeta = [ki.astype(F32) * bi for ki, bi in zip(k, beta)]
    egc = [jnp.exp(gci) for gci in gc]
    a = [jnp.where(strict[d], _dot_nt(kbi.astype(BF16), ki) * di, 0.0)
         for (_, d, _), kbi, ki, di in zip(chains, kbeta, k, decay)]
    qk = [(_dot_nt(qi, ki) * di).astype(BF16) for qi, ki, di in zip(q, k, decay)]
    rhs = [jnp.concatenate([vi.astype(F32) * bi, kbi * ei], axis=1).astype(BF16)
           for vi, bi, kbi, ei in zip(v, beta, kbeta, egc)]
    kd_t = [(ki.astype(F32) * jnp.exp(gli - gci)).T.astype(BF16) for ki, gli, gci in zip(k, gl, gc)]
    tinv = [ti.astype(BF16) for ti in _unit_triangular_inverses(a, c)]
    uw = [_dot(ti, ri) for ti, ri in zip(tinv, rhs)]
    qdk = [jnp.concatenate([(qi.astype(F32) * ei).astype(BF16), qki], axis=1) for qi, ei, qki in zip(q, egc, qk)]
    s = [s_ref[e, d, h] for e, d, h in chains]
    sb = [si.astype(BF16) for si in s]
    v_new = [(uwi[:, :A_DV] - _dot(uwi[:, A_DV:].astype(BF16), sbi)).astype(BF16) for uwi, sbi in zip(uw, sb)]
    o = [_dot(qi, jnp.concatenate([sbi, vi], axis=0)) for qi, sbi, vi in zip(qdk, sb, v_new)]
    for (e, d, h), sl, oi in zip(chains, sls, o):
        o_refs[d][e, :, sl] = oi.astype(BF16)
    for (e, d, h), gli, si, kdi, vi in zip(chains, gl, s, kd_t, v_new):
        s_ref[e, d, h] = jnp.exp(gli) * si + _dot(kdi, vi)


def _scan(qn, kn, vv, gates, s0, bb):
    b, t, _ = qn.shape
    c = GDN_CHUNK
    n = t // c
    assert b % bb == 0
    fwd = lambda width: pl.BlockSpec((bb, c, width), lambda i, j: (i, j, 0))
    bwd = lambda width: pl.BlockSpec((bb, c, width), lambda i, j: (i, n - 1 - j, 0))
    st = pl.BlockSpec((bb, 2, A_HEADS, A_DK, A_DV), lambda i, j: (i, 0, 0, 0, 0))
    return pl.pallas_call(
        functools.partial(_scan_kernel, c=c, bb=bb),
        grid=(b // bb, n),
        in_specs=[fwd(A_QK), fwd(A_QK), fwd(A_W), fwd(LANE), bwd(A_QK), bwd(A_QK), bwd(A_W), bwd(LANE), st],
        out_specs=[fwd(A_W), bwd(A_W), st],
        out_shape=[jax.ShapeDtypeStruct((b, t, A_W), BF16), jax.ShapeDtypeStruct((b, t, A_W), BF16),
                   jax.ShapeDtypeStruct(s0.shape, F32)],
        compiler_params=_params(("parallel", "arbitrary")),
        name="gdn_scan",
    )(qn, kn, vv, gates, qn, kn, vv, gates, s0)


def _attention_tile(q_ref, kv_refs, szb_ref):
    n_src = len(kv_refs) // 2
    heads = range(B_HEADS)
    qh = [q_ref[0, :, h * LANE:(h + 1) * LANE] for h in heads]
    blocks = []
    for i in range(n_src):
        tk = kv_refs[2 * i].shape[1]
        kb = min(ATTN_KEY_BLOCK, tk)
        blocks += [(kv_refs[2 * i], kv_refs[2 * i + 1], slice(j * kb, (j + 1) * kb)) for j in range(tk // kb)]

    def scores(blk):
        k_ref, _, ks = blk
        return [_dot_nt(k_ref[0, ks, h * LANE:(h + 1) * LANE], qh[h]) for h in heads]

    m = acc = None
    s_next = scores(blocks[0])
    for n, (_, vt_ref, ks) in enumerate(blocks):
        s, s_next = s_next, (scores(blocks[n + 1]) if n + 1 < len(blocks) else None)
        mb = [si.max(axis=0, keepdims=True) for si in s]
        m_new = mb if m is None else [jnp.maximum(mi, mbi) for mi, mbi in zip(m, mb)]
        p = [jnp.exp2(si - mn).astype(BF16) for si, mn in zip(s, m_new)]
        pv = [_dot(vt_ref[0, h * B_VA:(h + 1) * B_VA, ks], p[h]) for h in heads]
        if m is None:
            acc = pv
        else:
            acc = [jnp.exp2(mi - mn) * ai + pvi for mi, mn, ai, pvi in zip(m, m_new, acc, pv)]
        m = m_new
    o = jnp.concatenate([a[:B_V] / a[B_V:B_V + 1] for a in acc], axis=0).T
    return (o * szb_ref[0].astype(F32)).astype(BF16)


def _attn_out_kernel(*refs, n_src, final):
    q_ref = refs[0]
    kv_refs = refs[1:1 + 2 * n_src]
    (szb_ref, x_ref, of_ref, ob_ref, sza_ref, cx_ref, gng_ref, mod_ref, w_ref, fg_ref, o_ref) = refs[1 + 2 * n_src:]
    bx = _attention_tile(q_ref, kv_refs, szb_ref)
    o = of_ref[0].astype(F32) + ob_ref[0].astype(F32)
    sza = sza_ref[0].astype(F32)
    acc = None
    for h in range(A_HEADS):
        sl = slice(h * LANE, (h + 1) * LANE)
        oh = o[:, sl]
        ah = oh * jax.lax.rsqrt(jnp.mean(oh * oh, axis=-1, keepdims=True) + EPS) * gng_ref[...]
        term = _dot((ah * sza[:, sl]).astype(BF16), w_ref[sl, :])
        acc = term if acc is None else acc + term
    acc = acc + _dot(bx, w_ref[A_W:A_W + B_W, :]) + _dot(cx_ref[0], w_ref[A_W + B_W:, :])
    xn = x_ref[0] + mod_ref[0, 2:3, :] * acc
    if final:
        xn = xn * jax.lax.rsqrt(jnp.mean(xn * xn, axis=-1, keepdims=True) + EPS) * fg_ref[...]
    o_ref[0] = xn


def _attn_out(x, q, kvs, szb, of, ob, sza, cx, gng, mod, per_batch, layer, w, fg, final, tm):
    b, t, d = x.shape
    tm = min(tm, t)
    bm = (lambda i, j: (layer, i, 0, 0)) if per_batch else (lambda i, j: (layer, mod.shape[1] - 1, 0, 0))
    tok = lambda width: pl.BlockSpec((1, tm, width), lambda i, j: (i, j, 0))
    full = lambda a: pl.BlockSpec(a.shape, lambda i, j: (0,) * a.ndim)
    in_specs = [tok(B_HEADS * LANE)]
    args = [q]
    for k, v in kvs:
        in_specs.append(pl.BlockSpec((1,) + k.shape[1:], lambda i, j: (i, 0, 0)))
        in_specs.append(pl.BlockSpec((1,) + v.shape[1:], lambda i, j: (i, 0, 0)))
        args += [k, v]
    in_specs += [tok(B_W), tok(d), tok(A_W), tok(A_W), tok(A_W), tok(C_W), _layer_spec(gng, layer),
                 pl.BlockSpec((None, 1, 3, d), bm), _layer_spec(w, layer), full(fg)]
    args += [szb, x, of, ob, sza, cx, gng, mod, w, fg]
    return pl.pallas_call(
        functools.partial(_attn_out_kernel, n_src=len(kvs), final=final),
        grid=(b, t // tm),
        in_specs=in_specs,
        out_specs=tok(d),
        out_shape=jax.ShapeDtypeStruct((b, t, d), F32),
        compiler_params=_params(("parallel", "parallel")),
        name="attn_out",
    )(*args)


def _deinterleave(w32):
    even, odd = w32[..., 0::2], w32[..., 1::2]
    return jnp.concatenate([even, odd], axis=-1), jnp.concatenate([odd, even], axis=-1)


def _w_layout_kernel(w_ref, pm_ref, ps_ref, head_ref, tail_ref):
    w = w_ref[0]
    head_ref[0] = w[:, :W_HEAD].astype(BF16)
    o_b, o_qa, o_kv, o_zb, o_end = 2048, 2064, 2256, 2416, 3696
    kpe = w[:, o_kv + B_KV_LORA:o_zb].astype(BF16)
    kp_main = _dot(kpe, pm_ref[...])
    kp_swap = _dot(kpe, ps_ref[...])
    z = lambda n: jnp.zeros((w.shape[0], n), F32)
    cols = [w[:, o_zb:o_end],
            w[:, o_qa:o_kv], kp_swap, z(2 * LANE - B_Q_LORA - B_ROPE),
            w[:, o_kv:o_kv + B_KV_LORA], w[:, o_b:o_qa], z(B_NOPE - 4 * A_HEADS), kp_main,
            z(LANE - B_NOPE - B_ROPE)]
    tail_ref[0] = jnp.concatenate(cols, axis=-1).astype(BF16)


def _layout_w_in(w):
    depth, d, n_in = w.shape
    tk = 256
    src_main = np.concatenate([np.arange(0, B_ROPE, 2), np.arange(1, B_ROPE, 2)])
    src_swap = np.concatenate([np.arange(1, B_ROPE, 2), np.arange(0, B_ROPE, 2)])
    sel = lambda src: jnp.asarray(np.eye(B_ROPE, dtype=np.float32)[:, src], dtype=BF16)
    return pl.pallas_call(
        _w_layout_kernel,
        grid=(depth, d // tk),
        in_specs=[pl.BlockSpec((1, tk, n_in), lambda l, i: (l, i, 0)),
                  pl.BlockSpec((B_ROPE, B_ROPE), lambda l, i: (0, 0)),
                  pl.BlockSpec((B_ROPE, B_ROPE), lambda l, i: (0, 0))],
        out_specs=[pl.BlockSpec((1, tk, W_HEAD), lambda l, i: (l, i, 0)),
                   pl.BlockSpec((1, tk, W_TAIL), lambda l, i: (l, i, 0))],
        out_shape=[jax.ShapeDtypeStruct((depth, d, W_HEAD), BF16), jax.ShapeDtypeStruct((depth, d, W_TAIL), BF16)],
        compiler_params=_params(("parallel", "parallel")),
        name="w_layout",
    )(w, sel(src_main), sel(src_swap))


def _layout_w_qb(w):
    w = w.astype(BF16)
    z = lambda n: jnp.zeros(w.shape[:-1] + (n,), w.dtype)
    hd = B_NOPE + B_ROPE
    main, swap = [], []
    for h in range(B_HEADS):
        rm, rs = _deinterleave(w[..., h * hd + B_NOPE:(h + 1) * hd])
        main += [w[..., h * hd:h * hd + B_NOPE], rm, z(LANE - hd)]
        swap += [z(B_NOPE), rs, z(LANE - hd)]
    return jnp.concatenate(main, axis=-1), jnp.concatenate(swap, axis=-1)


def _layout_w_kvb(w):
    w = w.astype(BF16)
    hd = B_NOPE + B_V
    wk, wv = [], []
    for h in range(B_HEADS):
        wk += [w[..., h * hd:h * hd + B_NOPE], jnp.zeros(w.shape[:-1] + (LANE - B_NOPE,), w.dtype)]
        wv += [w[..., h * hd + B_NOPE:(h + 1) * hd], jnp.zeros(w.shape[:-1] + (B_VA - B_V,), w.dtype)]
    return jnp.concatenate(wk, axis=-1), jnp.swapaxes(jnp.concatenate(wv, axis=-1), -1, -2)


def _rope_tables(t, rotate):
    half = B_ROPE // 2
    if rotate:
        pos = np.arange(t)
        n_freq = B_ROPE // 4
        inv_freq = ROPE_THETA ** (-np.arange(n_freq, dtype=np.float64) / n_freq)
        ang = np.concatenate([(pos // GRID_W)[:, None] * inv_freq, (pos % GRID_W)[:, None] * inv_freq], axis=-1)
        cos, sin = np.cos(ang).astype(np.float32), np.sin(ang).astype(np.float32)
    else:
        cos, sin = np.ones((t, half), np.float32), np.zeros((t, half), np.float32)
    z = lambda n: np.zeros((t, n), np.float32)
    tail = LANE - B_NOPE - B_ROPE
    scale = np.float32((B_NOPE + B_ROPE) ** -0.5 * math.log2(math.e))
    cq = np.concatenate([np.ones((t, B_NOPE), np.float32), cos, cos, z(tail)], axis=1) * scale
    sq = np.concatenate([z(B_NOPE), -sin, sin, z(tail)], axis=1) * scale
    ck = np.concatenate([z(B_NOPE), cos, cos, z(tail)], axis=1)
    sk = np.concatenate([z(B_NOPE), -sin, sin, z(tail)], axis=1)
    return jnp.asarray(np.concatenate([cq, sq, ck, sk], axis=1), dtype=F32)


def kernel(x, c, ctx, c_ctx, w_ada, b_ada, norm_g, w_in, gdn_conv, gdn_a_log, gdn_dt_bias, gdn_norm_g,
           mla_q_norm_g, mla_w_qb, mla_kv_norm_g, mla_w_kvb, conv_w, w_out, final_norm_g):
    bsz, t, d = x.shape
    t_ctx = ctx.shape[1]
    depth = w_ada.shape[0]
    assert t % GDN_CHUNK == 0 and t_ctx % GDN_CHUNK == 0 and t % GRID_W == 0
    tm = 512
    bb = 2 if bsz % 2 == 0 else 1

    rows = -(-(bsz + 1) // 8) * 8
    cc = jnp.concatenate([c, jnp.zeros((rows - bsz - 1, d), F32), c_ctx[None, :]], axis=0)
    mod = _ada(cc, w_ada, b_ada).reshape(depth, rows, 3, d)

    tab_x = _rope_tables(t, True)
    tab_c = _rope_tables(t_ctx, False)
    fg = final_norm_g.reshape(1, d)
    w_all, wt_all = _layout_w_in(w_in)
    wqm_all, wqs_all = _layout_w_qb(mla_w_qb)
    wk_all, wv_all = _layout_w_kvb(mla_w_kvb)
    wo_all = w_out.astype(BF16)
    ng = norm_g.reshape(depth, 1, d)
    qng = mla_q_norm_g.reshape(depth, 1, B_Q_LORA)
    kvng = mla_kv_norm_g.reshape(depth, 1, B_KV_LORA)
    gng = gdn_norm_g.reshape(depth, 1, A_DV)
    gsc = jnp.pad(jnp.stack([gdn_a_log.reshape(depth, -1), gdn_dt_bias.reshape(depth, -1)], axis=1),
                  ((0, 0), (0, 6), (2 * A_HEADS, LANE - 4 * A_HEADS)))
    tap_pad = (A_CONV - C_CONV) // 2
    wconv = jnp.pad(jnp.concatenate([gdn_conv, jnp.pad(conv_w, ((0, 0), (tap_pad, tap_pad), (0, 0)))], axis=2),
                    ((0, 0), (0, 8 - A_CONV), (0, 0)))
    s_zero = jnp.zeros((bsz, 2, A_HEADS, A_DK, A_DV), F32)

    for l in range(depth):
        last = l == depth - 1
        shared = (l, ng, w_all, wt_all, qng, wqm_all, wqs_all, kvng, wk_all, wv_all)
        (qn_c, kn_c, vv_c, cx_c, sza_c, gates_c, qm_c, km_c, vm_c, szb_c) = _inproj(
            ctx, mod, False, *shared, tab_c, gsc, wconv, tm)
        (qn_x, kn_x, vv_x, cx_x, sza_x, gates_x, qm_x, km_x, vm_x, szb_x) = _inproj(
            x, mod, True, *shared, tab_x, gsc, wconv, tm)

        of_c, ob_c, s_ctx = _scan(qn_c, kn_c, vv_c, gates_c, s_zero, bb)
        of_x, ob_x, _ = _scan(qn_x, kn_x, vv_x, gates_x, s_ctx, bb)

        x_new = _attn_out(x, qm_x, [(km_x, vm_x), (km_c, vm_c)], szb_x, of_x, ob_x, sza_x, cx_x, gng, mod, True,
                          l, wo_all, fg, last, tm)
        if not last:
            ctx = _attn_out(ctx, qm_c, [(km_c, vm_c)], szb_c, of_c, ob_c, sza_c, cx_c, gng, mod, False,
                            l, wo_all, fg, False, tm)
        x = x_new
    return x
```

```python
import functools
import math

import jax
import jax.numpy as jnp
import numpy as np
from jax.experimental import pallas as pl
from jax.experimental.pallas import tpu as pltpu

F32 = jnp.float32
BF16 = jnp.bfloat16

EPS = 1e-6
GRID_W = 64
ROPE_THETA = 10000.0

A_HEADS = 4
A_DK = 128
A_DV = 128
A_QK = A_HEADS * A_DK
A_W = A_HEADS * A_DV
A_CONV = 5
B_HEADS = 4
B_NOPE = 64
B_ROPE = 32
B_V = 64
B_Q_LORA = 192
B_KV_LORA = 128
B_W = B_HEADS * B_V
B_VA = B_V + 16
B_WA = B_HEADS * B_VA
C_W = 256
C_CONV = 3

LANE = 128
HALO = 16
GDN_CHUNK = 128
ATTN_KEY_BLOCK = 256
ATTN_HEAD_GROUP = 4
VMEM_LIMIT = 48 * 1024 * 1024

W_HEAD = 2048
S_ZA = 1536
T_ZB = 0
T_H = 256
T_BG = 512
T_CG = 768
T_ZC = 1024
T_QA = 1280
T_CKV = 1536
W_TAIL = 1792
CIN_W = 3 * A_QK + C_W


def _dot(a, b):
    return jnp.dot(a, b, preferred_element_type=F32)


def _dot_nt(a, b):
    return jax.lax.dot_general(a, b, (((1,), (1,)), ((), ())), preferred_element_type=F32)


def _silu(x):
    return x * jax.nn.sigmoid(x)


def _params(sem):
    return pltpu.CompilerParams(dimension_semantics=sem, vmem_limit_bytes=VMEM_LIMIT)


def _ada_kernel(c_ref, w_ref, b_ref, o_ref):
    s = _silu(c_ref[...]).astype(BF16)
    o_ref[0] = _dot(s, w_ref[0].astype(BF16)) + b_ref[0]


def _ada(cc, w_ada, b_ada):
    depth, d, n3 = w_ada.shape
    r = cc.shape[0]
    tn = 768
    return pl.pallas_call(
        _ada_kernel,
        grid=(depth, n3 // tn),
        in_specs=[pl.BlockSpec((r, d), lambda l, j: (0, 0)),
                  pl.BlockSpec((1, d, tn), lambda l, j: (l, 0, j)),
                  pl.BlockSpec((1, 1, tn), lambda l, j: (l, 0, j))],
        out_specs=pl.BlockSpec((1, r, tn), lambda l, j: (l, 0, j)),
        out_shape=jax.ShapeDtypeStruct((depth, r, n3), F32),
        compiler_params=_params(("parallel", "parallel")),
        name="ada",
    )(cc, w_ada, b_ada.reshape(depth, 1, n3))


def _inproj_kernel(x_ref, xp_ref, xn_ref, mod_ref, ng_ref, w_ref, wt_ref, qng_ref, wqm_ref, wqs_ref, kvng_ref,
                   wk_ref, wv_ref, tab_ref, gsc_ref, wc_ref,
                   qn_ref, kn_ref, vv_ref, cx_ref, sza_ref, gates_ref, qm_ref, km_ref, vm_ref, szb_ref, *, tm):
    j = pl.program_id(1)
    nj = pl.num_programs(1)

    def normed(xv):
        y = xv * jax.lax.rsqrt(jnp.mean(xv * xv, axis=-1, keepdims=True) + EPS) * ng_ref[...]
        return y * (1.0 + mod_ref[0, 1:2, :]) + mod_ref[0, 0:1, :]

    hn = normed(x_ref[0]).astype(BF16)
    h_prev = jnp.where(j > 0, normed(xp_ref[0]), 0.0).astype(BF16)
    h_next = jnp.where(j < nj - 1, normed(xn_ref[0]), 0.0).astype(BF16)
    h_ext = jnp.concatenate([h_prev, hn, h_next], axis=0)

    def tail(a, b):
        return _dot_nt(hn, wt_ref[a:b, :])

    def conv(u_ext, c0, c1, taps=A_CONV):
        pad = A_CONV // 2
        rows = u_ext.shape[0]
        acc = None
        for k in range(pad - taps // 2, pad + taps // 2 + 1):
            shifted = u_ext if k == pad else pltpu.roll(u_ext, (pad - k) % rows, 0)
            term = shifted[HALO:HALO + tm, :] * wc_ref[k:k + 1, c0:c1]
            acc = term if acc is None else acc + term
        return acc

    blk = 2 * LANE
    for i in range(3 * A_QK // blk):
        y2 = _silu(conv(_dot_nt(h_ext, w_ref[i * blk:(i + 1) * blk, :]), i * blk, (i + 1) * blk))
        for hh in range(2):
            h = 2 * i + hh
            y = y2[:, hh * LANE:(hh + 1) * LANE]
            if h < 2 * A_HEADS:
                inv = jax.lax.rsqrt(jnp.sum(y * y, axis=-1, keepdims=True) + EPS)
                y = y * (inv * (A_DK ** -0.5) if h < A_HEADS else inv)
            if h < A_HEADS:
                qn_ref[0, :, h * LANE:(h + 1) * LANE] = y.astype(BF16)
            elif h < 2 * A_HEADS:
                kn_ref[0, :, (h - A_HEADS) * LANE:(h - A_HEADS + 1) * LANE] = y.astype(BF16)
            else:
                vv_ref[0, :, (h - 2 * A_HEADS) * LANE:(h - 2 * A_HEADS + 1) * LANE] = y.astype(BF16)

    hc_ext = _dot_nt(h_ext, wt_ref[T_CG:T_ZC, :]) * _dot_nt(h_ext, wt_ref[T_H:T_BG, :])
    cx_ref[0] = (tail(T_BG, T_CG) * _silu(tail(T_ZC, T_QA)) * conv(hc_ext, 3 * A_QK, CIN_W, C_CONV)).astype(BF16)

    sza_ref[0] = _silu(_dot_nt(hn, w_ref[S_ZA:W_HEAD, :])).astype(BF16)
    szb_ref[0] = _silu(tail(T_ZB, T_H)).astype(BF16)

    tab = tab_ref[...]
    cq, sq = tab[:, 0:LANE], tab[:, LANE:2 * LANE]
    ck, sk = tab[:, 2 * LANE:3 * LANE], tab[:, 3 * LANE:4 * LANE]

    qa_blk = tail(T_QA, T_CKV)
    qa = qa_blk[:, :B_Q_LORA]
    nq =(qa * jax.lax.rsqrt(jnp.mean(qa * qa, axis=-1, keepdims=True) + EPS) * qng_ref[...]).astype(BF16)
    qmain = _dot(nq, wqm_ref[...])
    qswap = _dot(nq, wqs_ref[...])
    for h in range(B_HEADS):
        sl = slice(h * LANE, (h + 1) * LANE)
        qm_ref[0, :, sl] = (qmain[:, sl] * cq + qswap[:, sl] * sq).astype(BF16)

    ckv_blk = tail(T_CKV, W_TAIL)
    ckv = ckv_blk[:, :B_KV_LORA]
    nkv =(ckv * jax.lax.rsqrt(jnp.mean(ckv * ckv, axis=-1, keepdims=True) + EPS) * kvng_ref[...]).astype(BF16)
    kk = _dot(nkv, wk_ref[...])
    vt = _dot_nt(wv_ref[...], nkv)
    vrow = jax.lax.broadcasted_iota(jnp.int32, vt.shape, 0)
    ones_row = functools.reduce(jnp.logical_or, [vrow == h * B_VA + B_V for h in range(B_HEADS)])
    vm_ref[0] = jnp.where(ones_row, 1.0, vt).astype(BF16)
    kpm = ckv_blk[:, B_KV_LORA:]
    kpe = kpm * ck + qa_blk[:, LANE:] * sk
    for h in range(B_HEADS):
        sl = slice(h * LANE, (h + 1) * LANE)
        km_ref[0, :, sl] = (kk[:, sl] + kpe).astype(BF16)

    lane = jax.lax.broadcasted_iota(jnp.int32, kpm.shape, 1)
    z = kpm + gsc_ref[1:2, :]
    sp = jnp.maximum(z, 0.0) + jnp.log1p(jnp.exp(-jnp.abs(z)))
    g = -jnp.exp(gsc_ref[0:1, :]) * sp
    gates_ref[0] = jnp.where(lane < 2 * A_HEADS, jax.nn.sigmoid(kpm), jnp.where(lane < 4 * A_HEADS, g, 0.0))


def _layer_spec(a, layer):
    return pl.BlockSpec((None,) + a.shape[1:], lambda i, j: (layer,) + (0,) * (a.ndim - 1))


def _inproj(x, mod, per_batch, layer, ng, w, wt, qng, wqm, wqs, kvng, wk, wv, tab, gsc, wconv, tm):
    b, t, d = x.shape
    tm = min(tm, t)
    r = tm // HALO
    nh = t // HALO
    bm = (lambda i, j: (layer, i, 0, 0)) if per_batch else (lambda i, j: (layer, mod.shape[1] - 1, 0, 0))
    tok = lambda width: pl.BlockSpec((1, tm, width), lambda i, j: (i, j, 0))
    full = lambda a: pl.BlockSpec(a.shape, lambda i, j: (0,) * a.ndim)
    lay = lambda a: _layer_spec(a, layer)
    widths = (A_QK, A_QK, A_W, C_W, A_W, LANE, B_HEADS * LANE, B_HEADS * LANE, B_W, B_W)
    dtypes = (BF16, BF16, BF16, BF16, BF16, F32, BF16, BF16, BF16, BF16)
    V_OUT = 8
    return pl.pallas_call(
        functools.partial(_inproj_kernel, tm=tm),
        grid=(b, t // tm),
        in_specs=[tok(d),
                  pl.BlockSpec((1, HALO, d), lambda i, j: (i, jnp.maximum(j * r - 1, 0), 0)),
                  pl.BlockSpec((1, HALO, d), lambda i, j: (i, jnp.minimum((j + 1) * r, nh - 1), 0)),
                  pl.BlockSpec((None, 1, 3, d), bm), lay(ng),
                  pl.BlockSpec((None, W_HEAD, d), lambda i, j: (layer, 0, 0)), lay(wt), lay(qng), lay(wqm), lay(wqs),
                  lay(kvng), lay(wk), lay(wv),
                  pl.BlockSpec((tm, 4 * LANE), lambda i, j: (j, 0)), lay(gsc), lay(wconv)],
        out_specs=[pl.BlockSpec((1, B_WA, tm), lambda i, j: (i, 0, j)) if n == V_OUT else tok(wd)
                   for n, wd in enumerate(widths)],
        out_shape=[jax.ShapeDtypeStruct((b, B_WA, t) if n == V_OUT else (b, t, wd), dt)
                   for n, (wd, dt) in enumerate(zip(widths, dtypes))],
        compiler_params=_params(("parallel", "parallel")),
        name="inproj",
    )(x, x, x, mod, ng, w, wt, qng, wqm, wqs, kvng, wk, wv, tab, gsc, wconv)


INV_BASE_LOG2 = 3

def _unit_triangular_inverses(mats, c):
    row = jax.lax.broadcasted_iota(jnp.int32, (c, c), 0)
    col = jax.lax.broadcasted_iota(jnp.int32, (c, c), 1)
    same = lambda k: (row >> k) == (col >> k)
    eye = jnp.where(row == col, 1.0, 0.0)
    base = same(INV_BASE_LOG2)
    a0 = [jnp.where(base, a, 0.0) for a in mats]
    t = [eye - ai for ai in a0]
    apow = [ai.astype(BF16) for ai in a0]
    for _ in range(INV_BASE_LOG2 - 1):
        apow = [_dot(ap, ap).astype(BF16) for ap in apow]
        t = [ti + _dot(ti.astype(BF16), ap) for ti, ap in zip(t, apow)]
    for k in range(INV_BASE_LOG2, int(math.log2(c))):
        coupling = same(k + 1) & jnp.logical_not(same(k))
        e = [jnp.where(coupling, a, 0.0).astype(BF16) for a in mats]
        tb = [ti.astype(BF16) for ti in t]
        te = [_dot(tbi, ei).astype(BF16) for tbi, ei in zip(tb, e)]
        t = [ti - _dot(tei, tbi) for ti, tei, tbi in zip(t, te, tb)]
    return t


def _scan_kernel(qf_ref, kf_ref, vf_ref, gf_ref, qb_ref, kb_ref, vb_ref, gb_ref, s0_ref,
                 of_ref, ob_ref, s_ref, *, c, bb):
    n = pl.program_id(1)

    @pl.when(n == 0)
    def _():
        s_ref[...] = s0_ref[...]

    row = jax.lax.broadcasted_iota(jnp.int32, (c, c), 0)
    col = jax.lax.broadcasted_iota(jnp.int32, (c, c), 1)
    dirs = ((qf_ref, kf_ref, vf_ref, gf_ref), (qb_ref, kb_ref, vb_ref, gb_ref))
    o_refs = (of_ref, ob_ref)
    incl = (col <= row, col >= row)
    strict = (col < row, col > row)
    gates, gcb, gcb_t, gl_row = {}, {}, {}, {}
    for e in range(bb):
        for d in range(2):
            g = dirs[d][3][e]
            tri = jnp.where(incl[d], 1.0, 0.0).astype(BF16)
            hi = g.astype(BF16)
            r1 = g - hi.astype(F32)
            mid = r1.astype(BF16)
            lo = (r1 - mid.astype(F32)).astype(BF16)
            gc_all = _dot(tri, hi) + _dot(tri, mid) + _dot(tri, lo)
            gates[e, d] = g
            gcb[e, d] = gc_all
            gcb_t[e, d] = gc_all.T
            gl_row[e, d] = gc_all[c - 1:c, :] if d == 0 else gc_all[0:1, :]

    chains = [(e, d, h) for e in range(bb) for d in range(2) for h in range(A_HEADS)]
    sls = [slice(h * LANE, (h + 1) * LANE) for _, _, h in chains]
    jbs = [A_HEADS * d + h for _, d, h in chains]
    jgs = [2 * A_HEADS + jb for jb in jbs]
    beta = [gates[e, d][:, jb:jb + 1] for (e, d, _), jb in zip(chains, jbs)]
    gc = [gcb[e, d][:, jg:jg + 1] for (e, d, _), jg in zip(chains, jgs)]
    gcr = [gcb_t[e, d][jg:jg + 1, :] for (e, d, _), jg in zip(chains, jgs)]
    gl = [gl_row[e, d][:, jg:jg + 1] for (e, d, _), jg in zip(chains, jgs)]
    q = [dirs[d][0][e, :, sl] for (e, d, _), sl in zip(chains, sls)]
    k = [dirs[d][1][e, :, sl] for (e, d, _), sl in zip(chains, sls)]
    v = [dirs[d][2][e, :, sl] for (e, d, _), sl in zip(chains, sls)]
    decay = [jnp.where(incl[d], jnp.exp(jnp.where(incl[d], gci - gcri, 0.0)), 0.0)
             for (_, d, _), gci, gcri in zip(chains, gc, gcr)]
    kbeta = [ki.astype(F32) * bi for ki, bi in zip(k, beta)]
    egc = [jnp.exp(gci) for gci in gc]
    a = [jnp.where(strict[d], _dot_nt(kbi.astype(BF16), ki) * di, 0.0)
         for (_, d, _), kbi, ki, di in zip(chains, kbeta, k, decay)]
    qk = [(_dot_nt(qi, ki) * di).astype(BF16) for qi, ki, di in zip(q, k, decay)]
    tinv = [ti.astype(BF16) for ti in _unit_triangular_inverses(a, c)]
    kd_t = [(ki.astype(F32) * jnp.exp(gli - gci)).T.astype(BF16) for ki, gli, gci in zip(k, gl, gc)]
    kbe = [(kbi * ei).astype(BF16) for kbi, ei in zip(kbeta, egc)]
    vbeta = [vi.astype(F32) * bi for vi, bi in zip(v, beta)]
    qdk = [jnp.concatenate([(qi.astype(F32) * ei).astype(BF16), qki], axis=1) for qi, ei, qki in zip(q, egc, qk)]
    s = [s_ref[e, d, h] for e, d, h in chains]
    sb = [si.astype(BF16) for si in s]
    rhs = [(vbi - _dot(kbi, sbi)).astype(BF16) for vbi, kbi, sbi in zip(vbeta, kbe, sb)]
    v_new = [_dot(ti, ri).astype(BF16) for ti, ri in zip(tinv, rhs)]
    o = [_dot(qi, jnp.concatenate([sbi, vi], axis=0)) for qi, sbi, vi in zip(qdk, sb, v_new)]
    for (e, d, h), sl, oi in zip(chains, sls, o):
        o_refs[d][e, :, sl] = oi.astype(BF16)
    for (e, d, h), gli, si, kdi, vi in zip(chains, gl, s, kd_t, v_new):
        s_ref[e, d, h] = jnp.exp(gli) * si + _dot(kdi, vi)


def _scan(qn, kn, vv, gates, s0, bb):
    b, t, _ = qn.shape
    c = GDN_CHUNK
    n = t // c
    assert b % bb == 0
    fwd = lambda width: pl.BlockSpec((bb, c, width), lambda i, j: (i, j, 0))
    bwd = lambda width: pl.BlockSpec((bb, c, width), lambda i, j: (i, n - 1 - j, 0))
    st = pl.BlockSpec((bb, 2, A_HEADS, A_DK, A_DV), lambda i, j: (i, 0, 0, 0, 0))
    return pl.pallas_call(
        functools.partial(_scan_kernel, c=c, bb=bb),
        grid=(b // bb, n),
        in_specs=[fwd(A_QK), fwd(A_QK), fwd(A_W), fwd(LANE), bwd(A_QK), bwd(A_QK), bwd(A_W), bwd(LANE), st],
        out_specs=[fwd(A_W), bwd(A_W), st],
        out_shape=[jax.ShapeDtypeStruct((b, t, A_W), BF16), jax.ShapeDtypeStruct((b, t, A_W), BF16),
                   jax.ShapeDtypeStruct(s0.shape, F32)],
        compiler_params=_params(("parallel", "arbitrary")),
        name="gdn_scan",
    )(qn, kn, vv, gates, qn, kn, vv, gates, s0)


def _attention_tile(q_ref, kv_refs, szb_ref):
    n_src = len(kv_refs) // 2
    heads = range(B_HEADS)
    qh = [q_ref[0, :, h * LANE:(h + 1) * LANE] for h in heads]
    blocks = []
    for i in range(n_src):
        tk = kv_refs[2 * i].shape[1]
        kb = min(ATTN_KEY_BLOCK, tk)
        blocks += [(kv_refs[2 * i], kv_refs[2 * i + 1], slice(j * kb, (j + 1) * kb)) for j in range(tk // kb)]

    groups = [tuple(range(g, g + ATTN_HEAD_GROUP)) for g in range(0, B_HEADS, ATTN_HEAD_GROUP)]
    units = [(blk, grp) for blk in blocks for grp in groups]

    def scores(unit):
        (k_ref, _, ks), grp = unit
        return [_dot_nt(k_ref[0, ks, h * LANE:(h + 1) * LANE], qh[h]) for h in grp]

    m = [None] * B_HEADS
    acc = [None] * B_HEADS
    s_next = scores(units[0])
    for n, ((_, vt_ref, ks), grp) in enumerate(units):
        s, s_next = s_next, (scores(units[n + 1]) if n + 1 < len(units) else None)
        mb = [si.max(axis=0, keepdims=True) for si in s]
        m_new = [mbi if m[h] is None else jnp.maximum(m[h], mbi) for h, mbi in zip(grp, mb)]
        p = [jnp.exp2(si - mn).astype(BF16) for si, mn in zip(s, m_new)]
        pv = [_dot(vt_ref[0, h * B_VA:(h + 1) * B_VA, ks], pi) for h, pi in zip(grp, p)]
        for h, mn, pvi in zip(grp, m_new, pv):
            acc[h] = pvi if m[h] is None else jnp.exp2(m[h] - mn) * acc[h] + pvi
            m[h] = mn
    o = jnp.concatenate([a[:B_V] / a[B_V:B_V + 1] for a in acc], axis=0).T
    return (o * szb_ref[0].astype(F32)).astype(BF16)


def _attn_out_kernel(*refs, n_src, final):
    q_ref = refs[0]
    kv_refs = refs[1:1 + 2 * n_src]
    (szb_ref, x_ref, of_ref, ob_ref, sza_ref, cx_ref, gng_ref, mod_ref, w_ref, fg_ref, o_ref) = refs[1 + 2 * n_src:]
    bx = _attention_tile(q_ref, kv_refs, szb_ref)
    o = of_ref[0].astype(F32) + ob_ref[0].astype(F32)
    sza = sza_ref[0].astype(F32)
    acc = None
    for h in range(A_HEADS):
        sl = slice(h * LANE, (h + 1) * LANE)
        oh = o[:, sl]
        ah = oh * jax.lax.rsqrt(jnp.mean(oh * oh, axis=-1, keepdims=True) + EPS) * gng_ref[...]
        term = _dot((ah * sza[:, sl]).astype(BF16), w_ref[sl, :])
        acc = term if acc is None else acc + term
    acc = acc + _dot(bx, w_ref[A_W:A_W + B_W, :]) + _dot(cx_ref[0], w_ref[A_W + B_W:, :])
    xn = x_ref[0] + mod_ref[0, 2:3, :] * acc
    if final:
        xn = xn * jax.lax.rsqrt(jnp.mean(xn * xn, axis=-1, keepdims=True) + EPS) * fg_ref[...]
    o_ref[0] = xn


def _attn_out(x, q, kvs, szb, of, ob, sza, cx, gng, mod, per_batch, layer, w, fg, final, tm):
    b, t, d = x.shape
    tm = min(tm, t)
    bm = (lambda i, j: (layer, i, 0, 0)) if per_batch else (lambda i, j: (layer, mod.shape[1] - 1, 0, 0))
    tok = lambda width: pl.BlockSpec((1, tm, width), lambda i, j: (i, j, 0))
    full = lambda a: pl.BlockSpec(a.shape, lambda i, j: (0,) * a.ndim)
    in_specs = [tok(B_HEADS * LANE)]
    args = [q]
    for k, v in kvs:
        in_specs.append(pl.BlockSpec((1,) + k.shape[1:], lambda i, j: (i, 0, 0)))
        in_specs.append(pl.BlockSpec((1,) + v.shape[1:], lambda i, j: (i, 0, 0)))
        args += [k, v]
    in_specs += [tok(B_W), tok(d), tok(A_W), tok(A_W), tok(A_W), tok(C_W), _layer_spec(gng, layer),
                 pl.BlockSpec((None, 1, 3, d), bm), _layer_spec(w, layer), full(fg)]
    args += [szb, x, of, ob, sza, cx, gng, mod, w, fg]
    return pl.pallas_call(
        functools.partial(_attn_out_kernel, n_src=len(kvs), final=final),
        grid=(b, t // tm),
        in_specs=in_specs,
        out_specs=tok(d),
        out_shape=jax.ShapeDtypeStruct((b, t, d), F32),
        compiler_params=_params(("parallel", "parallel")),
        name="attn_out",
    )(*args)


def _deinterleave(w32):
    even, odd = w32[..., 0::2], w32[..., 1::2]
    return jnp.concatenate([even, odd], axis=-1), jnp.concatenate([odd, even], axis=-1)


def _w_layout_kernel(w_ref, pm_ref, ps_ref, head_ref, tail_ref):
    w = w_ref[0]
    head_ref[0] = w[:, :W_HEAD].astype(BF16)
    o_b, o_qa, o_kv, o_zb, o_end = 2048, 2064, 2256, 2416, 3696
    kpe = w[:, o_kv + B_KV_LORA:o_zb].astype(BF16)
    kp_main = _dot(kpe, pm_ref[...])
    kp_swap = _dot(kpe, ps_ref[...])
    z = lambda n: jnp.zeros((w.shape[0], n), F32)
    cols = [w[:, o_zb:o_end],
            w[:, o_qa:o_kv], kp_swap, z(2 * LANE - B_Q_LORA - B_ROPE),
            w[:, o_kv:o_kv + B_KV_LORA], w[:, o_b:o_qa], z(B_NOPE - 4 * A_HEADS), kp_main,
            z(LANE - B_NOPE - B_ROPE)]
    tail_ref[0] = jnp.concatenate(cols, axis=-1).astype(BF16)


def _layout_w_in(w):
    wt = jnp.swapaxes(w, 1, 2)
    z = lambda n: jnp.zeros((w.shape[0], n, w.shape[1]), BF16)
    o_b, o_qa, o_kv, o_zb, o_end = 2048, 2064, 2256, 2416, 3696
    bf = lambda a: a.astype(BF16)
    kpe = wt[:, o_kv + B_KV_LORA:o_zb]
    kp_main = jnp.concatenate([kpe[:, 0::2], kpe[:, 1::2]], axis=1)
    kp_swap = jnp.concatenate([kpe[:, 1::2], kpe[:, 0::2]], axis=1)
    rows = [bf(wt[:, o_zb:o_end]),
            bf(wt[:, o_qa:o_kv]), bf(kp_swap), z(2 * LANE - B_Q_LORA - B_ROPE),
            bf(wt[:, o_kv:o_kv + B_KV_LORA]), bf(wt[:, o_b:o_qa]), z(B_NOPE - 4 * A_HEADS), bf(kp_main),
            z(LANE - B_NOPE - B_ROPE)]
    tail = jnp.concatenate(rows, axis=1)
    assert tail.shape[1] == W_TAIL
    return bf(wt[:, :W_HEAD]), tail


def _layout_w_in_unused(w):
    depth, d, n_in = w.shape
    tk = 256
    src_main = np.concatenate([np.arange(0, B_ROPE, 2), np.arange(1, B_ROPE, 2)])
    src_swap = np.concatenate([np.arange(1, B_ROPE, 2), np.arange(0, B_ROPE, 2)])
    sel = lambda src: jnp.asarray(np.eye(B_ROPE, dtype=np.float32)[:, src], dtype=BF16)
    return pl.pallas_call(
        _w_layout_kernel,
        grid=(depth, d // tk),
        in_specs=[pl.BlockSpec((1, tk, n_in), lambda l, i: (l, i, 0)),
                  pl.BlockSpec((B_ROPE, B_ROPE), lambda l, i: (0, 0)),
                  pl.BlockSpec((B_ROPE, B_ROPE), lambda l, i: (0, 0))],
        out_specs=[pl.BlockSpec((1, tk, W_HEAD), lambda l, i: (l, i, 0)),
                   pl.BlockSpec((1, tk, W_TAIL), lambda l, i: (l, i, 0))],
        out_shape=[jax.ShapeDtypeStruct((depth, d, W_HEAD), BF16), jax.ShapeDtypeStruct((depth, d, W_TAIL), BF16)],
        compiler_params=_params(("parallel", "parallel")),
        name="w_layout",
    )(w, sel(src_main), sel(src_swap))


def _layout_w_qb(w):
    w = w.astype(BF16)
    z = lambda n: jnp.zeros(w.shape[:-1] + (n,), w.dtype)
    hd = B_NOPE + B_ROPE
    main, swap = [], []
    for h in range(B_HEADS):
        rm, rs = _deinterleave(w[..., h * hd + B_NOPE:(h + 1) * hd])
        main += [w[..., h * hd:h * hd + B_NOPE], rm, z(LANE - hd)]
        swap += [z(B_NOPE), rs, z(LANE - hd)]
    return jnp.concatenate(main, axis=-1), jnp.concatenate(swap, axis=-1)


def _layout_w_kvb(w):
    w = w.astype(BF16)
    hd = B_NOPE + B_V
    wk, wv = [], []
    for h in range(B_HEADS):
        wk += [w[..., h * hd:h * hd + B_NOPE], jnp.zeros(w.shape[:-1] + (LANE - B_NOPE,), w.dtype)]
        wv += [w[..., h * hd + B_NOPE:(h + 1) * hd], jnp.zeros(w.shape[:-1] + (B_VA - B_V,), w.dtype)]
    return jnp.concatenate(wk, axis=-1), jnp.swapaxes(jnp.concatenate(wv, axis=-1), -1, -2)


def _rope_tables(t, rotate):
    half = B_ROPE // 2
    if rotate:
        pos = np.arange(t)
        n_freq = B_ROPE // 4
        inv_freq = ROPE_THETA ** (-np.arange(n_freq, dtype=np.float64) / n_freq)
        ang = np.concatenate([(pos // GRID_W)[:, None] * inv_freq, (pos % GRID_W)[:, None] * inv_freq], axis=-1)
        cos, sin = np.cos(ang).astype(np.float32), np.sin(ang).astype(np.float32)
    else:
        cos, sin = np.ones((t, half), np.float32), np.zeros((t, half), np.float32)
    z = lambda n: np.zeros((t, n), np.float32)
    tail = LANE - B_NOPE - B_ROPE
    scale = np.float32((B_NOPE + B_ROPE) ** -0.5 * math.log2(math.e))
    cq = np.concatenate([np.ones((t, B_NOPE), np.float32), cos, cos, z(tail)], axis=1) * scale
    sq = np.concatenate([z(B_NOPE), -sin, sin, z(tail)], axis=1) * scale
    ck = np.concatenate([z(B_NOPE), cos, cos, z(tail)], axis=1)
    sk = np.concatenate([z(B_NOPE), -sin, sin, z(tail)], axis=1)
    return jnp.asarray(np.concatenate([cq, sq, ck, sk], axis=1), dtype=F32)


def kernel(x, c, ctx, c_ctx, w_ada, b_ada, norm_g, w_in, gdn_conv, gdn_a_log, gdn_dt_bias, gdn_norm_g,
           mla_q_norm_g, mla_w_qb, mla_kv_norm_g, mla_w_kvb, conv_w, w_out, final_norm_g):
    bsz, t, d = x.shape
    t_ctx = ctx.shape[1]
    depth = w_ada.shape[0]
    assert t % GDN_CHUNK == 0 and t_ctx % GDN_CHUNK == 0 and t % GRID_W == 0
    tm = 512
    bb = 2 if bsz % 2 == 0 else 1

    rows = -(-(bsz + 1) // 8) * 8
    cc = jnp.concatenate([c, jnp.zeros((rows - bsz - 1, d), F32), c_ctx[None, :]], axis=0)
    mod = _ada(cc, w_ada, b_ada).reshape(depth, rows, 3, d)

    tab_x = _rope_tables(t, True)
    tab_c = _rope_tables(t_ctx, False)
    fg = final_norm_g.reshape(1, d)
    w_all, wt_all = _layout_w_in(w_in)
    wqm_all, wqs_all = _layout_w_qb(mla_w_qb)
    wk_all, wv_all = _layout_w_kvb(mla_w_kvb)
    wo_all = w_out.astype(BF16)
    ng = norm_g.reshape(depth, 1, d)
    qng = mla_q_norm_g.reshape(depth, 1, B_Q_LORA)
    kvng = mla_kv_norm_g.reshape(depth, 1, B_KV_LORA)
    gng = gdn_norm_g.reshape(depth, 1, A_DV)
    gsc = jnp.pad(jnp.stack([gdn_a_log.reshape(depth, -1), gdn_dt_bias.reshape(depth, -1)], axis=1),
                  ((0, 0), (0, 6), (2 * A_HEADS, LANE - 4 * A_HEADS)))
    tap_pad = (A_CONV - C_CONV) // 2
    wconv = jnp.pad(jnp.concatenate([gdn_conv, jnp.pad(conv_w, ((0, 0), (tap_pad, tap_pad), (0, 0)))], axis=2),
                    ((0, 0), (0, 8 - A_CONV), (0, 0)))
    s_zero = jnp.zeros((bsz, 2, A_HEADS, A_DK, A_DV), F32)

    for l in range(depth):
        last = l == depth - 1
        shared = (l, ng, w_all, wt_all, qng, wqm_all, wqs_all, kvng, wk_all, wv_all)
        (qn_c, kn_c, vv_c, cx_c, sza_c, gates_c, qm_c, km_c, vm_c, szb_c) = _inproj(
            ctx, mod, False, *shared, tab_c, gsc, wconv, tm)
        (qn_x, kn_x, vv_x, cx_x, sza_x, gates_x, qm_x, km_x, vm_x, szb_x) = _inproj(
            x, mod, True, *shared, tab_x, gsc, wconv, tm)

        of_c, ob_c, s_ctx = _scan(qn_c, kn_c, vv_c, gates_c, s_zero, bb)
        of_x, ob_x, _ = _scan(qn_x, kn_x, vv_x, gates_x, s_ctx, bb)

        x_new = _attn_out(x, qm_x, [(km_x, vm_x), (km_c, vm_c)], szb_x, of_x, ob_x, sza_x, cx_x, gng, mod, True,
                          l, wo_all, fg, last, tm)
        if not last:
            ctx = _attn_out(ctx, qm_c, [(km_c, vm_c)], szb_c, of_c, ob_c, sza_c, cx_c, gng, mod, False,
                            l, wo_all, fg, False, tm)
        x = x_new
    return x
```

```python
import functools
import math

import jax
import jax.numpy as jnp
import numpy as np
from jax.experimental import pallas as pl
from jax.experimental.pallas import tpu as pltpu

F32 = jnp.float32
BF16 = jnp.bfloat16

EPS = 1e-6
GRID_W = 64
ROPE_THETA = 10000.0

A_HEADS = 4
A_DK = 128
A_DV = 128
A_QK = A_HEADS * A_DK
A_W = A_HEADS * A_DV
A_CONV = 5
B_HEADS = 4
B_NOPE = 64
B_ROPE = 32
B_V = 64
B_Q_LORA = 192
B_KV_LORA = 128
B_W = B_HEADS * B_V
B_VA = B_V + 16
B_WA = B_HEADS * B_VA
C_W = 256
C_CONV = 3

LANE = 128
HALO = 16
GDN_CHUNK = 128
ATTN_KEY_BLOCK = 256
ATTN_HEAD_GROUP = 4
VMEM_LIMIT = 48 * 1024 * 1024

W_HEAD = 2048
S_ZA = 1536
T_ZB = 0
T_H = 256
T_BG = 512
T_CG = 768
T_ZC = 1024
T_QA = 1280
T_CKV = 1536
W_TAIL = 1792
CIN_W = 3 * A_QK + C_W


def _dot(a, b):
    return jnp.dot(a, b, preferred_element_type=F32)


def _dot_nt(a, b):
    return jax.lax.dot_general(a, b, (((1,), (1,)), ((), ())), preferred_element_type=F32)


def _silu(x):
    return x * jax.nn.sigmoid(x)


def _params(sem):
    return pltpu.CompilerParams(dimension_semantics=sem, vmem_limit_bytes=VMEM_LIMIT)


def _ada_kernel(c_ref, w_ref, b_ref, o_ref):
    s = _silu(c_ref[...]).astype(BF16)
    o_ref[0] = _dot(s, w_ref[0].astype(BF16)) + b_ref[0]


def _ada(cc, w_ada, b_ada):
    depth, d, n3 = w_ada.shape
    r = cc.shape[0]
    tn = 768
    return pl.pallas_call(
        _ada_kernel,
        grid=(depth, n3 // tn),
        in_specs=[pl.BlockSpec((r, d), lambda l, j: (0, 0)),
                  pl.BlockSpec((1, d, tn), lambda l, j: (l, 0, j)),
                  pl.BlockSpec((1, 1, tn), lambda l, j: (l, 0, j))],
        out_specs=pl.BlockSpec((1, r, tn), lambda l, j: (l, 0, j)),
        out_shape=jax.ShapeDtypeStruct((depth, r, n3), F32),
        compiler_params=_params(("parallel", "parallel")),
        name="ada",
    )(cc, w_ada, b_ada.reshape(depth, 1, n3))


def _inproj_kernel(x_ref, xp_ref, xn_ref, mod_ref, ng_ref, w_ref, wt_ref, qng_ref, wqm_ref, wqs_ref, kvng_ref,
                   wk_ref, wv_ref, tab_ref, gsc_ref, wc_ref,
                   qn_ref, kn_ref, vv_ref, cx_ref, sza_ref, gates_ref, qm_ref, km_ref, vm_ref, szb_ref, *, tm):
    j = pl.program_id(1)
    nj = pl.num_programs(1)

    def normed(xv):
        y = xv * jax.lax.rsqrt(jnp.mean(xv * xv, axis=-1, keepdims=True) + EPS) * ng_ref[...]
        return y * (1.0 + mod_ref[0, 1:2, :]) + mod_ref[0, 0:1, :]

    hn = normed(x_ref[0]).astype(BF16)
    h_prev = jnp.where(j > 0, normed(xp_ref[0]), 0.0).astype(BF16)
    h_next = jnp.where(j < nj - 1, normed(xn_ref[0]), 0.0).astype(BF16)
    h_ext = jnp.concatenate([h_prev, hn, h_next], axis=0)

    def tail(a, b):
        return _dot(hn, wt_ref[:, a:b])

    def conv(u_ext, c0, c1, taps=A_CONV):
        pad = A_CONV // 2
        rows = u_ext.shape[0]
        acc = None
        for k in range(pad - taps // 2, pad + taps // 2 + 1):
            shifted = u_ext if k == pad else pltpu.roll(u_ext, (pad - k) % rows, 0)
            term = shifted[HALO:HALO + tm, :] * wc_ref[k:k + 1, c0:c1]
            acc = term if acc is None else acc + term
        return acc

    blk = 2 * LANE
    for i in range(3 * A_QK // blk):
        y2 = _silu(conv(_dot(h_ext, w_ref[:, i * blk:(i + 1) * blk]), i * blk, (i + 1) * blk))
        for hh in range(2):
            h = 2 * i + hh
            y = y2[:, hh * LANE:(hh + 1) * LANE]
            if h < 2 * A_HEADS:
                inv = jax.lax.rsqrt(jnp.sum(y * y, axis=-1, keepdims=True) + EPS)
                y = y * (inv * (A_DK ** -0.5) if h < A_HEADS else inv)
            if h < A_HEADS:
                qn_ref[0, :, h * LANE:(h + 1) * LANE] = y.astype(BF16)
            elif h < 2 * A_HEADS:
                kn_ref[0, :, (h - A_HEADS) * LANE:(h - A_HEADS + 1) * LANE] = y.astype(BF16)
            else:
                vv_ref[0, :, (h - 2 * A_HEADS) * LANE:(h - 2 * A_HEADS + 1) * LANE] = y.astype(BF16)

    hc_ext = _dot(h_ext, wt_ref[:, T_CG:T_ZC]) * _dot(h_ext, wt_ref[:, T_H:T_BG])
    cx_ref[0] = (tail(T_BG, T_CG) * _silu(tail(T_ZC, T_QA)) * conv(hc_ext, 3 * A_QK, CIN_W, C_CONV)).astype(BF16)

    sza_ref[0] = _silu(_dot(hn, w_ref[:, S_ZA:W_HEAD])).astype(BF16)
    szb_ref[0] = _silu(tail(T_ZB, T_H)).astype(BF16)

    tab = tab_ref[...]
    cq, sq = tab[:, 0:LANE], tab[:, LANE:2 * LANE]
    ck, sk = tab[:, 2 * LANE:3 * LANE], tab[:, 3 * LANE:4 * LANE]

    qa_blk = tail(T_QA, T_CKV)
    qa = qa_blk[:, :B_Q_LORA]
    nq =(qa * jax.lax.rsqrt(jnp.mean(qa * qa, axis=-1, keepdims=True) + EPS) * qng_ref[...]).astype(BF16)
    qmain = _dot(nq, wqm_ref[...])
    qswap = _dot(nq, wqs_ref[...])
    for h in range(B_HEADS):
        sl = slice(h * LANE, (h + 1) * LANE)
        qm_ref[0, :, sl] = (qmain[:, sl] * cq + qswap[:, sl] * sq).astype(BF16)

    ckv_blk = tail(T_CKV, W_TAIL)
    ckv = ckv_blk[:, :B_KV_LORA]
    nkv =(ckv * jax.lax.rsqrt(jnp.mean(ckv * ckv, axis=-1, keepdims=True) + EPS) * kvng_ref[...]).astype(BF16)
    kk = _dot(nkv, wk_ref[...])
    vt = _dot_nt(wv_ref[...], nkv)
    vrow = jax.lax.broadcasted_iota(jnp.int32, vt.shape, 0)
    ones_row = functools.reduce(jnp.logical_or, [vrow == h * B_VA + B_V for h in range(B_HEADS)])
    vm_ref[0] = jnp.where(ones_row, 1.0, vt).astype(BF16)
    kpm = ckv_blk[:, B_KV_LORA:]
    kpe = kpm * ck + qa_blk[:, LANE:] * sk
    for h in range(B_HEADS):
        sl = slice(h * LANE, (h + 1) * LANE)
        km_ref[0, :, sl] = (kk[:, sl] + kpe).astype(BF16)

    lane = jax.lax.broadcasted_iota(jnp.int32, kpm.shape, 1)
    z = kpm + gsc_ref[1:2, :]
    sp = jnp.maximum(z, 0.0) + jnp.log1p(jnp.exp(-jnp.abs(z)))
    g = -jnp.exp(gsc_ref[0:1, :]) * sp
    gates_ref[0] = jnp.where(lane < 2 * A_HEADS, jax.nn.sigmoid(kpm), jnp.where(lane < 4 * A_HEADS, g, 0.0))


def _layer_spec(a, layer):
    return pl.BlockSpec((None,) + a.shape[1:], lambda i, j: (layer,) + (0,) * (a.ndim - 1))


def _inproj(x, mod, per_batch, layer, ng, w, wt, qng, wqm, wqs, kvng, wk, wv, tab, gsc, wconv, tm):
    b, t, d = x.shape
    tm = min(tm, t)
    r = tm // HALO
    nh = t // HALO
    bm = (lambda i, j: (layer, i, 0, 0)) if per_batch else (lambda i, j: (layer, mod.shape[1] - 1, 0, 0))
    tok = lambda width: pl.BlockSpec((1, tm, width), lambda i, j: (i, j, 0))
    full = lambda a: pl.BlockSpec(a.shape, lambda i, j: (0,) * a.ndim)
    lay = lambda a: _layer_spec(a, layer)
    widths = (A_QK, A_QK, A_W, C_W, A_W, LANE, B_HEADS * LANE, B_HEADS * LANE, B_W, B_W)
    dtypes = (BF16, BF16, BF16, BF16, BF16, F32, BF16, BF16, BF16, BF16)
    V_OUT = 8
    return pl.pallas_call(
        functools.partial(_inproj_kernel, tm=tm),
        grid=(b, t // tm),
        in_specs=[tok(d),
                  pl.BlockSpec((1, HALO, d), lambda i, j: (i, jnp.maximum(j * r - 1, 0), 0)),
                  pl.BlockSpec((1, HALO, d), lambda i, j: (i, jnp.minimum((j + 1) * r, nh - 1), 0)),
                  pl.BlockSpec((None, 1, 3, d), bm), lay(ng),
                  pl.BlockSpec((None, d, W_HEAD), lambda i, j: (layer, 0, 0)), lay(wt), lay(qng), lay(wqm), lay(wqs),
                  lay(kvng), lay(wk), lay(wv),
                  pl.BlockSpec((tm, 4 * LANE), lambda i, j: (j, 0)), lay(gsc), lay(wconv)],
        out_specs=[pl.BlockSpec((1, B_WA, tm), lambda i, j: (i, 0, j)) if n == V_OUT else tok(wd)
                   for n, wd in enumerate(widths)],
        out_shape=[jax.ShapeDtypeStruct((b, B_WA, t) if n == V_OUT else (b, t, wd), dt)
                   for n, (wd, dt) in enumerate(zip(widths, dtypes))],
        compiler_params=_params(("parallel", "parallel")),
        name="inproj",
    )(x, x, x, mod, ng, w, wt, qng, wqm, wqs, kvng, wk, wv, tab, gsc, wconv)


INV_BASE_LOG2 = 3

def _unit_triangular_inverses(mats, c):
    row = jax.lax.broadcasted_iota(jnp.int32, (c, c), 0)
    col = jax.lax.broadcasted_iota(jnp.int32, (c, c), 1)
    same = lambda k: (row >> k) == (col >> k)
    eye = jnp.where(row == col, 1.0, 0.0)
    base = same(INV_BASE_LOG2)
    a0 = [jnp.where(base, a, 0.0) for a in mats]
    t = [eye - ai for ai in a0]
    apow = [ai.astype(BF16) for ai in a0]
    for _ in range(INV_BASE_LOG2 - 1):
        apow = [_dot(ap, ap).astype(BF16) for ap in apow]
        t = [ti + _dot(ti.astype(BF16), ap) for ti, ap in zip(t, apow)]
    for k in range(INV_BASE_LOG2, int(math.log2(c))):
        coupling = same(k + 1) & jnp.logical_not(same(k))
        e = [jnp.where(coupling, a, 0.0).astype(BF16) for a in mats]
        tb = [ti.astype(BF16) for ti in t]
        te = [_dot(tbi, ei).astype(BF16) for tbi, ei in zip(tb, e)]
        t = [ti - _dot(tei, tbi) for ti, tei, tbi in zip(t, te, tb)]
    return t


def _scan_kernel(qf_ref, kf_ref, vf_ref, gf_ref, qb_ref, kb_ref, vb_ref, gb_ref, s0_ref,
                 of_ref, ob_ref, s_ref, *, c, bb):
    n = pl.program_id(1)

    @pl.when(n == 0)
    def _():
        s_ref[...] = s0_ref[...]

    row = jax.lax.broadcasted_iota(jnp.int32, (c, c), 0)
    col = jax.lax.broadcasted_iota(jnp.int32, (c, c), 1)
    dirs = ((qf_ref, kf_ref, vf_ref, gf_ref), (qb_ref, kb_ref, vb_ref, gb_ref))
    o_refs = (of_ref, ob_ref)
    incl = (col <= row, col >= row)
    strict = (col < row, col > row)
    gates, gcb, gcb_t, gl_row = {}, {}, {}, {}
    for e in range(bb):
        for d in range(2):
            g = dirs[d][3][e]
            tri = jnp.where(incl[d], 1.0, 0.0).astype(BF16)
            hi = g.astype(BF16)
            r1 = g - hi.astype(F32)
            mid = r1.astype(BF16)
            lo = (r1 - mid.astype(F32)).astype(BF16)
            gc_all = _dot(tri, hi) + _dot(tri, mid) + _dot(tri, lo)
            gates[e, d] = g
            gcb[e, d] = gc_all
            gcb_t[e, d] = gc_all.T
            gl_row[e, d] = gc_all[c - 1:c, :] if d == 0 else gc_all[0:1, :]

    chains = [(e, d, h) for e in range(bb) for d in range(2) for h in range(A_HEADS)]
    sls = [slice(h * LANE, (h + 1) * LANE) for _, _, h in chains]
    jbs = [A_HEADS * d + h for _, d, h in chains]
    jgs = [2 * A_HEADS + jb for jb in jbs]
    beta = [gates[e, d][:, jb:jb + 1] for (e, d, _), jb in zip(chains, jbs)]
    gc = [gcb[e, d][:, jg:jg + 1] for (e, d, _), jg in zip(chains, jgs)]
    gcr = [gcb_t[e, d][jg:jg + 1, :] for (e, d, _), jg in zip(chains, jgs)]
    gl = [gl_row[e, d][:, jg:jg + 1] for (e, d, _), jg in zip(chains, jgs)]
    q = [dirs[d][0][e, :, sl] for (e, d, _), sl in zip(chains, sls)]
    k = [dirs[d][1][e, :, sl] for (e, d, _), sl in zip(chains, sls)]
    v = [dirs[d][2][e, :, sl] for (e, d, _), sl in zip(chains, sls)]
    decay = [jnp.where(incl[d], jnp.exp(jnp.where(incl[d], gci - gcri, 0.0)), 0.0)
             for (_, d, _), gci, gcri in zip(chains, gc, gcr)]
    kbeta = [ki.astype(F32) * bi for ki, bi in zip(k, beta)]
    egc = [jnp.exp(gci) for gci in gc]
    a = [jnp.where(strict[d], _dot_nt(kbi.astype(BF16), ki) * di, 0.0)
         for (_, d, _), kbi, ki, di in zip(chains, kbeta, k, decay)]
    qk = [(_dot_nt(qi, ki) * di).astype(BF16) for qi, ki, di in zip(q, k, decay)]
    tinv = [ti.astype(BF16) for ti in _unit_triangular_inverses(a, c)]
    kd_t = [(ki.astype(F32) * jnp.exp(gli - gci)).T.astype(BF16) for ki, gli, gci in zip(k, gl, gc)]
    kbe = [(kbi * ei).astype(BF16) for kbi, ei in zip(kbeta, egc)]
    vbeta = [vi.astype(F32) * bi for vi, bi in zip(v, beta)]
    qdk = [jnp.concatenate([(qi.astype(F32) * ei).astype(BF16), qki], axis=1) for qi, ei, qki in zip(q, egc, qk)]
    s = [s_ref[e, d, h] for e, d, h in chains]
    sb = [si.astype(BF16) for si in s]
    rhs = [(vbi - _dot(kbi, sbi)).astype(BF16) for vbi, kbi, sbi in zip(vbeta, kbe, sb)]
    v_new = [_dot(ti, ri).astype(BF16) for ti, ri in zip(tinv, rhs)]
    o = [_dot(qi, jnp.concatenate([sbi, vi], axis=0)) for qi, sbi, vi in zip(qdk, sb, v_new)]
    for (e, d, h), sl, oi in zip(chains, sls, o):
        o_refs[d][e, :, sl] = oi.astype(BF16)
    for (e, d, h), gli, si, kdi, vi in zip(chains, gl, s, kd_t, v_new):
        s_ref[e, d, h] = jnp.exp(gli) * si + _dot(kdi, vi)


def _scan(qn, kn, vv, gates, s0, bb):
    b, t, _ = qn.shape
    c = GDN_CHUNK
    n = t // c
    assert b % bb == 0
    fwd = lambda width: pl.BlockSpec((bb, c, width), lambda i, j: (i, j, 0))
    bwd = lambda width: pl.BlockSpec((bb, c, width), lambda i, j: (i, n - 1 - j, 0))
    st = pl.BlockSpec((bb, 2, A_HEADS, A_DK, A_DV), lambda i, j: (i, 0, 0, 0, 0))
    return pl.pallas_call(
        functools.partial(_scan_kernel, c=c, bb=bb),
        grid=(b // bb, n),
        in_specs=[fwd(A_QK), fwd(A_QK), fwd(A_W), fwd(LANE), bwd(A_QK), bwd(A_QK), bwd(A_W), bwd(LANE), st],
        out_specs=[fwd(A_W), bwd(A_W), st],
        out_shape=[jax.ShapeDtypeStruct((b, t, A_W), BF16), jax.ShapeDtypeStruct((b, t, A_W), BF16),
                   jax.ShapeDtypeStruct(s0.shape, F32)],
        compiler_params=_params(("parallel", "arbitrary")),
        name="gdn_scan",
    )(qn, kn, vv, gates, qn, kn, vv, gates, s0)


def _attention_tile(q_ref, kv_refs, szb_ref):
    n_src = len(kv_refs) // 2
    heads = range(B_HEADS)
    qh = [q_ref[0, :, h * LANE:(h + 1) * LANE] for h in heads]
    blocks = []
    for i in range(n_src):
        tk = kv_refs[2 * i].shape[1]
        kb = min(ATTN_KEY_BLOCK, tk)
        blocks += [(kv_refs[2 * i], kv_refs[2 * i + 1], slice(j * kb, (j + 1) * kb)) for j in range(tk // kb)]

    groups = [tuple(range(g, g + ATTN_HEAD_GROUP)) for g in range(0, B_HEADS, ATTN_HEAD_GROUP)]
    units = [(blk, grp) for blk in blocks for grp in groups]

    def scores(unit):
        (k_ref, _, ks), grp = unit
        return [_dot_nt(k_ref[0, ks, h * LANE:(h + 1) * LANE], qh[h]) for h in grp]

    m = [None] * B_HEADS
    acc = [None] * B_HEADS
    s_next = scores(units[0])
    for n, ((_, vt_ref, ks), grp) in enumerate(units):
        s, s_next = s_next, (scores(units[n + 1]) if n + 1 < len(units) else None)
        mb = [si.max(axis=0, keepdims=True) for si in s]
        m_new = [mbi if m[h] is None else jnp.maximum(m[h], mbi) for h, mbi in zip(grp, mb)]
        p = [jnp.exp2(si - mn).astype(BF16) for si, mn in zip(s, m_new)]
        pv = [_dot(vt_ref[0, h * B_VA:(h + 1) * B_VA, ks], pi) for h, pi in zip(grp, p)]
        for h, mn, pvi in zip(grp, m_new, pv):
            acc[h] = pvi if m[h] is None else jnp.exp2(m[h] - mn) * acc[h] + pvi
            m[h] = mn
    o = jnp.concatenate([a[:B_V] / a[B_V:B_V + 1] for a in acc], axis=0).T
    return (o * szb_ref[0].astype(F32)).astype(BF16)


def _attn_out_kernel(*refs, n_src, final):
    q_ref = refs[0]
    kv_refs = refs[1:1 + 2 * n_src]
    (szb_ref, x_ref, of_ref, ob_ref, sza_ref, cx_ref, gng_ref, mod_ref, w_ref, fg_ref, o_ref) = refs[1 + 2 * n_src:]
    bx = _attention_tile(q_ref, kv_refs, szb_ref)
    o = of_ref[0].astype(F32) + ob_ref[0].astype(F32)
    sza = sza_ref[0].astype(F32)
    acc = None
    for h in range(A_HEADS):
        sl = slice(h * LANE, (h + 1) * LANE)
        oh = o[:, sl]
        ah = oh * jax.lax.rsqrt(jnp.mean(oh * oh, axis=-1, keepdims=True) + EPS) * gng_ref[...]
        term = _dot((ah * sza[:, sl]).astype(BF16), w_ref[sl, :])
        acc = term if acc is None else acc + term
    acc = acc + _dot(bx, w_ref[A_W:A_W + B_W, :]) + _dot(cx_ref[0], w_ref[A_W + B_W:, :])
    xn = x_ref[0] + mod_ref[0, 2:3, :] * acc
    if final:
        xn = xn * jax.lax.rsqrt(jnp.mean(xn * xn, axis=-1, keepdims=True) + EPS) * fg_ref[...]
    o_ref[0] = xn


def _attn_out(x, q, kvs, szb, of, ob, sza, cx, gng, mod, per_batch, layer, w, fg, final, tm):
    b, t, d = x.shape
    tm = min(tm, t)
    bm = (lambda i, j: (layer, i, 0, 0)) if per_batch else (lambda i, j: (layer, mod.shape[1] - 1, 0, 0))
    tok = lambda width: pl.BlockSpec((1, tm, width), lambda i, j: (i, j, 0))
    full = lambda a: pl.BlockSpec(a.shape, lambda i, j: (0,) * a.ndim)
    in_specs = [tok(B_HEADS * LANE)]
    args = [q]
    for k, v in kvs:
        in_specs.append(pl.BlockSpec((1,) + k.shape[1:], lambda i, j: (i, 0, 0)))
        in_specs.append(pl.BlockSpec((1,) + v.shape[1:], lambda i, j: (i, 0, 0)))
        args += [k, v]
    in_specs += [tok(B_W), tok(d), tok(A_W), tok(A_W), tok(A_W), tok(C_W), _layer_spec(gng, layer),
                 pl.BlockSpec((None, 1, 3, d), bm), _layer_spec(w, layer), full(fg)]
    args += [szb, x, of, ob, sza, cx, gng, mod, w, fg]
    return pl.pallas_call(
        functools.partial(_attn_out_kernel, n_src=len(kvs), final=final),
        grid=(b, t // tm),
        in_specs=in_specs,
        out_specs=tok(d),
        out_shape=jax.ShapeDtypeStruct((b, t, d), F32),
        compiler_params=_params(("parallel", "parallel")),
        name="attn_out",
    )(*args)


def _deinterleave(w32):
    even, odd = w32[..., 0::2], w32[..., 1::2]
    return jnp.concatenate([even, odd], axis=-1), jnp.concatenate([odd, even], axis=-1)


def _w_layout_kernel(w_ref, pm_ref, ps_ref, head_ref, tail_ref):
    w = w_ref[0]
    head_ref[0] = w[:, :W_HEAD].astype(BF16)
    o_b, o_qa, o_kv, o_zb, o_end = 2048, 2064, 2256, 2416, 3696
    kpe = w[:, o_kv + B_KV_LORA:o_zb].astype(BF16)
    kp_main = _dot(kpe, pm_ref[...])
    kp_swap = _dot(kpe, ps_ref[...])
    z = lambda n: jnp.zeros((w.shape[0], n), F32)
    cols = [w[:, o_zb:o_end],
            w[:, o_qa:o_kv], kp_swap, z(2 * LANE - B_Q_LORA - B_ROPE),
            w[:, o_kv:o_kv + B_KV_LORA], w[:, o_b:o_qa], z(B_NOPE - 4 * A_HEADS), kp_main,
            z(LANE - B_NOPE - B_ROPE)]
    tail_ref[0] = jnp.concatenate(cols, axis=-1).astype(BF16)


def _layout_w_in(w):
    depth, d, n_in = w.shape
    tk = 256
    src_main = np.concatenate([np.arange(0, B_ROPE, 2), np.arange(1, B_ROPE, 2)])
    src_swap = np.concatenate([np.arange(1, B_ROPE, 2), np.arange(0, B_ROPE, 2)])
    sel = lambda src: jnp.asarray(np.eye(B_ROPE, dtype=np.float32)[:, src], dtype=BF16)
    return pl.pallas_call(
        _w_layout_kernel,
        grid=(depth, d // tk),
        in_specs=[pl.BlockSpec((1, tk, n_in), lambda l, i: (l, i, 0)),
                  pl.BlockSpec((B_ROPE, B_ROPE), lambda l, i: (0, 0)),
                  pl.BlockSpec((B_ROPE, B_ROPE), lambda l, i: (0, 0))],
        out_specs=[pl.BlockSpec((1, tk, W_HEAD), lambda l, i: (l, i, 0)),
                   pl.BlockSpec((1, tk, W_TAIL), lambda l, i: (l, i, 0))],
        out_shape=[jax.ShapeDtypeStruct((depth, d, W_HEAD), BF16), jax.ShapeDtypeStruct((depth, d, W_TAIL), BF16)],
        compiler_params=_params(("parallel", "parallel")),
        name="w_layout",
    )(w, sel(src_main), sel(src_swap))


def _layout_w_qb(w):
    w = w.astype(BF16)
    z = lambda n: jnp.zeros(w.shape[:-1] + (n,), w.dtype)
    hd = B_NOPE + B_ROPE
    main, swap = [], []
    for h in range(B_HEADS):
        rm, rs = _deinterleave(w[..., h * hd + B_NOPE:(h + 1) * hd])
        main += [w[..., h * hd:h * hd + B_NOPE], rm, z(LANE - hd)]
        swap += [z(B_NOPE), rs, z(LANE - hd)]
    return jnp.concatenate(main, axis=-1), jnp.concatenate(swap, axis=-1)


def _layout_w_kvb(w):
    w = w.astype(BF16)
    hd = B_NOPE + B_V
    wk, wv = [], []
    for h in range(B_HEADS):
        wk += [w[..., h * hd:h * hd + B_NOPE], jnp.zeros(w.shape[:-1] + (LANE - B_NOPE,), w.dtype)]
        wv += [w[..., h * hd + B_NOPE:(h + 1) * hd], jnp.zeros(w.shape[:-1] + (B_VA - B_V,), w.dtype)]
    return jnp.concatenate(wk, axis=-1), jnp.swapaxes(jnp.concatenate(wv, axis=-1), -1, -2)


def _rope_tables(t, rotate):
    half = B_ROPE // 2
    if rotate:
        pos = np.arange(t)
        n_freq = B_ROPE // 4
        inv_freq = ROPE_THETA ** (-np.arange(n_freq, dtype=np.float64) / n_freq)
        ang = np.concatenate([(pos // GRID_W)[:, None] * inv_freq, (pos % GRID_W)[:, None] * inv_freq], axis=-1)
        cos, sin = np.cos(ang).astype(np.float32), np.sin(ang).astype(np.float32)
    else:
        cos, sin = np.ones((t, half), np.float32), np.zeros((t, half), np.float32)
    z = lambda n: np.zeros((t, n), np.float32)
    tail = LANE - B_NOPE - B_ROPE
    scale = np.float32((B_NOPE + B_ROPE) ** -0.5 * math.log2(math.e))
    cq = np.concatenate([np.ones((t, B_NOPE), np.float32), cos, cos, z(tail)], axis=1) * scale
    sq = np.concatenate([z(B_NOPE), -sin, sin, z(tail)], axis=1) * scale
    ck = np.concatenate([z(B_NOPE), cos, cos, z(tail)], axis=1)
    sk = np.concatenate([z(B_NOPE), -sin, sin, z(tail)], axis=1)
    return jnp.asarray(np.concatenate([cq, sq, ck, sk], axis=1), dtype=F32)


def kernel(x, c, ctx, c_ctx, w_ada, b_ada, norm_g, w_in, gdn_conv, gdn_a_log, gdn_dt_bias, gdn_norm_g,
           mla_q_norm_g, mla_w_qb, mla_kv_norm_g, mla_w_kvb, conv_w, w_out, final_norm_g):
    bsz, t, d = x.shape
    t_ctx = ctx.shape[1]
    depth = w_ada.shape[0]
    assert t % GDN_CHUNK == 0 and t_ctx % GDN_CHUNK == 0 and t % GRID_W == 0
    tm = 512
    bb = 2 if bsz % 2 == 0 else 1

    rows = -(-(bsz + 1) // 8) * 8
    cc = jnp.concatenate([c, jnp.zeros((rows - bsz - 1, d), F32), c_ctx[None, :]], axis=0)
    mod = _ada(cc, w_ada, b_ada).reshape(depth, rows, 3, d)

    tab_x = _rope_tables(t, True)
    tab_c = _rope_tables(t_ctx, False)
    fg = final_norm_g.reshape(1, d)
    w_all, wt_all = _layout_w_in(w_in)
    wqm_all, wqs_all = _layout_w_qb(mla_w_qb)
    wk_all, wv_all = _layout_w_kvb(mla_w_kvb)
    wo_all = w_out.astype(BF16)
    ng = norm_g.reshape(depth, 1, d)
    qng = mla_q_norm_g.reshape(depth, 1, B_Q_LORA)
    kvng = mla_kv_norm_g.reshape(depth, 1, B_KV_LORA)
    gng = gdn_norm_g.reshape(depth, 1, A_DV)
    gsc = jnp.pad(jnp.stack([gdn_a_log.reshape(depth, -1), gdn_dt_bias.reshape(depth, -1)], axis=1),
                  ((0, 0), (0, 6), (2 * A_HEADS, LANE - 4 * A_HEADS)))
    tap_pad = (A_CONV - C_CONV) // 2
    wconv = jnp.pad(jnp.concatenate([gdn_conv, jnp.pad(conv_w, ((0, 0), (tap_pad, tap_pad), (0, 0)))], axis=2),
                    ((0, 0), (0, 8 - A_CONV), (0, 0)))
    s_zero = jnp.zeros((bsz, 2, A_HEADS, A_DK, A_DV), F32)

    for l in range(depth):
        last = l == depth - 1
        shared = (l, ng, w_all, wt_all, qng, wqm_all, wqs_all, kvng, wk_all, wv_all)
        (qn_c, kn_c, vv_c, cx_c, sza_c, gates_c, qm_c, km_c, vm_c, szb_c) = _inproj(
            ctx, mod, False, *shared, tab_c, gsc, wconv, tm)
        (qn_x, kn_x, vv_x, cx_x, sza_x, gates_x, qm_x, km_x, vm_x, szb_x) = _inproj(
            x, mod, True, *shared, tab_x, gsc, wconv, tm)

        of_c, ob_c, s_ctx = _scan(qn_c, kn_c, vv_c, gates_c, s_zero, bb)
        of_x, ob_x, _ = _scan(qn_x, kn_x, vv_x, gates_x, s_ctx, bb)

        x_new = _attn_out(x, qm_x, [(km_x, vm_x), (km_c, vm_c)], szb_x, of_x, ob_x, sza_x, cx_x, gng, mod, True,
                          l, wo_all, fg, last, tm)
        if not last:
            ctx = _attn_out(ctx, qm_c, [(km_c, vm_c)], szb_c, of_c, ob_c, sza_c, cx_c, gng, mod, False,
                            l, wo_all, fg, False, tm)
        x = x_new
    return x
```

```python
import functools
import math

import jax
import jax.numpy as jnp
import numpy as np
from jax.experimental import pallas as pl
from jax.experimental.pallas import tpu as pltpu

F32 = jnp.float32
BF16 = jnp.bfloat16

EPS = 1e-6
GRID_W = 64
ROPE_THETA = 10000.0

A_HEADS = 4
A_DK = 128
A_DV = 128
A_QK = A_HEADS * A_DK
A_W = A_HEADS * A_DV
A_CONV = 5
B_HEADS = 4
B_NOPE = 64
B_ROPE = 32
B_V = 64
B_Q_LORA = 192
B_KV_LORA = 128
B_W = B_HEADS * B_V
B_VA = B_V + 16
B_WA = B_HEADS * B_VA
C_W = 256
C_CONV = 3

LANE = 128
HALO = 16
GDN_CHUNK = 128
ATTN_KEY_BLOCK = 256
ATTN_HEAD_GROUP = 4
VMEM_LIMIT = 48 * 1024 * 1024

W_HEAD = 2048
S_ZA = 1536
T_ZB = 0
T_H = 256
T_BG = 512
T_CG = 768
T_ZC = 1024
T_QA = 1280
T_CKV = 1536
W_TAIL = 1792
CIN_W = 3 * A_QK + C_W


def _dot(a, b):
    return jnp.dot(a, b, preferred_element_type=F32)


def _dot_nt(a, b):
    return jax.lax.dot_general(a, b, (((1,), (1,)), ((), ())), preferred_element_type=F32)


def _silu(x):
    h = 0.5 * x
    return h + h * jnp.tanh(h)


def _params(sem):
    return pltpu.CompilerParams(dimension_semantics=sem, vmem_limit_bytes=VMEM_LIMIT)


def _ada_kernel(c_ref, w_ref, b_ref, o_ref):
    s = _silu(c_ref[...]).astype(BF16)
    o_ref[0] = _dot(s, w_ref[0].astype(BF16)) + b_ref[0]


def _ada(cc, w_ada, b_ada):
    depth, d, n3 = w_ada.shape
    r = cc.shape[0]
    tn = 768
    return pl.pallas_call(
        _ada_kernel,
        grid=(depth, n3 // tn),
        in_specs=[pl.BlockSpec((r, d), lambda l, j: (0, 0)),
                  pl.BlockSpec((1, d, tn), lambda l, j: (l, 0, j)),
                  pl.BlockSpec((1, 1, tn), lambda l, j: (l, 0, j))],
        out_specs=pl.BlockSpec((1, r, tn), lambda l, j: (l, 0, j)),
        out_shape=jax.ShapeDtypeStruct((depth, r, n3), F32),
        compiler_params=_params(("parallel", "parallel")),
        name="ada",
    )(cc, w_ada, b_ada.reshape(depth, 1, n3))


def _inproj_kernel(x_ref, xp_ref, xn_ref, mod_ref, ng_ref, w_ref, wt_ref, qng_ref, wqm_ref, wqs_ref, kvng_ref,
                   wk_ref, wv_ref, tab_ref, gsc_ref, wc_ref,
                   qn_ref, kn_ref, vv_ref, cx_ref, sza_ref, gates_ref, qm_ref, km_ref, vm_ref, szb_ref, *, tm):
    j = pl.program_id(1)
    nj = pl.num_programs(1)

    def normed(xv):
        y = xv * jax.lax.rsqrt(jnp.mean(xv * xv, axis=-1, keepdims=True) + EPS) * ng_ref[...]
        return y * (1.0 + mod_ref[0, 1:2, :]) + mod_ref[0, 0:1, :]

    hn = normed(x_ref[0]).astype(BF16)
    h_prev = jnp.where(j > 0, normed(xp_ref[0]), 0.0).astype(BF16)
    h_next = jnp.where(j < nj - 1, normed(xn_ref[0]), 0.0).astype(BF16)
    h_ext = jnp.concatenate([h_prev, hn, h_next], axis=0)

    def tail(a, b):
        return _dot(hn, wt_ref[:, a:b])

    def conv(u_ext, c0, c1, taps=A_CONV):
        pad = A_CONV // 2
        rows = u_ext.shape[0]
        acc = None
        for k in range(pad - taps // 2, pad + taps // 2 + 1):
            shifted = u_ext if k == pad else pltpu.roll(u_ext, (pad - k) % rows, 0)
            term = shifted[HALO:HALO + tm, :] * wc_ref[k:k + 1, c0:c1]
            acc = term if acc is None else acc + term
        return acc

    blk = 2 * LANE
    for i in range(3 * A_QK // blk):
        y2 = _silu(conv(_dot(h_ext, w_ref[:, i * blk:(i + 1) * blk]), i * blk, (i + 1) * blk))
        for hh in range(2):
            h = 2 * i + hh
            y = y2[:, hh * LANE:(hh + 1) * LANE]
            if h < 2 * A_HEADS:
                inv = jax.lax.rsqrt(jnp.sum(y * y, axis=-1, keepdims=True) + EPS)
                y = y * (inv * (A_DK ** -0.5) if h < A_HEADS else inv)
            if h < A_HEADS:
                qn_ref[0, :, h * LANE:(h + 1) * LANE] = y.astype(BF16)
            elif h < 2 * A_HEADS:
                kn_ref[0, :, (h - A_HEADS) * LANE:(h - A_HEADS + 1) * LANE] = y.astype(BF16)
            else:
                vv_ref[0, :, (h - 2 * A_HEADS) * LANE:(h - 2 * A_HEADS + 1) * LANE] = y.astype(BF16)

    hc_ext = _dot(h_ext, wt_ref[:, T_CG:T_ZC]) * _dot(h_ext, wt_ref[:, T_H:T_BG])
    cx_ref[0] = (tail(T_BG, T_CG) * _silu(tail(T_ZC, T_QA)) * conv(hc_ext, 3 * A_QK, CIN_W, C_CONV)).astype(BF16)

    sza_ref[0] = _silu(_dot(hn, w_ref[:, S_ZA:W_HEAD])).astype(BF16)
    szb_ref[0] = _silu(tail(T_ZB, T_H)).astype(BF16)

    tab = tab_ref[...]
    cq, sq = tab[:, 0:LANE], tab[:, LANE:2 * LANE]
    ck, sk = tab[:, 2 * LANE:3 * LANE], tab[:, 3 * LANE:4 * LANE]

    qa_blk = tail(T_QA, T_CKV)
    qa = qa_blk[:, :B_Q_LORA]
    nq =(qa * jax.lax.rsqrt(jnp.mean(qa * qa, axis=-1, keepdims=True) + EPS) * qng_ref[...]).astype(BF16)
    qmain = _dot(nq, wqm_ref[...])
    qswap = _dot(nq, wqs_ref[...])
    for h in range(B_HEADS):
        sl = slice(h * LANE, (h + 1) * LANE)
        qm_ref[0, :, sl] = (qmain[:, sl] * cq + qswap[:, sl] * sq).astype(BF16)

    ckv_blk = tail(T_CKV, W_TAIL)
    ckv = ckv_blk[:, :B_KV_LORA]
    nkv =(ckv * jax.lax.rsqrt(jnp.mean(ckv * ckv, axis=-1, keepdims=True) + EPS) * kvng_ref[...]).astype(BF16)
    kk = _dot(nkv, wk_ref[...])
    vt = _dot_nt(wv_ref[...], nkv)
    vrow = jax.lax.broadcasted_iota(jnp.int32, vt.shape, 0)
    ones_row = functools.reduce(jnp.logical_or, [vrow == h * B_VA + B_V for h in range(B_HEADS)])
    vm_ref[0] = jnp.where(ones_row, 1.0, vt).astype(BF16)
    kpm = ckv_blk[:, B_KV_LORA:]
    kpe = kpm * ck + qa_blk[:, LANE:] * sk
    for h in range(B_HEADS):
        sl = slice(h * LANE, (h + 1) * LANE)
        km_ref[0, :, sl] = (kk[:, sl] + kpe).astype(BF16)

    lane = jax.lax.broadcasted_iota(jnp.int32, kpm.shape, 1)
    z = kpm + gsc_ref[1:2, :]
    sp = jnp.maximum(z, 0.0) + jnp.log1p(jnp.exp(-jnp.abs(z)))
    g = -jnp.exp(gsc_ref[0:1, :]) * sp
    gates_ref[0] = jnp.where(lane < 2 * A_HEADS, jax.nn.sigmoid(kpm), jnp.where(lane < 4 * A_HEADS, g, 0.0))


def _layer_spec(a, layer):
    return pl.BlockSpec((None,) + a.shape[1:], lambda i, j: (layer,) + (0,) * (a.ndim - 1))


def _inproj(x, mod, per_batch, layer, ng, w, wt, qng, wqm, wqs, kvng, wk, wv, tab, gsc, wconv, tm):
    b, t, d = x.shape
    tm = min(tm, t)
    r = tm // HALO
    nh = t // HALO
    bm = (lambda i, j: (layer, i, 0, 0)) if per_batch else (lambda i, j: (layer, mod.shape[1] - 1, 0, 0))
    tok = lambda width: pl.BlockSpec((1, tm, width), lambda i, j: (i, j, 0))
    full = lambda a: pl.BlockSpec(a.shape, lambda i, j: (0,) * a.ndim)
    lay = lambda a: _layer_spec(a, layer)
    widths = (A_QK, A_QK, A_W, C_W, A_W, LANE, B_HEADS * LANE, B_HEADS * LANE, B_W, B_W)
    dtypes = (BF16, BF16, BF16, BF16, BF16, F32, BF16, BF16, BF16, BF16)
    V_OUT = 8
    return pl.pallas_call(
        functools.partial(_inproj_kernel, tm=tm),
        grid=(b, t // tm),
        in_specs=[tok(d),
                  pl.BlockSpec((1, HALO, d), lambda i, j: (i, jnp.maximum(j * r - 1, 0), 0)),
                  pl.BlockSpec((1, HALO, d), lambda i, j: (i, jnp.minimum((j + 1) * r, nh - 1), 0)),
                  pl.BlockSpec((None, 1, 3, d), bm), lay(ng),
                  pl.BlockSpec((None, d, W_HEAD), lambda i, j: (layer, 0, 0)), lay(wt), lay(qng), lay(wqm), lay(wqs),
                  lay(kvng), lay(wk), lay(wv),
                  pl.BlockSpec((tm, 4 * LANE), lambda i, j: (j, 0)), lay(gsc), lay(wconv)],
        out_specs=[pl.BlockSpec((1, B_WA, tm), lambda i, j: (i, 0, j)) if n == V_OUT else tok(wd)
                   for n, wd in enumerate(widths)],
        out_shape=[jax.ShapeDtypeStruct((b, B_WA, t) if n == V_OUT else (b, t, wd), dt)
                   for n, (wd, dt) in enumerate(zip(widths, dtypes))],
        compiler_params=_params(("parallel", "parallel")),
        name="inproj",
    )(x, x, x, mod, ng, w, wt, qng, wqm, wqs, kvng, wk, wv, tab, gsc, wconv)


INV_BASE_LOG2 = 3
def _unit_triangular_inverses(mats, c):
    row = jax.lax.broadcasted_iota(jnp.int32, (c, c), 0)
    col = jax.lax.broadcasted_iota(jnp.int32, (c, c), 1)
    same = lambda k: (row >> k) == (col >> k)
    eye = jnp.where(row == col, 1.0, 0.0)
    base = same(INV_BASE_LOG2)
    a0 = [jnp.where(base, a, 0.0) for a in mats]
    t = [eye - ai for ai in a0]
    apow = [ai.astype(BF16) for ai in a0]
    for _ in range(INV_BASE_LOG2 - 1):
        apow = [_dot(ap, ap).astype(BF16) for ap in apow]
        t = [ti + _dot(ti.astype(BF16), ap) for ti, ap in zip(t, apow)]
    for k in range(INV_BASE_LOG2, int(math.log2(c))):
        coupling = same(k + 1) & jnp.logical_not(same(k))
        e = [jnp.where(coupling, a, 0.0).astype(BF16) for a in mats]
        tb = [ti.astype(BF16) for ti in t]
        te = [_dot(tbi, ei).astype(BF16) for tbi, ei in zip(tb, e)]
        t = [ti - _dot(tei, tbi) for ti, tei, tbi in zip(t, te, tb)]
    return t


def _scan_kernel(qf_ref, kf_ref, vf_ref, gf_ref, qb_ref, kb_ref, vb_ref, gb_ref, s0_ref,
                 of_ref, ob_ref, s_ref, *, c, bb):
    n = pl.program_id(1)

    @pl.when(n == 0)
    def _():
        s_ref[...] = s0_ref[...]

    row = jax.lax.broadcasted_iota(jnp.int32, (c, c), 0)
    col = jax.lax.broadcasted_iota(jnp.int32, (c, c), 1)
    dirs = ((qf_ref, kf_ref, vf_ref, gf_ref), (qb_ref, kb_ref, vb_ref, gb_ref))
    o_refs = (of_ref, ob_ref)
    incl = (col <= row, col >= row)
    strict = (col < row, col > row)
    gates, gcb, gcb_t, gl_row = {}, {}, {}, {}
    for e in range(bb):
        for d in range(2):
            g = dirs[d][3][e]
            tri = jnp.where(incl[d], 1.0, 0.0).astype(BF16)
            hi = g.astype(BF16)
            r1 = g - hi.astype(F32)
            mid = r1.astype(BF16)
            lo = (r1 - mid.astype(F32)).astype(BF16)
            gc_all = _dot(tri, hi) + _dot(tri, mid) + _dot(tri, lo)
            gates[e, d] = g
            gcb[e, d] = gc_all
            gcb_t[e, d] = gc_all.T
            gl_row[e, d] = gc_all[c - 1:c, :] if d == 0 else gc_all[0:1, :]

    chains = [(e, d, h) for e in range(bb) for d in range(2) for h in range(A_HEADS)]
    sls = [slice(h * LANE, (h + 1) * LANE) for _, _, h in chains]
    jbs = [A_HEADS * d + h for _, d, h in chains]
    jgs = [2 * A_HEADS + jb for jb in jbs]
    beta = [gates[e, d][:, jb:jb + 1] for (e, d, _), jb in zip(chains, jbs)]
    gc = [gcb[e, d][:, jg:jg + 1] for (e, d, _), jg in zip(chains, jgs)]
    gcr = [gcb_t[e, d][jg:jg + 1, :] for (e, d, _), jg in zip(chains, jgs)]
    gl = [gl_row[e, d][:, jg:jg + 1] for (e, d, _), jg in zip(chains, jgs)]
    q = [dirs[d][0][e, :, sl] for (e, d, _), sl in zip(chains, sls)]
    k = [dirs[d][1][e, :, sl] for (e, d, _), sl in zip(chains, sls)]
    v = [dirs[d][2][e, :, sl] for (e, d, _), sl in zip(chains, sls)]
    decay = [jnp.where(incl[d], jnp.exp(jnp.where(incl[d], gci - gcri, 0.0)), 0.0)
             for (_, d, _), gci, gcri in zip(chains, gc, gcr)]
    kbeta = [ki.astype(F32) * bi for ki, bi in zip(k, beta)]
    egc = [jnp.exp(gci) for gci in gc]
    a = [jnp.where(strict[d], _dot_nt(kbi.astype(BF16), ki) * di, 0.0)
         for (_, d, _), kbi, ki, di in zip(chains, kbeta, k, decay)]
    qk = [(_dot_nt(qi, ki) * di).astype(BF16) for qi, ki, di in zip(q, k, decay)]
    tinv = [ti.astype(BF16) for ti in _unit_triangular_inverses(a, c)]
    kd_t = [(ki.astype(F32) * jnp.exp(gli - gci)).T.astype(BF16) for ki, gli, gci in zip(k, gl, gc)]
    kbe = [(kbi * ei).astype(BF16) for kbi, ei in zip(kbeta, egc)]
    vbeta = [vi.astype(F32) * bi for vi, bi in zip(v, beta)]
    qd = [(qi.astype(F32) * ei).astype(BF16) for qi, ei in zip(q, egc)]
    s = [s_ref[e, d, h] for e, d, h in chains]
    sb = [si.astype(BF16) for si in s]
    rhs = [(vbi - _dot(kbi, sbi)).astype(BF16) for vbi, kbi, sbi in zip(vbeta, kbe, sb)]
    v_new = [_dot(ti, ri).astype(BF16) for ti, ri in zip(tinv, rhs)]
    o = [_dot(qdi, sbi) + _dot(qki, vi) for qdi, sbi, qki, vi in zip(qd, sb, qk, v_new)]
    for (e, d, h), sl, oi in zip(chains, sls, o):
        o_refs[d][e, :, sl] = oi.astype(BF16)
    for (e, d, h), gli, si, kdi, vi in zip(chains, gl, s, kd_t, v_new):
        s_ref[e, d, h] = jnp.exp(gli) * si + _dot(kdi, vi)


def _scan(qn, kn, vv, gates, s0, bb):
    b, t, _ = qn.shape
    c = GDN_CHUNK
    n = t // c
    assert b % bb == 0
    fwd = lambda width: pl.BlockSpec((bb, c, width), lambda i, j: (i, j, 0))
    bwd = lambda width: pl.BlockSpec((bb, c, width), lambda i, j: (i, n - 1 - j, 0))
    st = pl.BlockSpec((bb, 2, A_HEADS, A_DK, A_DV), lambda i, j: (i, 0, 0, 0, 0))
    return pl.pallas_call(
        functools.partial(_scan_kernel, c=c, bb=bb),
        grid=(b // bb, n),
        in_specs=[fwd(A_QK), fwd(A_QK), fwd(A_W), fwd(LANE), bwd(A_QK), bwd(A_QK), bwd(A_W), bwd(LANE), st],
        out_specs=[fwd(A_W), bwd(A_W), st],
        out_shape=[jax.ShapeDtypeStruct((b, t, A_W), BF16), jax.ShapeDtypeStruct((b, t, A_W), BF16),
                   jax.ShapeDtypeStruct(s0.shape, F32)],
        compiler_params=_params(("parallel", "arbitrary")),
        name="gdn_scan",
    )(qn, kn, vv, gates, qn, kn, vv, gates, s0)


def _attention_tile(q_ref, kv_refs, szb_ref):
    n_src = len(kv_refs) // 2
    heads = range(B_HEADS)
    qh = [q_ref[0, :, h * LANE:(h + 1) * LANE] for h in heads]
    blocks = []
    for i in range(n_src):
        tk = kv_refs[2 * i].shape[1]
        kb = min(ATTN_KEY_BLOCK, tk)
        blocks += [(kv_refs[2 * i], kv_refs[2 * i + 1], slice(j * kb, (j + 1) * kb)) for j in range(tk // kb)]

    groups = [tuple(range(g, g + ATTN_HEAD_GROUP)) for g in range(0, B_HEADS, ATTN_HEAD_GROUP)]
    units = [(blk, grp) for blk in blocks for grp in groups]

    def scores(unit):
        (k_ref, _, ks), grp = unit
        return [_dot_nt(k_ref[0, ks, h * LANE:(h + 1) * LANE], qh[h]) for h in grp]

    m = [None] * B_HEADS
    acc = [None] * B_HEADS
    s_next = scores(units[0])
    for n, ((_, vt_ref, ks), grp) in enumerate(units):
        s, s_next = s_next, (scores(units[n + 1]) if n + 1 < len(units) else None)
        mb = [si.max(axis=0, keepdims=True) for si in s]
        m_new = [mbi if m[h] is None else jnp.maximum(m[h], mbi) for h, mbi in zip(grp, mb)]
        p = [jnp.exp2(si - mn).astype(BF16) for si, mn in zip(s, m_new)]
        pv = [_dot(vt_ref[0, h * B_VA:(h + 1) * B_VA, ks], pi) for h, pi in zip(grp, p)]
        for h, mn, pvi in zip(grp, m_new, pv):
            acc[h] = pvi if m[h] is None else jnp.exp2(m[h] - mn) * acc[h] + pvi
            m[h] = mn
    o = jnp.concatenate([a[:B_V] / a[B_V:B_V + 1] for a in acc], axis=0).T
    return (o * szb_ref[0].astype(F32)).astype(BF16)


def _attn_out_kernel(*refs, n_src, final):
    q_ref = refs[0]
    kv_refs = refs[1:1 + 2 * n_src]
    (szb_ref, x_ref, of_ref, ob_ref, sza_ref, cx_ref, gng_ref, mod_ref, w_ref, fg_ref, o_ref) = refs[1 + 2 * n_src:]
    bx = _attention_tile(q_ref, kv_refs, szb_ref)
    o = of_ref[0].astype(F32) + ob_ref[0].astype(F32)
    sza = sza_ref[0].astype(F32)
    acc = None
    for h in range(A_HEADS):
        sl = slice(h * LANE, (h + 1) * LANE)
        oh = o[:, sl]
        ah = oh * jax.lax.rsqrt(jnp.mean(oh * oh, axis=-1, keepdims=True) + EPS) * gng_ref[...]
        term = _dot((ah * sza[:, sl]).astype(BF16), w_ref[sl, :])
        acc = term if acc is None else acc + term
    acc = acc + _dot(bx, w_ref[A_W:A_W + B_W, :]) + _dot(cx_ref[0], w_ref[A_W + B_W:, :])
    xn = x_ref[0] + mod_ref[0, 2:3, :] * acc
    if final:
        xn = xn * jax.lax.rsqrt(jnp.mean(xn * xn, axis=-1, keepdims=True) + EPS) * fg_ref[...]
    o_ref[0] = xn


def _attn_out(x, q, kvs, szb, of, ob, sza, cx, gng, mod, per_batch, layer, w, fg, final, tm):
    b, t, d = x.shape
    tm = min(tm, t)
    bm = (lambda i, j: (layer, i, 0, 0)) if per_batch else (lambda i, j: (layer, mod.shape[1] - 1, 0, 0))
    tok = lambda width: pl.BlockSpec((1, tm, width), lambda i, j: (i, j, 0))
    full = lambda a: pl.BlockSpec(a.shape, lambda i, j: (0,) * a.ndim)
    in_specs = [tok(B_HEADS * LANE)]
    args = [q]
    for k, v in kvs:
        in_specs.append(pl.BlockSpec((1,) + k.shape[1:], lambda i, j: (i, 0, 0)))
        in_specs.append(pl.BlockSpec((1,) + v.shape[1:], lambda i, j: (i, 0, 0)))
        args += [k, v]
    in_specs += [tok(B_W), tok(d), tok(A_W), tok(A_W), tok(A_W), tok(C_W), _layer_spec(gng, layer),
                 pl.BlockSpec((None, 1, 3, d), bm), _layer_spec(w, layer), full(fg)]
    args += [szb, x, of, ob, sza, cx, gng, mod, w, fg]
    return pl.pallas_call(
        functools.partial(_attn_out_kernel, n_src=len(kvs), final=final),
        grid=(b, t // tm),
        in_specs=in_specs,
        out_specs=tok(d),
        out_shape=jax.ShapeDtypeStruct((b, t, d), F32),
        compiler_params=_params(("parallel", "parallel")),
        name="attn_out",
    )(*args)


def _deinterleave(w32):
    even, odd = w32[..., 0::2], w32[..., 1::2]
    return jnp.concatenate([even, odd], axis=-1), jnp.concatenate([odd, even], axis=-1)


def _w_layout_kernel(w_ref, pm_ref, ps_ref, head_ref, tail_ref):
    w = w_ref[0]
    head_ref[0] = w[:, :W_HEAD].astype(BF16)
    o_b, o_qa, o_kv, o_zb, o_end = 2048, 2064, 2256, 2416, 3696
    kpe = w[:, o_kv + B_KV_LORA:o_zb].astype(BF16)
    kp_main = _dot(kpe, pm_ref[...])
    kp_swap = _dot(kpe, ps_ref[...])
    z = lambda n: jnp.zeros((w.shape[0], n), F32)
    cols = [w[:, o_zb:o_end],
            w[:, o_qa:o_kv], kp_swap, z(2 * LANE - B_Q_LORA - B_ROPE),
            w[:, o_kv:o_kv + B_KV_LORA], w[:, o_b:o_qa], z(B_NOPE - 4 * A_HEADS), kp_main,
            z(LANE - B_NOPE - B_ROPE)]
    tail_ref[0] = jnp.concatenate(cols, axis=-1).astype(BF16)


def _layout_w_in(w):
    depth, d, n_in = w.shape
    tk = 256
    src_main = np.concatenate([np.arange(0, B_ROPE, 2), np.arange(1, B_ROPE, 2)])
    src_swap = np.concatenate([np.arange(1, B_ROPE, 2), np.arange(0, B_ROPE, 2)])
    sel = lambda src: jnp.asarray(np.eye(B_ROPE, dtype=np.float32)[:, src], dtype=BF16)
    return pl.pallas_call(
        _w_layout_kernel,
        grid=(depth, d // tk),
        in_specs=[pl.BlockSpec((1, tk, n_in), lambda l, i: (l, i, 0)),
                  pl.BlockSpec((B_ROPE, B_ROPE), lambda l, i: (0, 0)),
                  pl.BlockSpec((B_ROPE, B_ROPE), lambda l, i: (0, 0))],
        out_specs=[pl.BlockSpec((1, tk, W_HEAD), lambda l, i: (l, i, 0)),
                   pl.BlockSpec((1, tk, W_TAIL), lambda l, i: (l, i, 0))],
        out_shape=[jax.ShapeDtypeStruct((depth, d, W_HEAD), BF16), jax.ShapeDtypeStruct((depth, d, W_TAIL), BF16)],
        compiler_params=_params(("parallel", "parallel")),
        name="w_layout",
    )(w, sel(src_main), sel(src_swap))


def _layout_w_qb(w):
    w = w.astype(BF16)
    z = lambda n: jnp.zeros(w.shape[:-1] + (n,), w.dtype)
    hd = B_NOPE + B_ROPE
    main, swap = [], []
    for h in range(B_HEADS):
        rm, rs = _deinterleave(w[..., h * hd + B_NOPE:(h + 1) * hd])
        main += [w[..., h * hd:h * hd + B_NOPE], rm, z(LANE - hd)]
        swap += [z(B_NOPE), rs, z(LANE - hd)]
    return jnp.concatenate(main, axis=-1), jnp.concatenate(swap, axis=-1)


def _layout_w_kvb(w):
    w = w.astype(BF16)
    hd = B_NOPE + B_V
    wk, wv = [], []
    for h in range(B_HEADS):
        wk += [w[..., h * hd:h * hd + B_NOPE], jnp.zeros(w.shape[:-1] + (LANE - B_NOPE,), w.dtype)]
        wv += [w[..., h * hd + B_NOPE:(h + 1) * hd], jnp.zeros(w.shape[:-1] + (B_VA - B_V,), w.dtype)]
    return jnp.concatenate(wk, axis=-1), jnp.swapaxes(jnp.concatenate(wv, axis=-1), -1, -2)


def _rope_tables(t, rotate):
    half = B_ROPE // 2
    if rotate:
        pos = np.arange(t)
        n_freq = B_ROPE // 4
        inv_freq = ROPE_THETA ** (-np.arange(n_freq, dtype=np.float64) / n_freq)
        ang = np.concatenate([(pos // GRID_W)[:, None] * inv_freq, (pos % GRID_W)[:, None] * inv_freq], axis=-1)
        cos, sin = np.cos(ang).astype(np.float32), np.sin(ang).astype(np.float32)
    else:
        cos, sin = np.ones((t, half), np.float32), np.zeros((t, half), np.float32)
    z = lambda n: np.zeros((t, n), np.float32)
    tail = LANE - B_NOPE - B_ROPE
    scale = np.float32((B_NOPE + B_ROPE) ** -0.5 * math.log2(math.e))
    cq = np.concatenate([np.ones((t, B_NOPE), np.float32), cos, cos, z(tail)], axis=1) * scale
    sq = np.concatenate([z(B_NOPE), -sin, sin, z(tail)], axis=1) * scale
    ck = np.concatenate([z(B_NOPE), cos, cos, z(tail)], axis=1)
    sk = np.concatenate([z(B_NOPE), -sin, sin, z(tail)], axis=1)
    return jnp.asarray(np.concatenate([cq, sq, ck, sk], axis=1), dtype=F32)


def kernel(x, c, ctx, c_ctx, w_ada, b_ada, norm_g, w_in, gdn_conv, gdn_a_log, gdn_dt_bias, gdn_norm_g,
           mla_q_norm_g, mla_w_qb, mla_kv_norm_g, mla_w_kvb, conv_w, w_out, final_norm_g):
    bsz, t, d = x.shape
    t_ctx = ctx.shape[1]
    depth = w_ada.shape[0]
    assert t % GDN_CHUNK == 0 and t_ctx % GDN_CHUNK == 0 and t % GRID_W == 0
    tm = 512
    bb = 2 if bsz % 2 == 0 else 1

    rows = -(-(bsz + 1) // 8) * 8
    cc = jnp.concatenate([c, jnp.zeros((rows - bsz - 1, d), F32), c_ctx[None, :]], axis=0)
    mod = _ada(cc, w_ada, b_ada).reshape(depth, rows, 3, d)

    tab_x = _rope_tables(t, True)
    tab_c = _rope_tables(t_ctx, False)
    fg = final_norm_g.reshape(1, d)
    w_all, wt_all = _layout_w_in(w_in)
    wqm_all, wqs_all = _layout_w_qb(mla_w_qb)
    wk_all, wv_all = _layout_w_kvb(mla_w_kvb)
    wo_all = w_out.astype(BF16)
    ng = norm_g.reshape(depth, 1, d)
    qng = mla_q_norm_g.reshape(depth, 1, B_Q_LORA)
    kvng = mla_kv_norm_g.reshape(depth, 1, B_KV_LORA)
    gng = gdn_norm_g.reshape(depth, 1, A_DV)
    gsc = jnp.pad(jnp.stack([gdn_a_log.reshape(depth, -1), gdn_dt_bias.reshape(depth, -1)], axis=1),
                  ((0, 0), (0, 6), (2 * A_HEADS, LANE - 4 * A_HEADS)))
    tap_pad = (A_CONV - C_CONV) // 2
    wconv = jnp.pad(jnp.concatenate([gdn_conv, jnp.pad(conv_w, ((0, 0), (tap_pad, tap_pad), (0, 0)))], axis=2),
                    ((0, 0), (0, 8 - A_CONV), (0, 0)))
    s_zero = jnp.zeros((bsz, 2, A_HEADS, A_DK, A_DV), F32)

    for l in range(depth):
        last = l == depth - 1
        shared = (l, ng, w_all, wt_all, qng, wqm_all, wqs_all, kvng, wk_all, wv_all)
        (qn_c, kn_c, vv_c, cx_c, sza_c, gates_c, qm_c, km_c, vm_c, szb_c) = _inproj(
            ctx, mod, False, *shared, tab_c, gsc, wconv, tm)
        (qn_x, kn_x, vv_x, cx_x, sza_x, gates_x, qm_x, km_x, vm_x, szb_x) = _inproj(
            x, mod, True, *shared, tab_x, gsc, wconv, tm)

        of_c, ob_c, s_ctx = _scan(qn_c, kn_c, vv_c, gates_c, s_zero, bb)
        of_x, ob_x, _ = _scan(qn_x, kn_x, vv_x, gates_x, s_ctx, bb)

        x_new = _attn_out(x, qm_x, [(km_x, vm_x), (km_c, vm_c)], szb_x, of_x, ob_x, sza_x, cx_x, gng, mod, True,
                          l, wo_all, fg, last, tm)
        if not last:
            ctx = _attn_out(ctx, qm_c, [(km_c, vm_c)], szb_c, of_c, ob_c, sza_c, cx_c, gng, mod, False,
                            l, wo_all, fg, False, tm)
        x = x_new
    return x
```

```python
import functools
import math

import jax
import jax.numpy as jnp
import numpy as np
from jax.experimental import pallas as pl
from jax.experimental.pallas import tpu as pltpu

F32 = jnp.float32
BF16 = jnp.bfloat16

EPS = 1e-6
GRID_W = 64
ROPE_THETA = 10000.0

A_HEADS = 4
A_DK = 128
A_DV = 128
A_QK = A_HEADS * A_DK
A_W = A_HEADS * A_DV
A_CONV = 5
B_HEADS = 4
B_NOPE = 64
B_ROPE = 32
B_V = 64
B_Q_LORA = 192
B_KV_LORA = 128
B_W = B_HEADS * B_V
B_VA = B_V + 16
B_WA = B_HEADS * B_VA
C_W = 256
C_CONV = 3

LANE = 128
HALO = 16
GDN_CHUNK = 128
ATTN_KEY_BLOCK = 256
ATTN_HEAD_GROUP = 4
VMEM_LIMIT = 48 * 1024 * 1024

W_HEAD = 2048
S_ZA = 1536
T_ZB = 0
T_H = 256
T_BG = 512
T_CG = 768
T_ZC = 1024
T_QA = 1280
T_CKV = 1536
W_TAIL = 1792
CIN_W = 3 * A_QK + C_W


def _dot(a, b):
    return jnp.dot(a, b, preferred_element_type=F32)


def _dot_nt(a, b):
    return jax.lax.dot_general(a, b, (((1,), (1,)), ((), ())), preferred_element_type=F32)


def _silu(x):
    h = 0.5 * x
    return h + h * jnp.tanh(h)


def _params(sem):
    return pltpu.CompilerParams(dimension_semantics=sem, vmem_limit_bytes=VMEM_LIMIT)


def _ada_kernel(c_ref, w_ref, b_ref, o_ref):
    s = _silu(c_ref[...]).astype(BF16)
    o_ref[0] = _dot(s, w_ref[0].astype(BF16)) + b_ref[0]


def _ada(cc, w_ada, b_ada):
    depth, d, n3 = w_ada.shape
    r = cc.shape[0]
    tn = 768
    return pl.pallas_call(
        _ada_kernel,
        grid=(depth, n3 // tn),
        in_specs=[pl.BlockSpec((r, d), lambda l, j: (0, 0)),
                  pl.BlockSpec((1, d, tn), lambda l, j: (l, 0, j)),
                  pl.BlockSpec((1, 1, tn), lambda l, j: (l, 0, j))],
        out_specs=pl.BlockSpec((1, r, tn), lambda l, j: (l, 0, j)),
        out_shape=jax.ShapeDtypeStruct((depth, r, n3), F32),
        compiler_params=_params(("parallel", "parallel")),
        name="ada",
    )(cc, w_ada, b_ada.reshape(depth, 1, n3))


def _inproj_kernel(x_ref, xp_ref, xn_ref, mod_ref, ng_ref, w_ref, wt_ref, qng_ref, wqm_ref, wqs_ref, kvng_ref,
                   wk_ref, wv_ref, tab_ref, gsc_ref, wc_ref, *out_refs, tm, mixer_outputs):
    if mixer_outputs:
        qn_ref, kn_ref, vv_ref, cx_ref, sza_ref, gates_ref, qm_ref, km_ref, vm_ref, szb_ref = out_refs
    else:
        qn_ref, kn_ref, vv_ref, gates_ref, km_ref, vm_ref = out_refs
    j = pl.program_id(1)
    nj = pl.num_programs(1)

    def normed(xv):
        y = xv * jax.lax.rsqrt(jnp.mean(xv * xv, axis=-1, keepdims=True) + EPS) * ng_ref[...]
        return y * (1.0 + mod_ref[0, 1:2, :]) + mod_ref[0, 0:1, :]

    hn = normed(x_ref[0]).astype(BF16)
    h_prev = jnp.where(j > 0, normed(xp_ref[0]), 0.0).astype(BF16)
    h_next = jnp.where(j < nj - 1, normed(xn_ref[0]), 0.0).astype(BF16)
    h_ext = jnp.concatenate([h_prev, hn, h_next], axis=0)

    def tail(a, b):
        return _dot(hn, wt_ref[:, a:b])

    def conv(u_ext, c0, c1, taps=A_CONV):
        pad = A_CONV // 2
        rows = u_ext.shape[0]
        acc = None
        for k in range(pad - taps // 2, pad + taps // 2 + 1):
            shifted = u_ext if k == pad else pltpu.roll(u_ext, (pad - k) % rows, 0)
            term = shifted[HALO:HALO + tm, :] * wc_ref[k:k + 1, c0:c1]
            acc = term if acc is None else acc + term
        return acc

    blk = 2 * LANE
    for i in range(3 * A_QK // blk):
        y2 = _silu(conv(_dot(h_ext, w_ref[:, i * blk:(i + 1) * blk]), i * blk, (i + 1) * blk))
        for hh in range(2):
            h = 2 * i + hh
            y = y2[:, hh * LANE:(hh + 1) * LANE]
            if h < 2 * A_HEADS:
                inv = jax.lax.rsqrt(jnp.sum(y * y, axis=-1, keepdims=True) + EPS)
                y = y * (inv * (A_DK ** -0.5) if h < A_HEADS else inv)
            if h < A_HEADS:
                qn_ref[0, :, h * LANE:(h + 1) * LANE] = y.astype(BF16)
            elif h < 2 * A_HEADS:
                kn_ref[0, :, (h - A_HEADS) * LANE:(h - A_HEADS + 1) * LANE] = y.astype(BF16)
            else:
                vv_ref[0, :, (h - 2 * A_HEADS) * LANE:(h - 2 * A_HEADS + 1) * LANE] = y.astype(BF16)

    tab = tab_ref[...]
    cq, sq = tab[:, 0:LANE], tab[:, LANE:2 * LANE]
    ck, sk = tab[:, 2 * LANE:3 * LANE], tab[:, 3 * LANE:4 * LANE]
    qa_blk = tail(T_QA, T_CKV)

    if mixer_outputs:
        hc_ext = _dot(h_ext, wt_ref[:, T_CG:T_ZC]) * _dot(h_ext, wt_ref[:, T_H:T_BG])
        cx_ref[0] = (tail(T_BG, T_CG) * _silu(tail(T_ZC, T_QA))
                     * conv(hc_ext, 3 * A_QK, CIN_W, C_CONV)).astype(BF16)

        sza_ref[0] = _silu(_dot(hn, w_ref[:, S_ZA:W_HEAD])).astype(BF16)
        szb_ref[0] = _silu(tail(T_ZB, T_H)).astype(BF16)

        qa = qa_blk[:, :B_Q_LORA]
        nq = (qa * jax.lax.rsqrt(jnp.mean(qa * qa, axis=-1, keepdims=True) + EPS) * qng_ref[...]).astype(BF16)
        qmain = _dot(nq, wqm_ref[...])
        qswap = _dot(nq, wqs_ref[...])
        for h in range(B_HEADS):
            sl = slice(h * LANE, (h + 1) * LANE)
            qm_ref[0, :, sl] = (qmain[:, sl] * cq + qswap[:, sl] * sq).astype(BF16)

    ckv_blk = tail(T_CKV, W_TAIL)
    ckv = ckv_blk[:, :B_KV_LORA]
    nkv =(ckv * jax.lax.rsqrt(jnp.mean(ckv * ckv, axis=-1, keepdims=True) + EPS) * kvng_ref[...]).astype(BF16)
    kk = _dot(nkv, wk_ref[...])
    vt = _dot_nt(wv_ref[...], nkv)
    vrow = jax.lax.broadcasted_iota(jnp.int32, vt.shape, 0)
    ones_row = functools.reduce(jnp.logical_or, [vrow == h * B_VA + B_V for h in range(B_HEADS)])
    vm_ref[0] = jnp.where(ones_row, 1.0, vt).astype(BF16)
    kpm = ckv_blk[:, B_KV_LORA:]
    kpe = kpm * ck + qa_blk[:, LANE:] * sk
    for h in range(B_HEADS):
        sl = slice(h * LANE, (h + 1) * LANE)
        km_ref[0, :, sl] = (kk[:, sl] + kpe).astype(BF16)

    lane = jax.lax.broadcasted_iota(jnp.int32, kpm.shape, 1)
    z = kpm + gsc_ref[1:2, :]
    sp = jnp.maximum(z, 0.0) + jnp.log1p(jnp.exp(-jnp.abs(z)))
    g = -jnp.exp(gsc_ref[0:1, :]) * sp
    gates_ref[0] = jnp.where(lane < 2 * A_HEADS, jax.nn.sigmoid(kpm), jnp.where(lane < 4 * A_HEADS, g, 0.0))


def _layer_spec(a, layer):
    return pl.BlockSpec((None,) + a.shape[1:], lambda i, j: (layer,) + (0,) * (a.ndim - 1))


def _inproj(x, mod, per_batch, layer, ng, w, wt, qng, wqm, wqs, kvng, wk, wv, tab, gsc, wconv, tm,
            mixer_outputs=True):
    b, t, d = x.shape
    tm = min(tm, t)
    r = tm // HALO
    nh = t // HALO
    bm = (lambda i, j: (layer, i, 0, 0)) if per_batch else (lambda i, j: (layer, mod.shape[1] - 1, 0, 0))
    tok = lambda width: pl.BlockSpec((1, tm, width), lambda i, j: (i, j, 0))
    full = lambda a: pl.BlockSpec(a.shape, lambda i, j: (0,) * a.ndim)
    lay = lambda a: _layer_spec(a, layer)
    outs = [(A_QK, BF16, True), (A_QK, BF16, True), (A_W, BF16, True), (C_W, BF16, False), (A_W, BF16, False),
            (LANE, F32, True), (B_HEADS * LANE, BF16, False), (B_HEADS * LANE, BF16, True), (None, BF16, True),
            (B_W, BF16, False)]
    outs = [o for o in outs if mixer_outputs or o[2]]
    widths = [o[0] for o in outs]
    dtypes = [o[1] for o in outs]
    V_OUT = widths.index(None)
    return pl.pallas_call(
        functools.partial(_inproj_kernel, tm=tm, mixer_outputs=mixer_outputs),
        grid=(b, t // tm),
        in_specs=[tok(d),
                  pl.BlockSpec((1, HALO, d), lambda i, j: (i, jnp.maximum(j * r - 1, 0), 0)),
                  pl.BlockSpec((1, HALO, d), lambda i, j: (i, jnp.minimum((j + 1) * r, nh - 1), 0)),
                  pl.BlockSpec((None, 1, 3, d), bm), lay(ng),
                  pl.BlockSpec((None, d, W_HEAD), lambda i, j: (layer, 0, 0)), lay(wt), lay(qng), lay(wqm), lay(wqs),
                  lay(kvng), lay(wk), lay(wv),
                  pl.BlockSpec((tm, 4 * LANE), lambda i, j: (j, 0)), lay(gsc), lay(wconv)],
        out_specs=[pl.BlockSpec((1, B_WA, tm), lambda i, j: (i, 0, j)) if n == V_OUT else tok(wd)
                   for n, wd in enumerate(widths)],
        out_shape=[jax.ShapeDtypeStruct((b, B_WA, t) if n == V_OUT else (b, t, wd), dt)
                   for n, (wd, dt) in enumerate(zip(widths, dtypes))],
        compiler_params=_params(("parallel", "parallel")),
        name="inproj",
    )(x, x, x, mod, ng, w, wt, qng, wqm, wqs, kvng, wk, wv, tab, gsc, wconv)


INV_BASE_LOG2 = 3
def _unit_triangular_inverses(mats, c):
    row = jax.lax.broadcasted_iota(jnp.int32, (c, c), 0)
    col = jax.lax.broadcasted_iota(jnp.int32, (c, c), 1)
    same = lambda k: (row >> k) == (col >> k)
    eye = jnp.where(row == col, 1.0, 0.0)
    base = same(INV_BASE_LOG2)
    a0 = [jnp.where(base, a, 0.0) for a in mats]
    t = [eye - ai for ai in a0]
    apow = [ai.astype(BF16) for ai in a0]
    for _ in range(INV_BASE_LOG2 - 1):
        apow = [_dot(ap, ap).astype(BF16) for ap in apow]
        t = [ti + _dot(ti.astype(BF16), ap) for ti, ap in zip(t, apow)]
    for k in range(INV_BASE_LOG2, int(math.log2(c))):
        coupling = same(k + 1) & jnp.logical_not(same(k))
        e = [jnp.where(coupling, a, 0.0).astype(BF16) for a in mats]
        tb = [ti.astype(BF16) for ti in t]
        te = [_dot(tbi, ei).astype(BF16) for tbi, ei in zip(tb, e)]
        t = [ti - _dot(tei, tbi) for ti, tei, tbi in zip(t, te, tb)]
    return t


def _scan_kernel(qf_ref, kf_ref, vf_ref, gf_ref, qb_ref, kb_ref, vb_ref, gb_ref, s0_ref,
                 *out_refs, c, bb):
    s_ref = out_refs[-1]
    o_refs = out_refs[:-1]
    n = pl.program_id(1)

    @pl.when(n == 0)
    def _():
        s_ref[...] = s0_ref[...]

    row = jax.lax.broadcasted_iota(jnp.int32, (c, c), 0)
    col = jax.lax.broadcasted_iota(jnp.int32, (c, c), 1)
    dirs = ((qf_ref, kf_ref, vf_ref, gf_ref), (qb_ref, kb_ref, vb_ref, gb_ref))
    incl = (col <= row, col >= row)
    strict = (col < row, col > row)
    gates, gcb, gcb_t, gl_row = {}, {}, {}, {}
    for e in range(bb):
        for d in range(2):
            g = dirs[d][3][e]
            tri = jnp.where(incl[d], 1.0, 0.0).astype(BF16)
            hi = g.astype(BF16)
            r1 = g - hi.astype(F32)
            mid = r1.astype(BF16)
            lo = (r1 - mid.astype(F32)).astype(BF16)
            gc_all = _dot(tri, hi) + _dot(tri, mid) + _dot(tri, lo)
            gates[e, d] = g
            gcb[e, d] = gc_all
            gcb_t[e, d] = gc_all.T
            gl_row[e, d] = gc_all[c - 1:c, :] if d == 0 else gc_all[0:1, :]

    chains = [(e, d, h) for e in range(bb) for d in range(2) for h in range(A_HEADS)]
    sls = [slice(h * LANE, (h + 1) * LANE) for _, _, h in chains]
    jbs = [A_HEADS * d + h for _, d, h in chains]
    jgs = [2 * A_HEADS + jb for jb in jbs]
    beta = [gates[e, d][:, jb:jb + 1] for (e, d, _), jb in zip(chains, jbs)]
    gc = [gcb[e, d][:, jg:jg + 1] for (e, d, _), jg in zip(chains, jgs)]
    gcr = [gcb_t[e, d][jg:jg + 1, :] for (e, d, _), jg in zip(chains, jgs)]
    gl = [gl_row[e, d][:, jg:jg + 1] for (e, d, _), jg in zip(chains, jgs)]
    q = [dirs[d][0][e, :, sl] for (e, d, _), sl in zip(chains, sls)]
    k = [dirs[d][1][e, :, sl] for (e, d, _), sl in zip(chains, sls)]
    v = [dirs[d][2][e, :, sl] for (e, d, _), sl in zip(chains, sls)]
    decay = [jnp.where(incl[d], jnp.exp(jnp.where(incl[d], gci - gcri, 0.0)), 0.0)
             for (_, d, _), gci, gcri in zip(chains, gc, gcr)]
    kbeta = [ki.astype(F32) * bi for ki, bi in zip(k, beta)]
    egc = [jnp.exp(gci) for gci in gc]
    a = [jnp.where(strict[d], _dot_nt(kbi.astype(BF16), ki) * di, 0.0)
         for (_, d, _), kbi, ki, di in zip(chains, kbeta, k, decay)]
    qk = [(_dot_nt(qi, ki) * di).astype(BF16) for qi, ki, di in zip(q, k, decay)]
    tinv = [ti.astype(BF16) for ti in _unit_triangular_inverses(a, c)]
    kd_t = [(ki.astype(F32) * jnp.exp(gli - gci)).T.astype(BF16) for ki, gli, gci in zip(k, gl, gc)]
    kbe = [(kbi * ei).astype(BF16) for kbi, ei in zip(kbeta, egc)]
    vbeta = [vi.astype(F32) * bi for vi, bi in zip(v, beta)]
    qd = [(qi.astype(F32) * ei).astype(BF16) for qi, ei in zip(q, egc)]
    s = [s_ref[e, d, h] for e, d, h in chains]
    sb = [si.astype(BF16) for si in s]
    rhs = [(vbi - _dot(kbi, sbi)).astype(BF16) for vbi, kbi, sbi in zip(vbeta, kbe, sb)]
    v_new = [_dot(ti, ri).astype(BF16) for ti, ri in zip(tinv, rhs)]
    if o_refs:
        o = [_dot(qdi, sbi) + _dot(qki, vi) for qdi, sbi, qki, vi in zip(qd, sb, qk, v_new)]
        for (e, d, h), sl, oi in zip(chains, sls, o):
            o_refs[d][e, :, sl] = oi.astype(BF16)
    for (e, d, h), gli, si, kdi, vi in zip(chains, gl, s, kd_t, v_new):
        s_ref[e, d, h] = jnp.exp(gli) * si + _dot(kdi, vi)


def _scan(qn, kn, vv, gates, s0, bb, state_only=False):
    b, t, _ = qn.shape
    c = GDN_CHUNK
    n = t // c
    assert b % bb == 0
    fwd = lambda width: pl.BlockSpec((bb, c, width), lambda i, j: (i, j, 0))
    bwd = lambda width: pl.BlockSpec((bb, c, width), lambda i, j: (i, n - 1 - j, 0))
    st = pl.BlockSpec((bb, 2, A_HEADS, A_DK, A_DV), lambda i, j: (i, 0, 0, 0, 0))
    o_specs = [] if state_only else [fwd(A_W), bwd(A_W)]
    o_shapes = [] if state_only else [jax.ShapeDtypeStruct((b, t, A_W), BF16)] * 2
    return pl.pallas_call(
        functools.partial(_scan_kernel, c=c, bb=bb),
        grid=(b // bb, n),
        in_specs=[fwd(A_QK), fwd(A_QK), fwd(A_W), fwd(LANE), bwd(A_QK), bwd(A_QK), bwd(A_W), bwd(LANE), st],
        out_specs=o_specs + [st],
        out_shape=o_shapes + [jax.ShapeDtypeStruct(s0.shape, F32)],
        compiler_params=_params(("parallel", "arbitrary")),
        name="gdn_scan",
    )(qn, kn, vv, gates, qn, kn, vv, gates, s0)


def _attention_tile(q_ref, kv_refs, szb_ref):
    n_src = len(kv_refs) // 2
    heads = range(B_HEADS)
    qh = [q_ref[0, :, h * LANE:(h + 1) * LANE] for h in heads]
    blocks = []
    for i in range(n_src):
        tk = kv_refs[2 * i].shape[1]
        kb = min(ATTN_KEY_BLOCK, tk)
        blocks += [(kv_refs[2 * i], kv_refs[2 * i + 1], slice(j * kb, (j + 1) * kb)) for j in range(tk // kb)]

    groups = [tuple(range(g, g + ATTN_HEAD_GROUP)) for g in range(0, B_HEADS, ATTN_HEAD_GROUP)]
    units = [(blk, grp) for blk in blocks for grp in groups]

    def scores(unit):
        (k_ref, _, ks), grp = unit
        return [_dot_nt(k_ref[0, ks, h * LANE:(h + 1) * LANE], qh[h]) for h in grp]

    m = [None] * B_HEADS
    acc = [None] * B_HEADS
    s_next = scores(units[0])
    for n, ((_, vt_ref, ks), grp) in enumerate(units):
        s, s_next = s_next, (scores(units[n + 1]) if n + 1 < len(units) else None)
        mb = [si.max(axis=0, keepdims=True) for si in s]
        m_new = [mbi if m[h] is None else jnp.maximum(m[h], mbi) for h, mbi in zip(grp, mb)]
        p = [jnp.exp2(si - mn).astype(BF16) for si, mn in zip(s, m_new)]
        pv = [_dot(vt_ref[0, h * B_VA:(h + 1) * B_VA, ks], pi) for h, pi in zip(grp, p)]
        for h, mn, pvi in zip(grp, m_new, pv):
            acc[h] = pvi if m[h] is None else jnp.exp2(m[h] - mn) * acc[h] + pvi
            m[h] = mn
    o = jnp.concatenate([a[:B_V] / a[B_V:B_V + 1] for a in acc], axis=0).T
    return (o * szb_ref[0].astype(F32)).astype(BF16)


def _attn_out_kernel(*refs, n_src, final):
    q_ref = refs[0]
    kv_refs = refs[1:1 + 2 * n_src]
    (szb_ref, x_ref, of_ref, ob_ref, sza_ref, cx_ref, gng_ref, mod_ref, w_ref, fg_ref, o_ref) = refs[1 + 2 * n_src:]
    bx = _attention_tile(q_ref, kv_refs, szb_ref)
    o = of_ref[0].astype(F32) + ob_ref[0].astype(F32)
    sza = sza_ref[0].astype(F32)
    acc = None
    for h in range(A_HEADS):
        sl = slice(h * LANE, (h + 1) * LANE)
        oh = o[:, sl]
        ah = oh * jax.lax.rsqrt(jnp.mean(oh * oh, axis=-1, keepdims=True) + EPS) * gng_ref[...]
        term = _dot((ah * sza[:, sl]).astype(BF16), w_ref[sl, :])
        acc = term if acc is None else acc + term
    acc = acc + _dot(bx, w_ref[A_W:A_W + B_W, :]) + _dot(cx_ref[0], w_ref[A_W + B_W:, :])
    xn = x_ref[0] + mod_ref[0, 2:3, :] * acc
    if final:
        xn = xn * jax.lax.rsqrt(jnp.mean(xn * xn, axis=-1, keepdims=True) + EPS) * fg_ref[...]
    o_ref[0] = xn


def _attn_out(x, q, kvs, szb, of, ob, sza, cx, gng, mod, per_batch, layer, w, fg, final, tm):
    b, t, d = x.shape
    tm = min(tm, t)
    bm = (lambda i, j: (layer, i, 0, 0)) if per_batch else (lambda i, j: (layer, mod.shape[1] - 1, 0, 0))
    tok = lambda width: pl.BlockSpec((1, tm, width), lambda i, j: (i, j, 0))
    full = lambda a: pl.BlockSpec(a.shape, lambda i, j: (0,) * a.ndim)
    in_specs = [tok(B_HEADS * LANE)]
    args = [q]
    for k, v in kvs:
        in_specs.append(pl.BlockSpec((1,) + k.shape[1:], lambda i, j: (i, 0, 0)))
        in_specs.append(pl.BlockSpec((1,) + v.shape[1:], lambda i, j: (i, 0, 0)))
        args += [k, v]
    in_specs += [tok(B_W), tok(d), tok(A_W), tok(A_W), tok(A_W), tok(C_W), _layer_spec(gng, layer),
                 pl.BlockSpec((None, 1, 3, d), bm), _layer_spec(w, layer), full(fg)]
    args += [szb, x, of, ob, sza, cx, gng, mod, w, fg]
    return pl.pallas_call(
        functools.partial(_attn_out_kernel, n_src=len(kvs), final=final),
        grid=(b, t // tm),
        in_specs=in_specs,
        out_specs=tok(d),
        out_shape=jax.ShapeDtypeStruct((b, t, d), F32),
        compiler_params=_params(("parallel", "parallel")),
        name="attn_out",
    )(*args)


def _deinterleave(w32):
    even, odd = w32[..., 0::2], w32[..., 1::2]
    return jnp.concatenate([even, odd], axis=-1), jnp.concatenate([odd, even], axis=-1)


def _w_layout_kernel(w_ref, pm_ref, ps_ref, head_ref, tail_ref):
    w = w_ref[0]
    head_ref[0] = w[:, :W_HEAD].astype(BF16)
    o_b, o_qa, o_kv, o_zb, o_end = 2048, 2064, 2256, 2416, 3696
    kpe = w[:, o_kv + B_KV_LORA:o_zb].astype(BF16)
    kp_main = _dot(kpe, pm_ref[...])
    kp_swap = _dot(kpe, ps_ref[...])
    z = lambda n: jnp.zeros((w.shape[0], n), F32)
    cols = [w[:, o_zb:o_end],
            w[:, o_qa:o_kv], kp_swap, z(2 * LANE - B_Q_LORA - B_ROPE),
            w[:, o_kv:o_kv + B_KV_LORA], w[:, o_b:o_qa], z(B_NOPE - 4 * A_HEADS), kp_main,
            z(LANE - B_NOPE - B_ROPE)]
    tail_ref[0] = jnp.concatenate(cols, axis=-1).astype(BF16)


def _layout_w_in(w):
    depth, d, n_in = w.shape
    tk = 256
    src_main = np.concatenate([np.arange(0, B_ROPE, 2), np.arange(1, B_ROPE, 2)])
    src_swap = np.concatenate([np.arange(1, B_ROPE, 2), np.arange(0, B_ROPE, 2)])
    sel = lambda src: jnp.asarray(np.eye(B_ROPE, dtype=np.float32)[:, src], dtype=BF16)
    return pl.pallas_call(
        _w_layout_kernel,
        grid=(depth, d // tk),
        in_specs=[pl.BlockSpec((1, tk, n_in), lambda l, i: (l, i, 0)),
                  pl.BlockSpec((B_ROPE, B_ROPE), lambda l, i: (0, 0)),
                  pl.BlockSpec((B_ROPE, B_ROPE), lambda l, i: (0, 0))],
        out_specs=[pl.BlockSpec((1, tk, W_HEAD), lambda l, i: (l, i, 0)),
                   pl.BlockSpec((1, tk, W_TAIL), lambda l, i: (l, i, 0))],
        out_shape=[jax.ShapeDtypeStruct((depth, d, W_HEAD), BF16), jax.ShapeDtypeStruct((depth, d, W_TAIL), BF16)],
        compiler_params=_params(("parallel", "parallel")),
        name="w_layout",
    )(w, sel(src_main), sel(src_swap))


def _layout_w_qb(w):
    w = w.astype(BF16)
    z = lambda n: jnp.zeros(w.shape[:-1] + (n,), w.dtype)
    hd = B_NOPE + B_ROPE
    main, swap = [], []
    for h in range(B_HEADS):
        rm, rs = _deinterleave(w[..., h * hd + B_NOPE:(h + 1) * hd])
        main += [w[..., h * hd:h * hd + B_NOPE], rm, z(LANE - hd)]
        swap += [z(B_NOPE), rs, z(LANE - hd)]
    return jnp.concatenate(main, axis=-1), jnp.concatenate(swap, axis=-1)


def _layout_w_kvb(w):
    w = w.astype(BF16)
    hd = B_NOPE + B_V
    wk, wv = [], []
    for h in range(B_HEADS):
        wk += [w[..., h * hd:h * hd + B_NOPE], jnp.zeros(w.shape[:-1] + (LANE - B_NOPE,), w.dtype)]
        wv += [w[..., h * hd + B_NOPE:(h + 1) * hd], jnp.zeros(w.shape[:-1] + (B_VA - B_V,), w.dtype)]
    return jnp.concatenate(wk, axis=-1), jnp.swapaxes(jnp.concatenate(wv, axis=-1), -1, -2)


def _rope_tables(t, rotate):
    half = B_ROPE // 2
    if rotate:
        pos = np.arange(t)
        n_freq = B_ROPE // 4
        inv_freq = ROPE_THETA ** (-np.arange(n_freq, dtype=np.float64) / n_freq)
        ang = np.concatenate([(pos // GRID_W)[:, None] * inv_freq, (pos % GRID_W)[:, None] * inv_freq], axis=-1)
        cos, sin = np.cos(ang).astype(np.float32), np.sin(ang).astype(np.float32)
    else:
        cos, sin = np.ones((t, half), np.float32), np.zeros((t, half), np.float32)
    z = lambda n: np.zeros((t, n), np.float32)
    tail = LANE - B_NOPE - B_ROPE
    scale = np.float32((B_NOPE + B_ROPE) ** -0.5 * math.log2(math.e))
    cq = np.concatenate([np.ones((t, B_NOPE), np.float32), cos, cos, z(tail)], axis=1) * scale
    sq = np.concatenate([z(B_NOPE), -sin, sin, z(tail)], axis=1) * scale
    ck = np.concatenate([z(B_NOPE), cos, cos, z(tail)], axis=1)
    sk = np.concatenate([z(B_NOPE), -sin, sin, z(tail)], axis=1)
    return jnp.asarray(np.concatenate([cq, sq, ck, sk], axis=1), dtype=F32)


def kernel(x, c, ctx, c_ctx, w_ada, b_ada, norm_g, w_in, gdn_conv, gdn_a_log, gdn_dt_bias, gdn_norm_g,
           mla_q_norm_g, mla_w_qb, mla_kv_norm_g, mla_w_kvb, conv_w, w_out, final_norm_g):
    bsz, t, d = x.shape
    t_ctx = ctx.shape[1]
    depth = w_ada.shape[0]
    assert t % GDN_CHUNK == 0 and t_ctx % GDN_CHUNK == 0 and t % GRID_W == 0
    tm = 512
    bb = 2 if bsz % 2 == 0 else 1

    rows = -(-(bsz + 1) // 8) * 8
    cc = jnp.concatenate([c, jnp.zeros((rows - bsz - 1, d), F32), c_ctx[None, :]], axis=0)
    mod = _ada(cc, w_ada, b_ada).reshape(depth, rows, 3, d)

    tab_x = _rope_tables(t, True)
    tab_c = _rope_tables(t_ctx, False)
    fg = final_norm_g.reshape(1, d)
    w_all, wt_all = _layout_w_in(w_in)
    wqm_all, wqs_all = _layout_w_qb(mla_w_qb)
    wk_all, wv_all = _layout_w_kvb(mla_w_kvb)
    wo_all = w_out.astype(BF16)
    ng = norm_g.reshape(depth, 1, d)
    qng = mla_q_norm_g.reshape(depth, 1, B_Q_LORA)
    kvng = mla_kv_norm_g.reshape(depth, 1, B_KV_LORA)
    gng = gdn_norm_g.reshape(depth, 1, A_DV)
    gsc = jnp.pad(jnp.stack([gdn_a_log.reshape(depth, -1), gdn_dt_bias.reshape(depth, -1)], axis=1),
                  ((0, 0), (0, 6), (2 * A_HEADS, LANE - 4 * A_HEADS)))
    tap_pad = (A_CONV - C_CONV) // 2
    wconv = jnp.pad(jnp.concatenate([gdn_conv, jnp.pad(conv_w, ((0, 0), (tap_pad, tap_pad), (0, 0)))], axis=2),
                    ((0, 0), (0, 8 - A_CONV), (0, 0)))
    s_zero = jnp.zeros((bsz, 2, A_HEADS, A_DK, A_DV), F32)

    for l in range(depth):
        last = l == depth - 1
        shared = (l, ng, w_all, wt_all, qng, wqm_all, wqs_all, kvng, wk_all, wv_all)
        if last:
            qn_c, kn_c, vv_c, gates_c, km_c, vm_c = _inproj(
                ctx, mod, False, *shared, tab_c, gsc, wconv, tm, mixer_outputs=False)
        else:
            (qn_c, kn_c, vv_c, cx_c, sza_c, gates_c, qm_c, km_c, vm_c, szb_c) = _inproj(
                ctx, mod, False, *shared, tab_c, gsc, wconv, tm)
        (qn_x, kn_x, vv_x, cx_x, sza_x, gates_x, qm_x, km_x, vm_x, szb_x) = _inproj(
            x, mod, True, *shared, tab_x, gsc, wconv, tm)

        if last:
            (s_ctx,) = _scan(qn_c, kn_c, vv_c, gates_c, s_zero, bb, state_only=True)
        else:
            of_c, ob_c, s_ctx = _scan(qn_c, kn_c, vv_c, gates_c, s_zero, bb)
        of_x, ob_x, _ = _scan(qn_x, kn_x, vv_x, gates_x, s_ctx, bb)

        x_new = _attn_out(x, qm_x, [(km_x, vm_x), (km_c, vm_c)], szb_x, of_x, ob_x, sza_x, cx_x, gng, mod, True,
                          l, wo_all, fg, last, tm)
        if not last:
            ctx = _attn_out(ctx, qm_c, [(km_c, vm_c)], szb_c, of_c, ob_c, sza_c, cx_c, gng, mod, False,
                            l, wo_all, fg, False, tm)
        x = x_new
    return x
```

```python
import functools
import math

import jax
import jax.numpy as jnp
import numpy as np
from jax.experimental import pallas as pl
from jax.experimental.pallas import tpu as pltpu

F32 = jnp.float32
BF16 = jnp.bfloat16

EPS = 1e-6
GRID_W = 64
ROPE_THETA = 10000.0

A_HEADS = 4
A_DK = 128
A_DV = 128
A_QK = A_HEADS * A_DK
A_W = A_HEADS * A_DV
A_CONV = 5
B_HEADS = 4
B_NOPE = 64
B_ROPE = 32
B_V = 64
B_Q_LORA = 192
B_KV_LORA = 128
B_W = B_HEADS * B_V
B_VA = B_V + 16
B_WA = B_HEADS * B_VA
C_W = 256
C_CONV = 3

LANE = 128
HALO = 16
GDN_CHUNK = 128
ATTN_KEY_BLOCK = 256
ATTN_HEAD_GROUP = 4
VMEM_LIMIT = 48 * 1024 * 1024

W_HEAD = 2048
S_ZA = 1536
T_ZB = 0
T_H = 256
T_BG = 512
T_CG = 768
T_ZC = 1024
T_QA = 1280
T_CKV = 1536
W_TAIL = 1792
CIN_W = 3 * A_QK + C_W


def _dot(a, b):
    return jnp.dot(a, b, preferred_element_type=F32)


def _dot_nt(a, b):
    return jax.lax.dot_general(a, b, (((1,), (1,)), ((), ())), preferred_element_type=F32)


def _silu(x):
    h = 0.5 * x
    return h + h * jnp.tanh(h)


def _params(sem):
    return pltpu.CompilerParams(dimension_semantics=sem, vmem_limit_bytes=VMEM_LIMIT)


def _ada_kernel(c_ref, w_ref, b_ref, o_ref):
    s = _silu(c_ref[...]).astype(BF16)
    o_ref[0] = _dot(s, w_ref[0].astype(BF16)) + b_ref[0]


def _ada(cc, w_ada, b_ada):
    depth, d, n3 = w_ada.shape
    r = cc.shape[0]
    tn = 768
    return pl.pallas_call(
        _ada_kernel,
        grid=(depth, n3 // tn),
        in_specs=[pl.BlockSpec((r, d), lambda l, j: (0, 0)),
                  pl.BlockSpec((1, d, tn), lambda l, j: (l, 0, j)),
                  pl.BlockSpec((1, 1, tn), lambda l, j: (l, 0, j))],
        out_specs=pl.BlockSpec((1, r, tn), lambda l, j: (l, 0, j)),
        out_shape=jax.ShapeDtypeStruct((depth, r, n3), F32),
        compiler_params=_params(("parallel", "parallel")),
        name="ada",
    )(cc, w_ada, b_ada.reshape(depth, 1, n3))


def _inproj_kernel(x_ref, xp_ref, xn_ref, mod_ref, ng_ref, w_ref, wt_ref, qng_ref, wqm_ref, wqs_ref, kvng_ref,
                   wk_ref, wv_ref, tab_ref, gsc_ref, wc_ref, *out_refs, tm, mixer_outputs):
    if mixer_outputs:
        qn_ref, kn_ref, vv_ref, cx_ref, sza_ref, gates_ref, qm_ref, km_ref, vm_ref, szb_ref = out_refs
    else:
        qn_ref, kn_ref, vv_ref, gates_ref, km_ref, vm_ref = out_refs
    j = pl.program_id(1)
    nj = pl.num_programs(1)

    def normed(xv):
        y = xv * jax.lax.rsqrt(jnp.mean(xv * xv, axis=-1, keepdims=True) + EPS) * ng_ref[...]
        return y * (1.0 + mod_ref[0, 1:2, :]) + mod_ref[0, 0:1, :]

    hn = normed(x_ref[0]).astype(BF16)
    h_prev = jnp.where(j > 0, normed(xp_ref[0]), 0.0).astype(BF16)
    h_next = jnp.where(j < nj - 1, normed(xn_ref[0]), 0.0).astype(BF16)
    h_ext = jnp.concatenate([h_prev, hn, h_next], axis=0)

    def tail(a, b):
        return _dot(hn, wt_ref[:, a:b])

    def conv(u_ext, c0, c1, taps=A_CONV):
        pad = A_CONV // 2
        rows = u_ext.shape[0]
        acc = None
        for k in range(pad - taps // 2, pad + taps // 2 + 1):
            shifted = u_ext if k == pad else pltpu.roll(u_ext, (pad - k) % rows, 0)
            term = shifted[HALO:HALO + tm, :] * wc_ref[k:k + 1, c0:c1]
            acc = term if acc is None else acc + term
        return acc

    blk = 2 * LANE
    for i in range(3 * A_QK // blk):
        y2 = _silu(conv(_dot(h_ext, w_ref[:, i * blk:(i + 1) * blk]), i * blk, (i + 1) * blk))
        for hh in range(2):
            h = 2 * i + hh
            y = y2[:, hh * LANE:(hh + 1) * LANE]
            if h < 2 * A_HEADS:
                inv = jax.lax.rsqrt(jnp.sum(y * y, axis=-1, keepdims=True) + EPS)
                y = y * (inv * (A_DK ** -0.5) if h < A_HEADS else inv)
            if h < A_HEADS:
                qn_ref[0, :, h * LANE:(h + 1) * LANE] = y.astype(BF16)
            elif h < 2 * A_HEADS:
                kn_ref[0, :, (h - A_HEADS) * LANE:(h - A_HEADS + 1) * LANE] = y.astype(BF16)
            else:
                vv_ref[0, :, (h - 2 * A_HEADS) * LANE:(h - 2 * A_HEADS + 1) * LANE] = y.astype(BF16)

    tab = tab_ref[...]
    cq, sq = tab[:, 0:LANE], tab[:, LANE:2 * LANE]
    ck, sk = tab[:, 2 * LANE:3 * LANE], tab[:, 3 * LANE:4 * LANE]
    qa_blk = tail(T_QA, T_CKV)

    if mixer_outputs:
        hc_ext = _dot(h_ext, wt_ref[:, T_CG:T_ZC]) * _dot(h_ext, wt_ref[:, T_H:T_BG])
        cx_ref[0] = (tail(T_BG, T_CG) * _silu(tail(T_ZC, T_QA))
                     * conv(hc_ext, 3 * A_QK, CIN_W, C_CONV)).astype(BF16)

        sza_ref[0] = _silu(_dot(hn, w_ref[:, S_ZA:W_HEAD])).astype(BF16)
        szb_ref[0] = _silu(tail(T_ZB, T_H)).astype(BF16)

        qa = qa_blk[:, :B_Q_LORA]
        nq = (qa * jax.lax.rsqrt(jnp.mean(qa * qa, axis=-1, keepdims=True) + EPS) * qng_ref[...]).astype(BF16)
        qmain = _dot(nq, wqm_ref[...])
        qswap = _dot(nq, wqs_ref[...])
        for h in range(B_HEADS):
            sl = slice(h * LANE, (h + 1) * LANE)
            qm_ref[0, :, sl] = (qmain[:, sl] * cq + qswap[:, sl] * sq).astype(BF16)

    ckv_blk = tail(T_CKV, W_TAIL)
    ckv = ckv_blk[:, :B_KV_LORA]
    nkv =(ckv * jax.lax.rsqrt(jnp.mean(ckv * ckv, axis=-1, keepdims=True) + EPS) * kvng_ref[...]).astype(BF16)
    kk = _dot(nkv, wk_ref[...])
    vt = _dot_nt(wv_ref[...], nkv)
    vrow = jax.lax.broadcasted_iota(jnp.int32, vt.shape, 0)
    ones_row = functools.reduce(jnp.logical_or, [vrow == h * B_VA + B_V for h in range(B_HEADS)])
    vm_ref[0] = jnp.where(ones_row, 1.0, vt).astype(BF16)
    kpm = ckv_blk[:, B_KV_LORA:]
    kpe = kpm * ck + qa_blk[:, LANE:] * sk
    for h in range(B_HEADS):
        sl = slice(h * LANE, (h + 1) * LANE)
        km_ref[0, :, sl] = (kk[:, sl] + kpe).astype(BF16)

    lane = jax.lax.broadcasted_iota(jnp.int32, kpm.shape, 1)
    z = kpm + gsc_ref[1:2, :]
    sp = jnp.maximum(z, 0.0) + jnp.log1p(jnp.exp(-jnp.abs(z)))
    g = -jnp.exp(gsc_ref[0:1, :]) * sp
    gates_ref[0] = jnp.where(lane < 2 * A_HEADS, jax.nn.sigmoid(kpm), jnp.where(lane < 4 * A_HEADS, g, 0.0))


def _layer_spec(a, layer):
    return pl.BlockSpec((None,) + a.shape[1:], lambda i, j: (layer,) + (0,) * (a.ndim - 1))


def _inproj(x, mod, per_batch, layer, ng, w, wt, qng, wqm, wqs, kvng, wk, wv, tab, gsc, wconv, tm,
            mixer_outputs=True):
    b, t, d = x.shape
    tm = min(tm, t)
    r = tm // HALO
    nh = t // HALO
    bm = (lambda i, j: (layer, i, 0, 0)) if per_batch else (lambda i, j: (layer, mod.shape[1] - 1, 0, 0))
    tok = lambda width: pl.BlockSpec((1, tm, width), lambda i, j: (i, j, 0))
    full = lambda a: pl.BlockSpec(a.shape, lambda i, j: (0,) * a.ndim)
    lay = lambda a: _layer_spec(a, layer)
    outs = [(A_QK, BF16, True), (A_QK, BF16, True), (A_W, BF16, True), (C_W, BF16, False), (A_W, BF16, False),
            (LANE, F32, True), (B_HEADS * LANE, BF16, False), (B_HEADS * LANE, BF16, True), (None, BF16, True),
            (B_W, BF16, False)]
    outs = [o for o in outs if mixer_outputs or o[2]]
    widths = [o[0] for o in outs]
    dtypes = [o[1] for o in outs]
    V_OUT = widths.index(None)
    return pl.pallas_call(
        functools.partial(_inproj_kernel, tm=tm, mixer_outputs=mixer_outputs),
        grid=(b, t // tm),
        in_specs=[tok(d),
                  pl.BlockSpec((1, HALO, d), lambda i, j: (i, jnp.maximum(j * r - 1, 0), 0)),
                  pl.BlockSpec((1, HALO, d), lambda i, j: (i, jnp.minimum((j + 1) * r, nh - 1), 0)),
                  pl.BlockSpec((None, 1, 3, d), bm), lay(ng),
                  pl.BlockSpec((None, d, W_HEAD), lambda i, j: (layer, 0, 0)), lay(wt), lay(qng), lay(wqm), lay(wqs),
                  lay(kvng), lay(wk), lay(wv),
                  pl.BlockSpec((tm, 4 * LANE), lambda i, j: (j, 0)), lay(gsc), lay(wconv)],
        out_specs=[pl.BlockSpec((1, B_WA, tm), lambda i, j: (i, 0, j)) if n == V_OUT else tok(wd)
                   for n, wd in enumerate(widths)],
        out_shape=[jax.ShapeDtypeStruct((b, B_WA, t) if n == V_OUT else (b, t, wd), dt)
                   for n, (wd, dt) in enumerate(zip(widths, dtypes))],
        compiler_params=_params(("parallel", "parallel")),
        name="inproj",
    )(x, x, x, mod, ng, w, wt, qng, wqm, wqs, kvng, wk, wv, tab, gsc, wconv)


INV_BASE_LOG2 = 3
def _unit_triangular_inverses(mats, c):
    row = jax.lax.broadcasted_iota(jnp.int32, (c, c), 0)
    col = jax.lax.broadcasted_iota(jnp.int32, (c, c), 1)
    same = lambda k: (row >> k) == (col >> k)
    eye = jnp.where(row == col, 1.0, 0.0)
    base = same(INV_BASE_LOG2)
    a0 = [jnp.where(base, a, 0.0) for a in mats]
    t = [eye - ai for ai in a0]
    apow = [ai.astype(BF16) for ai in a0]
    for _ in range(INV_BASE_LOG2 - 1):
        apow = [_dot(ap, ap).astype(BF16) for ap in apow]
        t = [ti + _dot(ti.astype(BF16), ap) for ti, ap in zip(t, apow)]
    for k in range(INV_BASE_LOG2, int(math.log2(c))):
        coupling = same(k + 1) & jnp.logical_not(same(k))
        e = [jnp.where(coupling, a, 0.0).astype(BF16) for a in mats]
        tb = [ti.astype(BF16) for ti in t]
        te = [_dot(tbi, ei).astype(BF16) for tbi, ei in zip(tb, e)]
        t = [ti - _dot(tei, tbi) for ti, tei, tbi in zip(t, te, tb)]
    return t


def _scan_kernel(qf_ref, kf_ref, vf_ref, gf_ref, qb_ref, kb_ref, vb_ref, gb_ref, s0_ref,
                 *out_refs, c, bb):
    s_ref = out_refs[-1]
    o_refs = out_refs[:-1]
    n = pl.program_id(1)

    @pl.when(n == 0)
    def _():
        s_ref[...] = s0_ref[...]

    row = jax.lax.broadcasted_iota(jnp.int32, (c, c), 0)
    col = jax.lax.broadcasted_iota(jnp.int32, (c, c), 1)
    dirs = ((qf_ref, kf_ref, vf_ref, gf_ref), (qb_ref, kb_ref, vb_ref, gb_ref))
    incl = (col <= row, col >= row)
    strict = (col < row, col > row)
    gates, gcb, gcb_t, gl_row = {}, {}, {}, {}
    for e in range(bb):
        for d in range(2):
            g = dirs[d][3][e]
            tri = jnp.where(incl[d], 1.0, 0.0).astype(BF16)
            hi = g.astype(BF16)
            r1 = g - hi.astype(F32)
            mid = r1.astype(BF16)
            lo = (r1 - mid.astype(F32)).astype(BF16)
            gc_all = _dot(tri, hi) + _dot(tri, mid) + _dot(tri, lo)
            gates[e, d] = g
            gcb[e, d] = gc_all
            gcb_t[e, d] = gc_all.T
            gl_row[e, d] = gc_all[c - 1:c, :] if d == 0 else gc_all[0:1, :]

    chains = [(e, d, h) for e in range(bb) for d in range(2) for h in range(A_HEADS)]
    sls = [slice(h * LANE, (h + 1) * LANE) for _, _, h in chains]
    jbs = [A_HEADS * d + h for _, d, h in chains]
    jgs = [2 * A_HEADS + jb for jb in jbs]
    beta = [gates[e, d][:, jb:jb + 1] for (e, d, _), jb in zip(chains, jbs)]
    gc = [gcb[e, d][:, jg:jg + 1] for (e, d, _), jg in zip(chains, jgs)]
    gcr = [gcb_t[e, d][jg:jg + 1, :] for (e, d, _), jg in zip(chains, jgs)]
    gl = [gl_row[e, d][:, jg:jg + 1] for (e, d, _), jg in zip(chains, jgs)]
    q = [dirs[d][0][e, :, sl] for (e, d, _), sl in zip(chains, sls)]
    k = [dirs[d][1][e, :, sl] for (e, d, _), sl in zip(chains, sls)]
    v = [dirs[d][2][e, :, sl] for (e, d, _), sl in zip(chains, sls)]
    decay = [jnp.where(incl[d], jnp.exp(jnp.where(incl[d], gci - gcri, 0.0)), 0.0)
             for (_, d, _), gci, gcri in zip(chains, gc, gcr)]
    kbeta = [ki.astype(F32) * bi for ki, bi in zip(k, beta)]
    egc = [jnp.exp(gci) for gci in gc]
    a = [jnp.where(strict[d], _dot_nt(kbi.astype(BF16), ki) * di, 0.0)
         for (_, d, _), kbi, ki, di in zip(chains, kbeta, k, decay)]
    qk = [(_dot_nt(qi, ki) * di).astype(BF16) for qi, ki, di in zip(q, k, decay)]
    tinv = [ti.astype(BF16) for ti in _unit_triangular_inverses(a, c)]
    kd_t = [(ki.astype(F32) * jnp.exp(gli - gci)).T.astype(BF16) for ki, gli, gci in zip(k, gl, gc)]
    kbe = [(kbi * ei).astype(BF16) for kbi, ei in zip(kbeta, egc)]
    vbeta = [vi.astype(F32) * bi for vi, bi in zip(v, beta)]
    qd = [(qi.astype(F32) * ei).astype(BF16) for qi, ei in zip(q, egc)]
    s = [s_ref[e, d, h] for e, d, h in chains]
    sb = [si.astype(BF16) for si in s]
    rhs = [(vbi - _dot(kbi, sbi)).astype(BF16) for vbi, kbi, sbi in zip(vbeta, kbe, sb)]
    v_new = [_dot(ti, ri).astype(BF16) for ti, ri in zip(tinv, rhs)]
    if o_refs:
        o = [_dot(qdi, sbi) + _dot(qki, vi) for qdi, sbi, qki, vi in zip(qd, sb, qk, v_new)]
        for (e, d, h), sl, oi in zip(chains, sls, o):
            o_refs[d][e, :, sl] = oi.astype(BF16)
    for (e, d, h), gli, si, kdi, vi in zip(chains, gl, s, kd_t, v_new):
        s_ref[e, d, h] = jnp.exp(gli) * si + _dot(kdi, vi)


def _scan(qn, kn, vv, gates, s0, bb, state_only=False):
    b, t, _ = qn.shape
    c = GDN_CHUNK
    n = t // c
    assert b % bb == 0
    fwd = lambda width: pl.BlockSpec((bb, c, width), lambda i, j: (i, j, 0))
    bwd = lambda width: pl.BlockSpec((bb, c, width), lambda i, j: (i, n - 1 - j, 0))
    st = pl.BlockSpec((bb, 2, A_HEADS, A_DK, A_DV), lambda i, j: (i, 0, 0, 0, 0))
    o_specs = [] if state_only else [fwd(A_W), bwd(A_W)]
    o_shapes = [] if state_only else [jax.ShapeDtypeStruct((b, t, A_W), BF16)] * 2
    return pl.pallas_call(
        functools.partial(_scan_kernel, c=c, bb=bb),
        grid=(b // bb, n),
        in_specs=[fwd(A_QK), fwd(A_QK), fwd(A_W), fwd(LANE), bwd(A_QK), bwd(A_QK), bwd(A_W), bwd(LANE), st],
        out_specs=o_specs + [st],
        out_shape=o_shapes + [jax.ShapeDtypeStruct(s0.shape, F32)],
        compiler_params=_params(("parallel", "arbitrary")),
        name="gdn_scan",
    )(qn, kn, vv, gates, qn, kn, vv, gates, s0)


def _attention_tile(q_ref, kv_refs, szb_ref):
    n_src = len(kv_refs) // 2
    heads = range(B_HEADS)
    qh = [q_ref[0, :, h * LANE:(h + 1) * LANE] for h in heads]
    blocks = []
    for i in range(n_src):
        tk = kv_refs[2 * i].shape[1]
        kb = min(ATTN_KEY_BLOCK, tk)
        blocks += [(kv_refs[2 * i], kv_refs[2 * i + 1], slice(j * kb, (j + 1) * kb)) for j in range(tk // kb)]

    groups = [tuple(range(g, g + ATTN_HEAD_GROUP)) for g in range(0, B_HEADS, ATTN_HEAD_GROUP)]
    units = [(blk, grp) for blk in blocks for grp in groups]

    def scores(unit):
        (k_ref, _, ks), grp = unit
        return [_dot_nt(k_ref[0, ks, h * LANE:(h + 1) * LANE], qh[h]) for h in grp]

    m = [None] * B_HEADS
    acc = [None] * B_HEADS
    s_next = scores(units[0])
    for n, ((_, vt_ref, ks), grp) in enumerate(units):
        s, s_next = s_next, (scores(units[n + 1]) if n + 1 < len(units) else None)
        mb = [si.max(axis=0, keepdims=True) for si in s]
        m_new = [mbi if m[h] is None else jnp.maximum(m[h], mbi) for h, mbi in zip(grp, mb)]
        p = [jnp.exp2(si - mn).astype(BF16) for si, mn in zip(s, m_new)]
        pv = [_dot(vt_ref[0, h * B_VA:(h + 1) * B_VA, ks], pi) for h, pi in zip(grp, p)]
        for h, mn, pvi in zip(grp, m_new, pv):
            acc[h] = pvi if m[h] is None else jnp.exp2(m[h] - mn) * acc[h] + pvi
            m[h] = mn
    o = jnp.concatenate([a[:B_V] / a[B_V:B_V + 1] for a in acc], axis=0).T
    return (o * szb_ref[0].astype(F32)).astype(BF16)


def _attn_out_kernel(*refs, n_src, final):
    q_ref = refs[0]
    kv_refs = refs[1:1 + 2 * n_src]
    (szb_ref, x_ref, of_ref, ob_ref, sza_ref, cx_ref, gng_ref, mod_ref, w_ref, fg_ref, o_ref) = refs[1 + 2 * n_src:]
    bx = _attention_tile(q_ref, kv_refs, szb_ref)
    o = of_ref[0].astype(F32) + ob_ref[0].astype(F32)
    sza = sza_ref[0].astype(F32)
    acc = None
    for h in range(A_HEADS):
        sl = slice(h * LANE, (h + 1) * LANE)
        oh = o[:, sl]
        ah = oh * jax.lax.rsqrt(jnp.mean(oh * oh, axis=-1, keepdims=True) + EPS) * gng_ref[...]
        term = _dot((ah * sza[:, sl]).astype(BF16), w_ref[sl, :])
        acc = term if acc is None else acc + term
    acc = acc + _dot(bx, w_ref[A_W:A_W + B_W, :]) + _dot(cx_ref[0], w_ref[A_W + B_W:, :])
    xn = x_ref[0] + mod_ref[0, 2:3, :] * acc
    if final:
        xn = xn * jax.lax.rsqrt(jnp.mean(xn * xn, axis=-1, keepdims=True) + EPS) * fg_ref[...]
    o_ref[0] = xn


def _attn_out(x, q, kvs, szb, of, ob, sza, cx, gng, mod, per_batch, layer, w, fg, final, tm):
    b, t, d = x.shape
    tm = min(tm, t)
    bm = (lambda i, j: (layer, i, 0, 0)) if per_batch else (lambda i, j: (layer, mod.shape[1] - 1, 0, 0))
    tok = lambda width: pl.BlockSpec((1, tm, width), lambda i, j: (i, j, 0))
    full = lambda a: pl.BlockSpec(a.shape, lambda i, j: (0,) * a.ndim)
    in_specs = [tok(B_HEADS * LANE)]
    args = [q]
    for k, v in kvs:
        in_specs.append(pl.BlockSpec((1,) + k.shape[1:], lambda i, j: (i, 0, 0)))
        in_specs.append(pl.BlockSpec((1,) + v.shape[1:], lambda i, j: (i, 0, 0)))
        args += [k, v]
    in_specs += [tok(B_W), tok(d), tok(A_W), tok(A_W), tok(A_W), tok(C_W), _layer_spec(gng, layer),
                 pl.BlockSpec((None, 1, 3, d), bm), _layer_spec(w, layer), full(fg)]
    args += [szb, x, of, ob, sza, cx, gng, mod, w, fg]
    return pl.pallas_call(
        functools.partial(_attn_out_kernel, n_src=len(kvs), final=final),
        grid=(b, t // tm),
        in_specs=in_specs,
        out_specs=tok(d),
        out_shape=jax.ShapeDtypeStruct((b, t, d), F32),
        compiler_params=_params(("parallel", "parallel")),
        name="attn_out",
    )(*args)


def _deinterleave(w32):
    even, odd = w32[..., 0::2], w32[..., 1::2]
    return jnp.concatenate([even, odd], axis=-1), jnp.concatenate([odd, even], axis=-1)


def _w_layout_kernel(wt_ref, pm_ref, ps_ref, head_ref, tail_ref):
    wt = wt_ref[0]
    tk = wt.shape[1]
    head_ref[0] = wt[:W_HEAD].T.astype(BF16)
    o_b, o_qa, o_kv, o_zb, o_end = 2048, 2064, 2256, 2416, 3696
    kpe = wt[o_kv + B_KV_LORA:o_zb].astype(BF16)
    kp_main = _dot(pm_ref[...], kpe)
    kp_swap = _dot(ps_ref[...], kpe)
    z = lambda n: jnp.zeros((n, tk), F32)
    rows = [wt[o_zb:o_end],
            wt[o_qa:o_kv], kp_swap, z(2 * LANE - B_Q_LORA - B_ROPE),
            wt[o_kv:o_kv + B_KV_LORA], wt[o_b:o_qa], z(B_NOPE - 4 * A_HEADS), kp_main,
            z(LANE - B_NOPE - B_ROPE)]
    tail_ref[0] = jnp.concatenate(rows, axis=0).T.astype(BF16)


def _layout_w_in(w):
    depth, d, n_in = w.shape
    tk = 256
    src_main = np.concatenate([np.arange(0, B_ROPE, 2), np.arange(1, B_ROPE, 2)])
    src_swap = np.concatenate([np.arange(1, B_ROPE, 2), np.arange(0, B_ROPE, 2)])
    sel = lambda src: jnp.asarray(np.eye(B_ROPE, dtype=np.float32)[src], dtype=BF16)
    return pl.pallas_call(
        _w_layout_kernel,
        grid=(depth, d // tk),
        in_specs=[pl.BlockSpec((1, n_in, tk), lambda l, i: (l, 0, i)),
                  pl.BlockSpec((B_ROPE, B_ROPE), lambda l, i: (0, 0)),
                  pl.BlockSpec((B_ROPE, B_ROPE), lambda l, i: (0, 0))],
        out_specs=[pl.BlockSpec((1, tk, W_HEAD), lambda l, i: (l, i, 0)),
                   pl.BlockSpec((1, tk, W_TAIL), lambda l, i: (l, i, 0))],
        out_shape=[jax.ShapeDtypeStruct((depth, d, W_HEAD), BF16), jax.ShapeDtypeStruct((depth, d, W_TAIL), BF16)],
        compiler_params=_params(("parallel", "parallel")),
        name="w_layout",
    )(jnp.swapaxes(w, 1, 2), sel(src_main), sel(src_swap))


def _layout_w_qb(w):
    w = w.astype(BF16)
    z = lambda n: jnp.zeros(w.shape[:-1] + (n,), w.dtype)
    hd = B_NOPE + B_ROPE
    main, swap = [], []
    for h in range(B_HEADS):
        rm, rs = _deinterleave(w[..., h * hd + B_NOPE:(h + 1) * hd])
        main += [w[..., h * hd:h * hd + B_NOPE], rm, z(LANE - hd)]
        swap += [z(B_NOPE), rs, z(LANE - hd)]
    return jnp.concatenate(main, axis=-1), jnp.concatenate(swap, axis=-1)


def _layout_w_kvb(w):
    w = w.astype(BF16)
    hd = B_NOPE + B_V
    wk, wv = [], []
    for h in range(B_HEADS):
        wk += [w[..., h * hd:h * hd + B_NOPE], jnp.zeros(w.shape[:-1] + (LANE - B_NOPE,), w.dtype)]
        wv += [w[..., h * hd + B_NOPE:(h + 1) * hd], jnp.zeros(w.shape[:-1] + (B_VA - B_V,), w.dtype)]
    return jnp.concatenate(wk, axis=-1), jnp.swapaxes(jnp.concatenate(wv, axis=-1), -1, -2)


def _rope_tables(t, rotate):
    half = B_ROPE // 2
    if rotate:
        pos = np.arange(t)
        n_freq = B_ROPE // 4
        inv_freq = ROPE_THETA ** (-np.arange(n_freq, dtype=np.float64) / n_freq)
        ang = np.concatenate([(pos // GRID_W)[:, None] * inv_freq, (pos % GRID_W)[:, None] * inv_freq], axis=-1)
        cos, sin = np.cos(ang).astype(np.float32), np.sin(ang).astype(np.float32)
    else:
        cos, sin = np.ones((t, half), np.float32), np.zeros((t, half), np.float32)
    z = lambda n: np.zeros((t, n), np.float32)
    tail = LANE - B_NOPE - B_ROPE
    scale = np.float32((B_NOPE + B_ROPE) ** -0.5 * math.log2(math.e))
    cq = np.concatenate([np.ones((t, B_NOPE), np.float32), cos, cos, z(tail)], axis=1) * scale
    sq = np.concatenate([z(B_NOPE), -sin, sin, z(tail)], axis=1) * scale
    ck = np.concatenate([z(B_NOPE), cos, cos, z(tail)], axis=1)
    sk = np.concatenate([z(B_NOPE), -sin, sin, z(tail)], axis=1)
    return jnp.asarray(np.concatenate([cq, sq, ck, sk], axis=1), dtype=F32)


def kernel(x, c, ctx, c_ctx, w_ada, b_ada, norm_g, w_in, gdn_conv, gdn_a_log, gdn_dt_bias, gdn_norm_g,
           mla_q_norm_g, mla_w_qb, mla_kv_norm_g, mla_w_kvb, conv_w, w_out, final_norm_g):
    bsz, t, d = x.shape
    t_ctx = ctx.shape[1]
    depth = w_ada.shape[0]
    assert t % GDN_CHUNK == 0 and t_ctx % GDN_CHUNK == 0 and t % GRID_W == 0
    tm = 512
    bb = 2 if bsz % 2 == 0 else 1

    rows = -(-(bsz + 1) // 8) * 8
    cc = jnp.concatenate([c, jnp.zeros((rows - bsz - 1, d), F32), c_ctx[None, :]], axis=0)
    mod = _ada(cc, w_ada, b_ada).reshape(depth, rows, 3, d)

    tab_x = _rope_tables(t, True)
    tab_c = _rope_tables(t_ctx, False)
    fg = final_norm_g.reshape(1, d)
    w_all, wt_all = _layout_w_in(w_in)
    wqm_all, wqs_all = _layout_w_qb(mla_w_qb)
    wk_all, wv_all = _layout_w_kvb(mla_w_kvb)
    wo_all = w_out.astype(BF16)
    ng = norm_g.reshape(depth, 1, d)
    qng = mla_q_norm_g.reshape(depth, 1, B_Q_LORA)
    kvng = mla_kv_norm_g.reshape(depth, 1, B_KV_LORA)
    gng = gdn_norm_g.reshape(depth, 1, A_DV)
    gsc = jnp.pad(jnp.stack([gdn_a_log.reshape(depth, -1), gdn_dt_bias.reshape(depth, -1)], axis=1),
                  ((0, 0), (0, 6), (2 * A_HEADS, LANE - 4 * A_HEADS)))
    tap_pad = (A_CONV - C_CONV) // 2
    wconv = jnp.pad(jnp.concatenate([gdn_conv, jnp.pad(conv_w, ((0, 0), (tap_pad, tap_pad), (0, 0)))], axis=2),
                    ((0, 0), (0, 8 - A_CONV), (0, 0)))
    s_zero = jnp.zeros((bsz, 2, A_HEADS, A_DK, A_DV), F32)

    for l in range(depth):
        last = l == depth - 1
        shared = (l, ng, w_all, wt_all, qng, wqm_all, wqs_all, kvng, wk_all, wv_all)
        if last:
            qn_c, kn_c, vv_c, gates_c, km_c, vm_c = _inproj(
                ctx, mod, False, *shared, tab_c, gsc, wconv, tm, mixer_outputs=False)
        else:
            (qn_c, kn_c, vv_c, cx_c, sza_c, gates_c, qm_c, km_c, vm_c, szb_c) = _inproj(
                ctx, mod, False, *shared, tab_c, gsc, wconv, tm)
        (qn_x, kn_x, vv_x, cx_x, sza_x, gates_x, qm_x, km_x, vm_x, szb_x) = _inproj(
            x, mod, True, *shared, tab_x, gsc, wconv, tm)

        if last:
            (s_ctx,) = _scan(qn_c, kn_c, vv_c, gates_c, s_zero, bb, state_only=True)
        else:
            of_c, ob_c, s_ctx = _scan(qn_c, kn_c, vv_c, gates_c, s_zero, bb)
        of_x, ob_x, _ = _scan(qn_x, kn_x, vv_x, gates_x, s_ctx, bb)

        x_new = _attn_out(x, qm_x, [(km_x, vm_x), (km_c, vm_c)], szb_x, of_x, ob_x, sza_x, cx_x, gng, mod, True,
                          l, wo_all, fg, last, tm)
        if not last:
            ctx = _attn_out(ctx, qm_c, [(km_c, vm_c)], szb_c, of_c, ob_c, sza_c, cx_c, gng, mod, False,
                            l, wo_all, fg, False, tm)
        x = x_new
    return x
```

```python
import functools
import math

import jax
import jax.numpy as jnp
import numpy as np
from jax.experimental import pallas as pl
from jax.experimental.pallas import tpu as pltpu

F32 = jnp.float32
BF16 = jnp.bfloat16

EPS = 1e-6
GRID_W = 64
ROPE_THETA = 10000.0

A_HEADS = 4
A_DK = 128
A_DV = 128
A_QK = A_HEADS * A_DK
A_W = A_HEADS * A_DV
A_CONV = 5
B_HEADS = 4
B_NOPE = 64
B_ROPE = 32
B_V = 64
B_Q_LORA = 192
B_KV_LORA = 128
B_W = B_HEADS * B_V
B_VA = B_V + 16
B_WA = B_HEADS * B_VA
C_W = 256
C_CONV = 3

LANE = 128
HALO = 16
GDN_CHUNK = 128
ATTN_KEY_BLOCK = 256
ATTN_HEAD_GROUP = 4
VMEM_LIMIT = 48 * 1024 * 1024

O_B = 3 * A_QK + A_W
O_QA = O_B + 4 * A_HEADS
O_KV = O_QA + B_Q_LORA
O_ZB = O_KV + B_KV_LORA + B_ROPE
O_END = O_ZB + B_W + 4 * C_W

W_HEAD = O_B
S_ZA = 3 * A_QK
T_ZB = 0
T_H = T_ZB + B_W
T_BG = T_H + C_W
T_CG = T_BG + C_W
T_ZC = T_CG + C_W
T_QA = T_ZC + C_W
T_CKV = T_QA + 2 * LANE
W_TAIL = T_CKV + 2 * LANE
CIN_W = 3 * A_QK + C_W


def _dot(a, b):
    return jnp.dot(a, b, preferred_element_type=F32)


def _dot_nt(a, b):
    return jax.lax.dot_general(a, b, (((1,), (1,)), ((), ())), preferred_element_type=F32)


def _silu(x):
    h = 0.5 * x
    return h + h * jnp.tanh(h)


def _params(sem):
    return pltpu.CompilerParams(dimension_semantics=sem, vmem_limit_bytes=VMEM_LIMIT)


def _ada_kernel(c_ref, w_ref, b_ref, o_ref):
    s = _silu(c_ref[...]).astype(BF16)
    o_ref[0] = _dot(s, w_ref[0].astype(BF16)) + b_ref[0]


def _ada(cc, w_ada, b_ada):
    depth, d, n3 = w_ada.shape
    r = cc.shape[0]
    tn = 768
    return pl.pallas_call(
        _ada_kernel,
        grid=(depth, n3 // tn),
        in_specs=[pl.BlockSpec((r, d), lambda l, j: (0, 0)),
                  pl.BlockSpec((1, d, tn), lambda l, j: (l, 0, j)),
                  pl.BlockSpec((1, 1, tn), lambda l, j: (l, 0, j))],
        out_specs=pl.BlockSpec((1, r, tn), lambda l, j: (l, 0, j)),
        out_shape=jax.ShapeDtypeStruct((depth, r, n3), F32),
        compiler_params=_params(("parallel", "parallel")),
        name="ada",
    )(cc, w_ada, b_ada.reshape(depth, 1, n3))


def _inproj_kernel(x_ref, xp_ref, xn_ref, mod_ref, ng_ref, w_ref, wt_ref, qng_ref, wqm_ref, wqs_ref, kvng_ref,
                   wk_ref, wv_ref, tab_ref, gsc_ref, wc_ref, *out_refs, tm, mixer_outputs):
    if mixer_outputs:
        qn_ref, kn_ref, vv_ref, cx_ref, sza_ref, gates_ref, qm_ref, km_ref, vm_ref, szb_ref = out_refs
    else:
        qn_ref, kn_ref, vv_ref, gates_ref, km_ref, vm_ref = out_refs
    j = pl.program_id(1)
    nj = pl.num_programs(1)

    def normed(xv):
        y = xv * jax.lax.rsqrt(jnp.mean(xv * xv, axis=-1, keepdims=True) + EPS) * ng_ref[...]
        return y * (1.0 + mod_ref[0, 1:2, :]) + mod_ref[0, 0:1, :]

    hn = normed(x_ref[0]).astype(BF16)
    h_prev = jnp.where(j > 0, normed(xp_ref[0]), 0.0).astype(BF16)
    h_next = jnp.where(j < nj - 1, normed(xn_ref[0]), 0.0).astype(BF16)
    h_ext = jnp.concatenate([h_prev, hn, h_next], axis=0)

    def tail(a, b):
        return _dot(hn, wt_ref[:, a:b])

    def conv(u_ext, c0, c1, taps=A_CONV):
        pad = A_CONV // 2
        rows = u_ext.shape[0]
        acc = None
        for k in range(pad - taps // 2, pad + taps // 2 + 1):
            shifted = u_ext if k == pad else pltpu.roll(u_ext, (pad - k) % rows, 0)
            term = shifted[HALO:HALO + tm, :] * wc_ref[k:k + 1, c0:c1]
            acc = term if acc is None else acc + term
        return acc

    blk = 2 * LANE
    for i in range(3 * A_QK // blk):
        y2 = _silu(conv(_dot(h_ext, w_ref[:, i * blk:(i + 1) * blk]), i * blk, (i + 1) * blk))
        for hh in range(2):
            h = 2 * i + hh
            y = y2[:, hh * LANE:(hh + 1) * LANE]
            if h < 2 * A_HEADS:
                inv = jax.lax.rsqrt(jnp.sum(y * y, axis=-1, keepdims=True) + EPS)
                y = y * (inv * (A_DK ** -0.5) if h < A_HEADS else inv)
            if h < A_HEADS:
                qn_ref[0, :, h * LANE:(h + 1) * LANE] = y.astype(BF16)
            elif h < 2 * A_HEADS:
                kn_ref[0, :, (h - A_HEADS) * LANE:(h - A_HEADS + 1) * LANE] = y.astype(BF16)
            else:
                vv_ref[0, :, (h - 2 * A_HEADS) * LANE:(h - 2 * A_HEADS + 1) * LANE] = y.astype(BF16)

    tab = tab_ref[...]
    cq, sq = tab[:, 0:LANE], tab[:, LANE:2 * LANE]
    ck, sk = tab[:, 2 * LANE:3 * LANE], tab[:, 3 * LANE:4 * LANE]
    qa_blk = tail(T_QA, T_CKV)

    if mixer_outputs:
        hc_ext = _dot(h_ext, wt_ref[:, T_CG:T_ZC]) * _dot(h_ext, wt_ref[:, T_H:T_BG])
        cx_ref[0] = (tail(T_BG, T_CG) * _silu(tail(T_ZC, T_QA))
                     * conv(hc_ext, 3 * A_QK, CIN_W, C_CONV)).astype(BF16)

        sza_ref[0] = _silu(_dot(hn, w_ref[:, S_ZA:W_HEAD])).astype(BF16)
        szb_ref[0] = _silu(tail(T_ZB, T_H)).astype(BF16)

        qa = qa_blk[:, :B_Q_LORA]
        nq = (qa * jax.lax.rsqrt(jnp.mean(qa * qa, axis=-1, keepdims=True) + EPS) * qng_ref[...]).astype(BF16)
        qmain = _dot(nq, wqm_ref[...])
        qswap = _dot(nq, wqs_ref[...])
        for h in range(B_HEADS):
            sl = slice(h * LANE, (h + 1) * LANE)
            qm_ref[0, :, sl] = (qmain[:, sl] * cq + qswap[:, sl] * sq).astype(BF16)

    ckv_blk = tail(T_CKV, W_TAIL)
    ckv = ckv_blk[:, :B_KV_LORA]
    nkv = (ckv * jax.lax.rsqrt(jnp.mean(ckv * ckv, axis=-1, keepdims=True) + EPS) * kvng_ref[...]).astype(BF16)
    kk = _dot(nkv, wk_ref[...])
    vt = _dot_nt(wv_ref[...], nkv)
    vrow = jax.lax.broadcasted_iota(jnp.int32, vt.shape, 0)
    ones_row = functools.reduce(jnp.logical_or, [vrow == h * B_VA + B_V for h in range(B_HEADS)])
    vm_ref[0] = jnp.where(ones_row, 1.0, vt).astype(BF16)
    kpm = ckv_blk[:, B_KV_LORA:]
    kpe = kpm * ck + qa_blk[:, LANE:] * sk
    for h in range(B_HEADS):
        sl = slice(h * LANE, (h + 1) * LANE)
        km_ref[0, :, sl] = (kk[:, sl] + kpe).astype(BF16)

    lane = jax.lax.broadcasted_iota(jnp.int32, kpm.shape, 1)
    z = kpm + gsc_ref[1:2, :]
    sp = jnp.maximum(z, 0.0) + jnp.log1p(jnp.exp(-jnp.abs(z)))
    g = -jnp.exp(gsc_ref[0:1, :]) * sp
    gates_ref[0] = jnp.where(lane < 2 * A_HEADS, jax.nn.sigmoid(kpm), jnp.where(lane < 4 * A_HEADS, g, 0.0))


def _layer_spec(a, layer):
    return pl.BlockSpec((None,) + a.shape[1:], lambda i, j: (layer,) + (0,) * (a.ndim - 1))


def _inproj(x, mod, per_batch, layer, ng, w, wt, qng, wqm, wqs, kvng, wk, wv, tab, gsc, wconv, tm,
            mixer_outputs=True):
    b, t, d = x.shape
    tm = min(tm, t)
    r = tm // HALO
    nh = t // HALO
    bm = (lambda i, j: (layer, i, 0, 0)) if per_batch else (lambda i, j: (layer, mod.shape[1] - 1, 0, 0))
    tok = lambda width: pl.BlockSpec((1, tm, width), lambda i, j: (i, j, 0))
    full = lambda a: pl.BlockSpec(a.shape, lambda i, j: (0,) * a.ndim)
    lay = lambda a: _layer_spec(a, layer)
    outs = [(A_QK, BF16, True), (A_QK, BF16, True), (A_W, BF16, True), (C_W, BF16, False), (A_W, BF16, False),
            (LANE, F32, True), (B_HEADS * LANE, BF16, False), (B_HEADS * LANE, BF16, True), (None, BF16, True),
            (B_W, BF16, False)]
    outs = [o for o in outs if mixer_outputs or o[2]]
    widths = [o[0] for o in outs]
    dtypes = [o[1] for o in outs]
    V_OUT = widths.index(None)
    return pl.pallas_call(
        functools.partial(_inproj_kernel, tm=tm, mixer_outputs=mixer_outputs),
        grid=(b, t // tm),
        in_specs=[tok(d),
                  pl.BlockSpec((1, HALO, d), lambda i, j: (i, jnp.maximum(j * r - 1, 0), 0)),
                  pl.BlockSpec((1, HALO, d), lambda i, j: (i, jnp.minimum((j + 1) * r, nh - 1), 0)),
                  pl.BlockSpec((None, 1, 3, d), bm), lay(ng),
                  pl.BlockSpec((None, d, W_HEAD), lambda i, j: (layer, 0, 0)), lay(wt), lay(qng), lay(wqm), lay(wqs),
                  lay(kvng), lay(wk), lay(wv),
                  pl.BlockSpec((tm, 4 * LANE), lambda i, j: (j, 0)), lay(gsc), lay(wconv)],
        out_specs=[pl.BlockSpec((1, B_WA, tm), lambda i, j: (i, 0, j)) if n == V_OUT else tok(wd)
                   for n, wd in enumerate(widths)],
        out_shape=[jax.ShapeDtypeStruct((b, B_WA, t) if n == V_OUT else (b, t, wd), dt)
                   for n, (wd, dt) in enumerate(zip(widths, dtypes))],
        compiler_params=_params(("parallel", "parallel")),
        name="inproj",
    )(x, x, x, mod, ng, w, wt, qng, wqm, wqs, kvng, wk, wv, tab, gsc, wconv)


INV_BASE_LOG2 = 3
def _unit_triangular_inverses(mats, c):
    row = jax.lax.broadcasted_iota(jnp.int32, (c, c), 0)
    col = jax.lax.broadcasted_iota(jnp.int32, (c, c), 1)
    same = lambda k: (row >> k) == (col >> k)
    eye = jnp.where(row == col, 1.0, 0.0)
    base = same(INV_BASE_LOG2)
    a0 = [jnp.where(base, a, 0.0) for a in mats]
    t = [eye - ai for ai in a0]
    apow = [ai.astype(BF16) for ai in a0]
    for _ in range(INV_BASE_LOG2 - 1):
        apow = [_dot(ap, ap).astype(BF16) for ap in apow]
        t = [ti + _dot(ti.astype(BF16), ap) for ti, ap in zip(t, apow)]
    for k in range(INV_BASE_LOG2, int(math.log2(c))):
        coupling = same(k + 1) & jnp.logical_not(same(k))
        e = [jnp.where(coupling, a, 0.0).astype(BF16) for a in mats]
        tb = [ti.astype(BF16) for ti in t]
        te = [_dot(tbi, ei).astype(BF16) for tbi, ei in zip(tb, e)]
        t = [ti - _dot(tei, tbi) for ti, tei, tbi in zip(t, te, tb)]
    return t


def _scan_kernel(qf_ref, kf_ref, vf_ref, gf_ref, qb_ref, kb_ref, vb_ref, gb_ref, s0_ref,
                 *out_refs, c, bb):
    s_ref = out_refs[-1]
    o_refs = out_refs[:-1]
    n = pl.program_id(1)

    @pl.when(n == 0)
    def _():
        s_ref[...] = s0_ref[...]

    row = jax.lax.broadcasted_iota(jnp.int32, (c, c), 0)
    col = jax.lax.broadcasted_iota(jnp.int32, (c, c), 1)
    dirs = ((qf_ref, kf_ref, vf_ref, gf_ref), (qb_ref, kb_ref, vb_ref, gb_ref))
    incl = (col <= row, col >= row)
    strict = (col < row, col > row)
    gates, gcb, gcb_t, gl_row = {}, {}, {}, {}
    for e in range(bb):
        for d in range(2):
            g = dirs[d][3][e]
            tri = jnp.where(incl[d], 1.0, 0.0).astype(BF16)
            hi = g.astype(BF16)
            r1 = g - hi.astype(F32)
            mid = r1.astype(BF16)
            lo = (r1 - mid.astype(F32)).astype(BF16)
            gc_all = _dot(tri, hi) + _dot(tri, mid) + _dot(tri, lo)
            gates[e, d] = g
            gcb[e, d] = gc_all
            gcb_t[e, d] = gc_all.T
            gl_row[e, d] = gc_all[c - 1:c, :] if d == 0 else gc_all[0:1, :]

    chains = [(e, d, h) for e in range(bb) for d in range(2) for h in range(A_HEADS)]
    sls = [slice(h * LANE, (h + 1) * LANE) for _, _, h in chains]
    jbs = [A_HEADS * d + h for _, d, h in chains]
    jgs = [2 * A_HEADS + jb for jb in jbs]
    beta = [gates[e, d][:, jb:jb + 1] for (e, d, _), jb in zip(chains, jbs)]
    gc = [gcb[e, d][:, jg:jg + 1] for (e, d, _), jg in zip(chains, jgs)]
    gcr = [gcb_t[e, d][jg:jg + 1, :] for (e, d, _), jg in zip(chains, jgs)]
    gl = [gl_row[e, d][:, jg:jg + 1] for (e, d, _), jg in zip(chains, jgs)]
    q = [dirs[d][0][e, :, sl] for (e, d, _), sl in zip(chains, sls)]
    k = [dirs[d][1][e, :, sl] for (e, d, _), sl in zip(chains, sls)]
    v = [dirs[d][2][e, :, sl] for (e, d, _), sl in zip(chains, sls)]
    decay = [jnp.where(incl[d], jnp.exp(jnp.where(incl[d], gci - gcri, 0.0)), 0.0)
             for (_, d, _), gci, gcri in zip(chains, gc, gcr)]
    kbeta = [ki.astype(F32) * bi for ki, bi in zip(k, beta)]
    egc = [jnp.exp(gci) for gci in gc]
    a = [jnp.where(strict[d], _dot_nt(kbi.astype(BF16), ki) * di, 0.0)
         for (_, d, _), kbi, ki, di in zip(chains, kbeta, k, decay)]
    qk = [(_dot_nt(qi, ki) * di).astype(BF16) for qi, ki, di in zip(q, k, decay)]
    tinv = [ti.astype(BF16) for ti in _unit_triangular_inverses(a, c)]
    kd_t = [(ki.astype(F32) * jnp.exp(gli - gci)).T.astype(BF16) for ki, gli, gci in zip(k, gl, gc)]
    kbe = [(kbi * ei).astype(BF16) for kbi, ei in zip(kbeta, egc)]
    vbeta = [vi.astype(F32) * bi for vi, bi in zip(v, beta)]
    qd = [(qi.astype(F32) * ei).astype(BF16) for qi, ei in zip(q, egc)]
    s = [s_ref[e, d, h] for e, d, h in chains]
    sb = [si.astype(BF16) for si in s]
    rhs = [(vbi - _dot(kbi, sbi)).astype(BF16) for vbi, kbi, sbi in zip(vbeta, kbe, sb)]
    v_new = [_dot(ti, ri).astype(BF16) for ti, ri in zip(tinv, rhs)]
    if o_refs:
        o = [_dot(qdi, sbi) + _dot(qki, vi) for qdi, sbi, qki, vi in zip(qd, sb, qk, v_new)]
        for (e, d, h), sl, oi in zip(chains, sls, o):
            o_refs[d][e, :, sl] = oi.astype(BF16)
    for (e, d, h), gli, si, kdi, vi in zip(chains, gl, s, kd_t, v_new):
        s_ref[e, d, h] = jnp.exp(gli) * si + _dot(kdi, vi)


def _scan(qn, kn, vv, gates, s0, bb, state_only=False):
    b, t, _ = qn.shape
    c = GDN_CHUNK
    n = t // c
    assert b % bb == 0
    fwd = lambda width: pl.BlockSpec((bb, c, width), lambda i, j: (i, j, 0))
    bwd = lambda width: pl.BlockSpec((bb, c, width), lambda i, j: (i, n - 1 - j, 0))
    st = pl.BlockSpec((bb, 2, A_HEADS, A_DK, A_DV), lambda i, j: (i, 0, 0, 0, 0))
    o_specs = [] if state_only else [fwd(A_W), bwd(A_W)]
    o_shapes = [] if state_only else [jax.ShapeDtypeStruct((b, t, A_W), BF16)] * 2
    return pl.pallas_call(
        functools.partial(_scan_kernel, c=c, bb=bb),
        grid=(b // bb, n),
        in_specs=[fwd(A_QK), fwd(A_QK), fwd(A_W), fwd(LANE), bwd(A_QK), bwd(A_QK), bwd(A_W), bwd(LANE), st],
        out_specs=o_specs + [st],
        out_shape=o_shapes + [jax.ShapeDtypeStruct(s0.shape, F32)],
        compiler_params=_params(("parallel", "arbitrary")),
        name="gdn_scan",
    )(qn, kn, vv, gates, qn, kn, vv, gates, s0)


def _attention_tile(q_ref, kv_refs, szb_ref):
    n_src = len(kv_refs) // 2
    heads = range(B_HEADS)
    qh = [q_ref[0, :, h * LANE:(h + 1) * LANE] for h in heads]
    blocks = []
    for i in range(n_src):
        tk = kv_refs[2 * i].shape[1]
        kb = min(ATTN_KEY_BLOCK, tk)
        blocks += [(kv_refs[2 * i], kv_refs[2 * i + 1], slice(j * kb, (j + 1) * kb)) for j in range(tk // kb)]

    groups = [tuple(range(g, g + ATTN_HEAD_GROUP)) for g in range(0, B_HEADS, ATTN_HEAD_GROUP)]
    units = [(blk, grp) for blk in blocks for grp in groups]

    def scores(unit):
        (k_ref, _, ks), grp = unit
        return [_dot_nt(k_ref[0, ks, h * LANE:(h + 1) * LANE], qh[h]) for h in grp]

    m = [None] * B_HEADS
    acc = [None] * B_HEADS
    s_next = scores(units[0])
    for n, ((_, vt_ref, ks), grp) in enumerate(units):
        s, s_next = s_next, (scores(units[n + 1]) if n + 1 < len(units) else None)
        mb = [si.max(axis=0, keepdims=True) for si in s]
        m_new = [mbi if m[h] is None else jnp.maximum(m[h], mbi) for h, mbi in zip(grp, mb)]
        p = [jnp.exp2(si - mn).astype(BF16) for si, mn in zip(s, m_new)]
        pv = [_dot(vt_ref[0, h * B_VA:(h + 1) * B_VA, ks], pi) for h, pi in zip(grp, p)]
        for h, mn, pvi in zip(grp, m_new, pv):
            acc[h] = pvi if m[h] is None else jnp.exp2(m[h] - mn) * acc[h] + pvi
            m[h] = mn
    o = jnp.concatenate([a[:B_V] / a[B_V:B_V + 1] for a in acc], axis=0).T
    return (o * szb_ref[0].astype(F32)).astype(BF16)


def _attn_out_kernel(*refs, n_src, final):
    q_ref = refs[0]
    kv_refs = refs[1:1 + 2 * n_src]
    (szb_ref, x_ref, of_ref, ob_ref, sza_ref, cx_ref, gng_ref, mod_ref, w_ref, fg_ref, o_ref) = refs[1 + 2 * n_src:]
    bx = _attention_tile(q_ref, kv_refs, szb_ref)
    o = of_ref[0].astype(F32) + ob_ref[0].astype(F32)
    sza = sza_ref[0].astype(F32)
    acc = None
    for h in range(A_HEADS):
        sl = slice(h * LANE, (h + 1) * LANE)
        oh = o[:, sl]
        ah = oh * jax.lax.rsqrt(jnp.mean(oh * oh, axis=-1, keepdims=True) + EPS) * gng_ref[...]
        term = _dot((ah * sza[:, sl]).astype(BF16), w_ref[sl, :])
        acc = term if acc is None else acc + term
    acc = acc + _dot(bx, w_ref[A_W:A_W + B_W, :]) + _dot(cx_ref[0], w_ref[A_W + B_W:, :])
    xn = x_ref[0] + mod_ref[0, 2:3, :] * acc
    if final:
        xn = xn * jax.lax.rsqrt(jnp.mean(xn * xn, axis=-1, keepdims=True) + EPS) * fg_ref[...]
    o_ref[0] = xn


def _attn_out(x, q, kvs, szb, of, ob, sza, cx, gng, mod, per_batch, layer, w, fg, final, tm):
    b, t, d = x.shape
    tm = min(tm, t)
    bm = (lambda i, j: (layer, i, 0, 0)) if per_batch else (lambda i, j: (layer, mod.shape[1] - 1, 0, 0))
    tok = lambda width: pl.BlockSpec((1, tm, width), lambda i, j: (i, j, 0))
    full = lambda a: pl.BlockSpec(a.shape, lambda i, j: (0,) * a.ndim)
    in_specs = [tok(B_HEADS * LANE)]
    args = [q]
    for k, v in kvs:
        in_specs.append(pl.BlockSpec((1,) + k.shape[1:], lambda i, j: (i, 0, 0)))
        in_specs.append(pl.BlockSpec((1,) + v.shape[1:], lambda i, j: (i, 0, 0)))
        args += [k, v]
    in_specs += [tok(B_W), tok(d), tok(A_W), tok(A_W), tok(A_W), tok(C_W), _layer_spec(gng, layer),
                 pl.BlockSpec((None, 1, 3, d), bm), _layer_spec(w, layer), full(fg)]
    args += [szb, x, of, ob, sza, cx, gng, mod, w, fg]
    return pl.pallas_call(
        functools.partial(_attn_out_kernel, n_src=len(kvs), final=final),
        grid=(b, t // tm),
        in_specs=in_specs,
        out_specs=tok(d),
        out_shape=jax.ShapeDtypeStruct((b, t, d), F32),
        compiler_params=_params(("parallel", "parallel")),
        name="attn_out",
    )(*args)


def _deinterleave(w32):
    even, odd = w32[..., 0::2], w32[..., 1::2]
    return jnp.concatenate([even, odd], axis=-1), jnp.concatenate([odd, even], axis=-1)


def _w_layout_kernel(wt_ref, pm_ref, ps_ref, head_ref, tail_ref):
    wt = wt_ref[0]
    tk = wt.shape[1]
    head_ref[0] = wt[:W_HEAD].T.astype(BF16)
    kpe = wt[O_KV + B_KV_LORA:O_ZB].astype(BF16)
    kp_main = _dot(pm_ref[...], kpe)
    kp_swap = _dot(ps_ref[...], kpe)
    z = lambda n: jnp.zeros((n, tk), F32)
    rows = [wt[O_ZB:O_END],
            wt[O_QA:O_KV], kp_swap, z(2 * LANE - B_Q_LORA - B_ROPE),
            wt[O_KV:O_KV + B_KV_LORA], wt[O_B:O_QA], z(B_NOPE - 4 * A_HEADS), kp_main,
            z(LANE - B_NOPE - B_ROPE)]
    tail_ref[0] = jnp.concatenate(rows, axis=0).T.astype(BF16)


def _layout_w_in(w):
    depth, d, n_in = w.shape
    tk = 256
    src_main = np.concatenate([np.arange(0, B_ROPE, 2), np.arange(1, B_ROPE, 2)])
    src_swap = np.concatenate([np.arange(1, B_ROPE, 2), np.arange(0, B_ROPE, 2)])
    sel = lambda src: jnp.asarray(np.eye(B_ROPE, dtype=np.float32)[src], dtype=BF16)
    return pl.pallas_call(
        _w_layout_kernel,
        grid=(depth, d // tk),
        in_specs=[pl.BlockSpec((1, n_in, tk), lambda l, i: (l, 0, i)),
                  pl.BlockSpec((B_ROPE, B_ROPE), lambda l, i: (0, 0)),
                  pl.BlockSpec((B_ROPE, B_ROPE), lambda l, i: (0, 0))],
        out_specs=[pl.BlockSpec((1, tk, W_HEAD), lambda l, i: (l, i, 0)),
                   pl.BlockSpec((1, tk, W_TAIL), lambda l, i: (l, i, 0))],
        out_shape=[jax.ShapeDtypeStruct((depth, d, W_HEAD), BF16), jax.ShapeDtypeStruct((depth, d, W_TAIL), BF16)],
        compiler_params=_params(("parallel", "parallel")),
        name="w_layout",
    )(jnp.swapaxes(w, 1, 2), sel(src_main), sel(src_swap))


def _layout_w_qb(w):
    w = w.astype(BF16)
    z = lambda n: jnp.zeros(w.shape[:-1] + (n,), w.dtype)
    hd = B_NOPE + B_ROPE
    main, swap = [], []
    for h in range(B_HEADS):
        rm, rs = _deinterleave(w[..., h * hd + B_NOPE:(h + 1) * hd])
        main += [w[..., h * hd:h * hd + B_NOPE], rm, z(LANE - hd)]
        swap += [z(B_NOPE), rs, z(LANE - hd)]
    return jnp.concatenate(main, axis=-1), jnp.concatenate(swap, axis=-1)


def _layout_w_kvb(w):
    w = w.astype(BF16)
    hd = B_NOPE + B_V
    wk, wv = [], []
    for h in range(B_HEADS):
        wk += [w[..., h * hd:h * hd + B_NOPE], jnp.zeros(w.shape[:-1] + (LANE - B_NOPE,), w.dtype)]
        wv += [w[..., h * hd + B_NOPE:(h + 1) * hd], jnp.zeros(w.shape[:-1] + (B_VA - B_V,), w.dtype)]
    return jnp.concatenate(wk, axis=-1), jnp.swapaxes(jnp.concatenate(wv, axis=-1), -1, -2)


def _rope_tables(t, rotate):
    half = B_ROPE // 2
    if rotate:
        pos = np.arange(t)
        n_freq = B_ROPE // 4
        inv_freq = ROPE_THETA ** (-np.arange(n_freq, dtype=np.float64) / n_freq)
        ang = np.concatenate([(pos // GRID_W)[:, None] * inv_freq, (pos % GRID_W)[:, None] * inv_freq], axis=-1)
        cos, sin = np.cos(ang).astype(np.float32), np.sin(ang).astype(np.float32)
    else:
        cos, sin = np.ones((t, half), np.float32), np.zeros((t, half), np.float32)
    z = lambda n: np.zeros((t, n), np.float32)
    tail = LANE - B_NOPE - B_ROPE
    scale = np.float32((B_NOPE + B_ROPE) ** -0.5 * math.log2(math.e))
    cq = np.concatenate([np.ones((t, B_NOPE), np.float32), cos, cos, z(tail)], axis=1) * scale
    sq = np.concatenate([z(B_NOPE), -sin, sin, z(tail)], axis=1) * scale
    ck = np.concatenate([z(B_NOPE), cos, cos, z(tail)], axis=1)
    sk = np.concatenate([z(B_NOPE), -sin, sin, z(tail)], axis=1)
    return jnp.asarray(np.concatenate([cq, sq, ck, sk], axis=1), dtype=F32)


def kernel(x, c, ctx, c_ctx, w_ada, b_ada, norm_g, w_in, gdn_conv, gdn_a_log, gdn_dt_bias, gdn_norm_g,
           mla_q_norm_g, mla_w_qb, mla_kv_norm_g, mla_w_kvb, conv_w, w_out, final_norm_g):
    bsz, t, d = x.shape
    t_ctx = ctx.shape[1]
    depth = w_ada.shape[0]
    assert t % GDN_CHUNK == 0 and t_ctx % GDN_CHUNK == 0 and t % GRID_W == 0
    tm = 512
    bb = 2 if bsz % 2 == 0 else 1

    rows = -(-(bsz + 1) // 8) * 8
    cc = jnp.concatenate([c, jnp.zeros((rows - bsz - 1, d), F32), c_ctx[None, :]], axis=0)
    mod = _ada(cc, w_ada, b_ada).reshape(depth, rows, 3, d)

    tab_x = _rope_tables(t, True)
    tab_c = _rope_tables(t_ctx, False)
    fg = final_norm_g.reshape(1, d)
    w_all, wt_all = _layout_w_in(w_in)
    wqm_all, wqs_all = _layout_w_qb(mla_w_qb)
    wk_all, wv_all = _layout_w_kvb(mla_w_kvb)
    wo_all = w_out.astype(BF16)
    ng = norm_g.reshape(depth, 1, d)
    qng = mla_q_norm_g.reshape(depth, 1, B_Q_LORA)
    kvng = mla_kv_norm_g.reshape(depth, 1, B_KV_LORA)
    gng = gdn_norm_g.reshape(depth, 1, A_DV)
    gsc = jnp.pad(jnp.stack([gdn_a_log.reshape(depth, -1), gdn_dt_bias.reshape(depth, -1)], axis=1),
                  ((0, 0), (0, 6), (2 * A_HEADS, LANE - 4 * A_HEADS)))
    tap_pad = (A_CONV - C_CONV) // 2
    wconv = jnp.pad(jnp.concatenate([gdn_conv, jnp.pad(conv_w, ((0, 0), (tap_pad, tap_pad), (0, 0)))], axis=2),
                    ((0, 0), (0, 8 - A_CONV), (0, 0)))
    s_zero = jnp.zeros((bsz, 2, A_HEADS, A_DK, A_DV), F32)

    for l in range(depth):
        last = l == depth - 1
        shared = (l, ng, w_all, wt_all, qng, wqm_all, wqs_all, kvng, wk_all, wv_all)
        if last:
            qn_c, kn_c, vv_c, gates_c, km_c, vm_c = _inproj(
                ctx, mod, False, *shared, tab_c, gsc, wconv, tm, mixer_outputs=False)
        else:
            (qn_c, kn_c, vv_c, cx_c, sza_c, gates_c, qm_c, km_c, vm_c, szb_c) = _inproj(
                ctx, mod, False, *shared, tab_c, gsc, wconv, tm)
        (qn_x, kn_x, vv_x, cx_x, sza_x, gates_x, qm_x, km_x, vm_x, szb_x) = _inproj(
            x, mod, True, *shared, tab_x, gsc, wconv, tm)

        if last:
            (s_ctx,) = _scan(qn_c, kn_c, vv_c, gates_c, s_zero, bb, state_only=True)
        else:
            of_c, ob_c, s_ctx = _scan(qn_c, kn_c, vv_c, gates_c, s_zero, bb)
        of_x, ob_x, _ = _scan(qn_x, kn_x, vv_x, gates_x, s_ctx, bb)

        x_new = _attn_out(x, qm_x, [(km_x, vm_x), (km_c, vm_c)], szb_x, of_x, ob_x, sza_x, cx_x, gng, mod, True,
                          l, wo_all, fg, last, tm)
        if not last:
            ctx = _attn_out(ctx, qm_c, [(km_c, vm_c)], szb_c, of_c, ob_c, sza_c, cx_c, gng, mod, False,
                            l, wo_all, fg, False, tm)
        x = x_new
    return x
```

```python
import functools
import math

import jax
import jax.numpy as jnp
import numpy as np
from jax.experimental import pallas as pl
from jax.experimental.pallas import tpu as pltpu

F32 = jnp.float32
BF16 = jnp.bfloat16

EPS = 1e-6
GRID_W = 64
ROPE_THETA = 10000.0

A_HEADS = 4
A_DK = 128
A_DV = 128
A_QK = A_HEADS * A_DK
A_W = A_HEADS * A_DV
A_CONV = 5
B_HEADS = 4
B_NOPE = 64
B_ROPE = 32
B_V = 64
B_Q_LORA = 192
B_KV_LORA = 128
B_W = B_HEADS * B_V
B_VA = B_V + 16
B_WA = B_HEADS * B_VA
C_W = 256
C_CONV = 3

LANE = 128
HALO = 16
GDN_CHUNK = 128
ATTN_KEY_BLOCK = 256
ATTN_HEAD_GROUP = 4
VMEM_LIMIT = 48 * 1024 * 1024

O_B = 3 * A_QK + A_W
O_QA = O_B + 4 * A_HEADS
O_KV = O_QA + B_Q_LORA
O_ZB = O_KV + B_KV_LORA + B_ROPE
O_END = O_ZB + B_W + 4 * C_W

W_HEAD = O_B
S_ZA = 3 * A_QK
T_ZB = 0
T_H = T_ZB + B_W
T_BG = T_H + C_W
T_CG = T_BG + C_W
T_ZC = T_CG + C_W
T_QA = T_ZC + C_W
T_CKV = T_QA + 2 * LANE
W_TAIL = T_CKV + 2 * LANE
CIN_W = 3 * A_QK + C_W


def _dot(a, b):
    return jnp.dot(a, b, preferred_element_type=F32)


def _dot_nt(a, b):
    return jax.lax.dot_general(a, b, (((1,), (1,)), ((), ())), preferred_element_type=F32)


def _silu(x):
    h = 0.5 * x
    return h + h * jnp.tanh(h)


def _params(sem):
    return pltpu.CompilerParams(dimension_semantics=sem, vmem_limit_bytes=VMEM_LIMIT)


def _ada_kernel(c_ref, w_ref, b_ref, o_ref):
    s = _silu(c_ref[...]).astype(BF16)
    o_ref[0] = _dot(s, w_ref[0].astype(BF16)) + b_ref[0]


def _ada(cc, w_ada, b_ada):
    depth, d, n3 = w_ada.shape
    r = cc.shape[0]
    tn = 768
    return pl.pallas_call(
        _ada_kernel,
        grid=(depth, n3 // tn),
        in_specs=[pl.BlockSpec((r, d), lambda l, j: (0, 0)),
                  pl.BlockSpec((1, d, tn), lambda l, j: (l, 0, j)),
                  pl.BlockSpec((1, 1, tn), lambda l, j: (l, 0, j))],
        out_specs=pl.BlockSpec((1, r, tn), lambda l, j: (l, 0, j)),
        out_shape=jax.ShapeDtypeStruct((depth, r, n3), F32),
        compiler_params=_params(("parallel", "parallel")),
        name="ada",
    )(cc, w_ada, b_ada.reshape(depth, 1, n3))


def _inproj_kernel(x_ref, xp_ref, xn_ref, mod_ref, ng_ref, w_ref, wt_ref, qng_ref, wqm_ref, wqs_ref, kvng_ref,
                   wk_ref, wv_ref, tab_ref, gsc_ref, wc_ref, *out_refs, tm, mixer_outputs):
    if mixer_outputs:
        qn_ref, kn_ref, vv_ref, cx_ref, sza_ref, gates_ref, qm_ref, km_ref, vm_ref, szb_ref = out_refs
    else:
        qn_ref, kn_ref, vv_ref, gates_ref, km_ref, vm_ref = out_refs
    j = pl.program_id(1)
    nj = pl.num_programs(1)

    def normed(xv):
        y = xv * jax.lax.rsqrt(jnp.mean(xv * xv, axis=-1, keepdims=True) + EPS) * ng_ref[...]
        return y * (1.0 + mod_ref[0, 1:2, :]) + mod_ref[0, 0:1, :]

    hn = normed(x_ref[0]).astype(BF16)
    h_prev = jnp.where(j > 0, normed(xp_ref[0]), 0.0).astype(BF16)
    h_next = jnp.where(j < nj - 1, normed(xn_ref[0]), 0.0).astype(BF16)
    h_ext = jnp.concatenate([h_prev, hn, h_next], axis=0)

    def tail(a, b):
        return _dot(hn, wt_ref[:, a:b])

    def conv(u_ext, c0, c1, taps=A_CONV):
        pad = A_CONV // 2
        rows = u_ext.shape[0]
        acc = None
        for k in range(pad - taps // 2, pad + taps // 2 + 1):
            shifted = u_ext if k == pad else pltpu.roll(u_ext, (pad - k) % rows, 0)
            term = shifted[HALO:HALO + tm, :] * wc_ref[k:k + 1, c0:c1]
            acc = term if acc is None else acc + term
        return acc

    blk = 2 * LANE
    for i in range(3 * A_QK // blk):
        y2 = _silu(conv(_dot(h_ext, w_ref[:, i * blk:(i + 1) * blk]), i * blk, (i + 1) * blk))
        for hh in range(2):
            h = 2 * i + hh
            y = y2[:, hh * LANE:(hh + 1) * LANE]
            if h < 2 * A_HEADS:
                inv = jax.lax.rsqrt(jnp.sum(y * y, axis=-1, keepdims=True) + EPS)
                y = y * (inv * (A_DK ** -0.5) if h < A_HEADS else inv)
            if h < A_HEADS:
                qn_ref[0, :, h * LANE:(h + 1) * LANE] = y.astype(BF16)
            elif h < 2 * A_HEADS:
                kn_ref[0, :, (h - A_HEADS) * LANE:(h - A_HEADS + 1) * LANE] = y.astype(BF16)
            else:
                vv_ref[0, :, (h - 2 * A_HEADS) * LANE:(h - 2 * A_HEADS + 1) * LANE] = y.astype(BF16)

    tab = tab_ref[...]
    cq, sq = tab[:, 0:LANE], tab[:, LANE:2 * LANE]
    ck, sk = tab[:, 2 * LANE:3 * LANE], tab[:, 3 * LANE:4 * LANE]
    qa_blk = tail(T_QA, T_CKV)

    if mixer_outputs:
        hc_ext = _dot(h_ext, wt_ref[:, T_CG:T_ZC]) * _dot(h_ext, wt_ref[:, T_H:T_BG])
        cx_ref[0] = (tail(T_BG, T_CG) * _silu(tail(T_ZC, T_QA))
                     * conv(hc_ext, 3 * A_QK, CIN_W, C_CONV)).astype(BF16)

        sza_ref[0] = _silu(_dot(hn, w_ref[:, S_ZA:W_HEAD])).astype(BF16)
        szb_ref[0] = _silu(tail(T_ZB, T_H)).astype(BF16)

        qa = qa_blk[:, :B_Q_LORA]
        nq = (qa * jax.lax.rsqrt(jnp.mean(qa * qa, axis=-1, keepdims=True) + EPS) * qng_ref[...]).astype(BF16)
        qmain = _dot(nq, wqm_ref[...])
        qswap = _dot(nq, wqs_ref[...])
        for h in range(B_HEADS):
            sl = slice(h * LANE, (h + 1) * LANE)
            qm_ref[0, :, sl] = (qmain[:, sl] * cq + qswap[:, sl] * sq).astype(BF16)

    ckv_blk = tail(T_CKV, W_TAIL)
    ckv = ckv_blk[:, :B_KV_LORA]
    nkv = (ckv * jax.lax.rsqrt(jnp.mean(ckv * ckv, axis=-1, keepdims=True) + EPS) * kvng_ref[...]).astype(BF16)
    kk = _dot(nkv, wk_ref[...])
    vt = _dot_nt(wv_ref[...], nkv)
    vrow = jax.lax.broadcasted_iota(jnp.int32, vt.shape, 0)
    ones_row = functools.reduce(jnp.logical_or, [vrow == h * B_VA + B_V for h in range(B_HEADS)])
    vm_ref[0] = jnp.where(ones_row, 1.0, vt).astype(BF16)
    kpm = ckv_blk[:, B_KV_LORA:]
    kpe = kpm * ck + qa_blk[:, LANE:] * sk
    for h in range(B_HEADS):
        sl = slice(h * LANE, (h + 1) * LANE)
        km_ref[0, :, sl] = (kk[:, sl] + kpe).astype(BF16)

    lane = jax.lax.broadcasted_iota(jnp.int32, kpm.shape, 1)
    z = kpm + gsc_ref[1:2, :]
    sp = jnp.maximum(z, 0.0) + jnp.log1p(jnp.exp(-jnp.abs(z)))
    g = -jnp.exp(gsc_ref[0:1, :]) * sp
    gates_ref[0] = jnp.where(lane < 2 * A_HEADS, jax.nn.sigmoid(kpm), jnp.where(lane < 4 * A_HEADS, g, 0.0))


def _layer_spec(a, layer):
    return pl.BlockSpec((None,) + a.shape[1:], lambda i, j: (layer,) + (0,) * (a.ndim - 1))


def _inproj(x, mod, per_batch, layer, ng, w, wt, qng, wqm, wqs, kvng, wk, wv, tab, gsc, wconv, tm,
            mixer_outputs=True):
    b, t, d = x.shape
    tm = min(tm, t)
    r = tm // HALO
    nh = t // HALO
    bm = (lambda i, j: (layer, i, 0, 0)) if per_batch else (lambda i, j: (layer, mod.shape[1] - 1, 0, 0))
    tok = lambda width: pl.BlockSpec((1, tm, width), lambda i, j: (i, j, 0))
    full = lambda a: pl.BlockSpec(a.shape, lambda i, j: (0,) * a.ndim)
    lay = lambda a: _layer_spec(a, layer)
    outs = [(A_QK, BF16, True), (A_QK, BF16, True), (A_W, BF16, True), (C_W, BF16, False), (A_W, BF16, False),
            (LANE, F32, True), (B_HEADS * LANE, BF16, False), (B_HEADS * LANE, BF16, True), (None, BF16, True),
            (B_W, BF16, False)]
    outs = [o for o in outs if mixer_outputs or o[2]]
    widths = [o[0] for o in outs]
    dtypes = [o[1] for o in outs]
    V_OUT = widths.index(None)
    return pl.pallas_call(
        functools.partial(_inproj_kernel, tm=tm, mixer_outputs=mixer_outputs),
        grid=(b, t // tm),
        in_specs=[tok(d),
                  pl.BlockSpec((1, HALO, d), lambda i, j: (i, jnp.maximum(j * r - 1, 0), 0)),
                  pl.BlockSpec((1, HALO, d), lambda i, j: (i, jnp.minimum((j + 1) * r, nh - 1), 0)),
                  pl.BlockSpec((None, 1, 3, d), bm), lay(ng),
                  pl.BlockSpec((None, d, W_HEAD), lambda i, j: (layer, 0, 0)), lay(wt), lay(qng), lay(wqm), lay(wqs),
                  lay(kvng), lay(wk), lay(wv),
                  pl.BlockSpec((tm, 4 * LANE), lambda i, j: (j, 0)), lay(gsc), lay(wconv)],
        out_specs=[pl.BlockSpec((1, B_WA, tm), lambda i, j: (i, 0, j)) if n == V_OUT else tok(wd)
                   for n, wd in enumerate(widths)],
        out_shape=[jax.ShapeDtypeStruct((b, B_WA, t) if n == V_OUT else (b, t, wd), dt)
                   for n, (wd, dt) in enumerate(zip(widths, dtypes))],
        compiler_params=_params(("parallel", "parallel")),
        name="inproj",
    )(x, x, x, mod, ng, w, wt, qng, wqm, wqs, kvng, wk, wv, tab, gsc, wconv)


INV_BASE_LOG2 = 3
def _unit_triangular_inverses(mats, c):
    row = jax.lax.broadcasted_iota(jnp.int32, (c, c), 0)
    col = jax.lax.broadcasted_iota(jnp.int32, (c, c), 1)
    same = lambda k: (row >> k) == (col >> k)
    eye = jnp.where(row == col, 1.0, 0.0)
    base = same(INV_BASE_LOG2)
    a0 = [jnp.where(base, a, 0.0) for a in mats]
    t = [eye - ai for ai in a0]
    apow = [ai.astype(BF16) for ai in a0]
    for _ in range(INV_BASE_LOG2 - 1):
        apow = [_dot(ap, ap).astype(BF16) for ap in apow]
        t = [ti + _dot(ti.astype(BF16), ap) for ti, ap in zip(t, apow)]
    for k in range(INV_BASE_LOG2, int(math.log2(c))):
        coupling = same(k + 1) & jnp.logical_not(same(k))
        e = [jnp.where(coupling, a, 0.0).astype(BF16) for a in mats]
        tb = [ti.astype(BF16) for ti in t]
        te = [_dot(tbi, ei).astype(BF16) for tbi, ei in zip(tb, e)]
        t = [ti - _dot(tei, tbi) for ti, tei, tbi in zip(t, te, tb)]
    return t


def _scan_kernel(qf_ref, kf_ref, vf_ref, gf_ref, qb_ref, kb_ref, vb_ref, gb_ref, s0_ref,
                 *out_refs, c, bb):
    s_ref = out_refs[-1]
    o_refs = out_refs[:-1]
    n = pl.program_id(1)

    @pl.when(n == 0)
    def _():
        s_ref[...] = s0_ref[...]

    row = jax.lax.broadcasted_iota(jnp.int32, (c, c), 0)
    col = jax.lax.broadcasted_iota(jnp.int32, (c, c), 1)
    dirs = ((qf_ref, kf_ref, vf_ref, gf_ref), (qb_ref, kb_ref, vb_ref, gb_ref))
    incl = (col <= row, col >= row)
    strict = (col < row, col > row)
    gates, gcb, gcb_t, gl_row = {}, {}, {}, {}
    for e in range(bb):
        for d in range(2):
            g = dirs[d][3][e]
            tri = jnp.where(incl[d], 1.0, 0.0).astype(BF16)
            hi = g.astype(BF16)
            r1 = g - hi.astype(F32)
            mid = r1.astype(BF16)
            lo = (r1 - mid.astype(F32)).astype(BF16)
            gc_all = _dot(tri, hi) + _dot(tri, mid) + _dot(tri, lo)
            gates[e, d] = g
            gcb[e, d] = gc_all
            gcb_t[e, d] = gc_all.T
            gl_row[e, d] = gc_all[c - 1:c, :] if d == 0 else gc_all[0:1, :]

    chains = [(e, d, h) for e in range(bb) for d in range(2) for h in range(A_HEADS)]
    sls = [slice(h * LANE, (h + 1) * LANE) for _, _, h in chains]
    jbs = [A_HEADS * d + h for _, d, h in chains]
    jgs = [2 * A_HEADS + jb for jb in jbs]
    beta = [gates[e, d][:, jb:jb + 1] for (e, d, _), jb in zip(chains, jbs)]
    gc = [gcb[e, d][:, jg:jg + 1] for (e, d, _), jg in zip(chains, jgs)]
    gcr = [gcb_t[e, d][jg:jg + 1, :] for (e, d, _), jg in zip(chains, jgs)]
    gl = [gl_row[e, d][:, jg:jg + 1] for (e, d, _), jg in zip(chains, jgs)]
    q = [dirs[d][0][e, :, sl] for (e, d, _), sl in zip(chains, sls)]
    k = [dirs[d][1][e, :, sl] for (e, d, _), sl in zip(chains, sls)]
    v = [dirs[d][2][e, :, sl] for (e, d, _), sl in zip(chains, sls)]
    decay = [jnp.where(incl[d], jnp.exp(jnp.where(incl[d], gci - gcri, 0.0)), 0.0)
             for (_, d, _), gci, gcri in zip(chains, gc, gcr)]
    kbeta = [ki.astype(F32) * bi for ki, bi in zip(k, beta)]
    egc = [jnp.exp(gci) for gci in gc]
    a = [jnp.where(strict[d], _dot_nt(kbi.astype(BF16), ki) * di, 0.0)
         for (_, d, _), kbi, ki, di in zip(chains, kbeta, k, decay)]
    qk = [(_dot_nt(qi, ki) * di).astype(BF16) for qi, ki, di in zip(q, k, decay)]
    tinv = [ti.astype(BF16) for ti in _unit_triangular_inverses(a, c)]
    kd_t = [(ki.astype(F32) * jnp.exp(gli - gci)).T.astype(BF16) for ki, gli, gci in zip(k, gl, gc)]
    kbe = [(kbi * ei).astype(BF16) for kbi, ei in zip(kbeta, egc)]
    vbeta = [vi.astype(F32) * bi for vi, bi in zip(v, beta)]
    qd = [(qi.astype(F32) * ei).astype(BF16) for qi, ei in zip(q, egc)]
    s = [s_ref[e, d, h] for e, d, h in chains]
    sb = [si.astype(BF16) for si in s]
    rhs = [(vbi - _dot(kbi, sbi)).astype(BF16) for vbi, kbi, sbi in zip(vbeta, kbe, sb)]
    v_new = [_dot(ti, ri).astype(BF16) for ti, ri in zip(tinv, rhs)]
    if o_refs:
        o = [_dot(qdi, sbi) + _dot(qki, vi) for qdi, sbi, qki, vi in zip(qd, sb, qk, v_new)]
        for (e, d, h), sl, oi in zip(chains, sls, o):
            o_refs[d][e, :, sl] = oi.astype(BF16)
    for (e, d, h), gli, si, kdi, vi in zip(chains, gl, s, kd_t, v_new):
        s_ref[e, d, h] = jnp.exp(gli) * si + _dot(kdi, vi)


def _scan(qn, kn, vv, gates, s0, bb, state_only=False):
    b, t, _ = qn.shape
    c = GDN_CHUNK
    n = t // c
    assert b % bb == 0
    fwd = lambda width: pl.BlockSpec((bb, c, width), lambda i, j: (i, j, 0))
    bwd = lambda width: pl.BlockSpec((bb, c, width), lambda i, j: (i, n - 1 - j, 0))
    st = pl.BlockSpec((bb, 2, A_HEADS, A_DK, A_DV), lambda i, j: (i, 0, 0, 0, 0))
    o_specs = [] if state_only else [fwd(A_W), bwd(A_W)]
    o_shapes = [] if state_only else [jax.ShapeDtypeStruct((b, t, A_W), BF16)] * 2
    return pl.pallas_call(
        functools.partial(_scan_kernel, c=c, bb=bb),
        grid=(b // bb, n),
        in_specs=[fwd(A_QK), fwd(A_QK), fwd(A_W), fwd(LANE), bwd(A_QK), bwd(A_QK), bwd(A_W), bwd(LANE), st],
        out_specs=o_specs + [st],
        out_shape=o_shapes + [jax.ShapeDtypeStruct(s0.shape, F32)],
        compiler_params=_params(("parallel", "arbitrary")),
        name="gdn_scan",
    )(qn, kn, vv, gates, qn, kn, vv, gates, s0)


def _attention_tile(q_ref, kv_refs, szb_ref):
    n_src = len(kv_refs) // 2
    heads = range(B_HEADS)
    qh = [q_ref[0, :, h * LANE:(h + 1) * LANE] for h in heads]
    blocks = []
    for i in range(n_src):
        tk = kv_refs[2 * i].shape[1]
        kb = min(ATTN_KEY_BLOCK, tk)
        blocks += [(kv_refs[2 * i], kv_refs[2 * i + 1], slice(j * kb, (j + 1) * kb)) for j in range(tk // kb)]

    groups = [tuple(range(g, g + ATTN_HEAD_GROUP)) for g in range(0, B_HEADS, ATTN_HEAD_GROUP)]
    units = [(blk, grp) for blk in blocks for grp in groups]

    def scores(unit):
        (k_ref, _, ks), grp = unit
        return [_dot_nt(k_ref[0, ks, h * LANE:(h + 1) * LANE], qh[h]) for h in grp]

    m = [None] * B_HEADS
    acc = [None] * B_HEADS
    s_next = scores(units[0])
    for n, ((_, vt_ref, ks), grp) in enumerate(units):
        s, s_next = s_next, (scores(units[n + 1]) if n + 1 < len(units) else None)
        mb = [si.max(axis=0, keepdims=True) for si in s]
        m_new = [mbi if m[h] is None else jnp.maximum(m[h], mbi) for h, mbi in zip(grp, mb)]
        p = [jnp.exp2(si - mn).astype(BF16) for si, mn in zip(s, m_new)]
        pv = [_dot(vt_ref[0, h * B_VA:(h + 1) * B_VA, ks], pi) for h, pi in zip(grp, p)]
        for h, mn, pvi in zip(grp, m_new, pv):
            acc[h] = pvi if m[h] is None else jnp.exp2(m[h] - mn) * acc[h] + pvi
            m[h] = mn
    o = jnp.concatenate([a[:B_V] / a[B_V:B_V + 1] for a in acc], axis=0).T
    return (o * szb_ref[0].astype(F32)).astype(BF16)


def _attn_out_kernel(*refs, n_src, final):
    q_ref = refs[0]
    kv_refs = refs[1:1 + 2 * n_src]
    (szb_ref, x_ref, of_ref, ob_ref, sza_ref, cx_ref, gng_ref, mod_ref, w_ref, fg_ref, o_ref) = refs[1 + 2 * n_src:]
    bx = _attention_tile(q_ref, kv_refs, szb_ref)
    o = of_ref[0].astype(F32) + ob_ref[0].astype(F32)
    sza = sza_ref[0].astype(F32)
    acc = None
    for h in range(A_HEADS):
        sl = slice(h * LANE, (h + 1) * LANE)
        oh = o[:, sl]
        ah = oh * jax.lax.rsqrt(jnp.mean(oh * oh, axis=-1, keepdims=True) + EPS) * gng_ref[...]
        term = _dot((ah * sza[:, sl]).astype(BF16), w_ref[sl, :])
        acc = term if acc is None else acc + term
    acc = acc + _dot(bx, w_ref[A_W:A_W + B_W, :]) + _dot(cx_ref[0], w_ref[A_W + B_W:, :])
    xn = x_ref[0] + mod_ref[0, 2:3, :] * acc
    if final:
        xn = xn * jax.lax.rsqrt(jnp.mean(xn * xn, axis=-1, keepdims=True) + EPS) * fg_ref[...]
    o_ref[0] = xn


def _attn_out(x, q, kvs, szb, of, ob, sza, cx, gng, mod, per_batch, layer, w, fg, final, tm):
    b, t, d = x.shape
    tm = min(tm, t)
    bm = (lambda i, j: (layer, i, 0, 0)) if per_batch else (lambda i, j: (layer, mod.shape[1] - 1, 0, 0))
    tok = lambda width: pl.BlockSpec((1, tm, width), lambda i, j: (i, j, 0))
    full = lambda a: pl.BlockSpec(a.shape, lambda i, j: (0,) * a.ndim)
    in_specs = [tok(B_HEADS * LANE)]
    args = [q]
    for k, v in kvs:
        in_specs.append(pl.BlockSpec((1,) + k.shape[1:], lambda i, j: (i, 0, 0)))
        in_specs.append(pl.BlockSpec((1,) + v.shape[1:], lambda i, j: (i, 0, 0)))
        args += [k, v]
    in_specs += [tok(B_W), tok(d), tok(A_W), tok(A_W), tok(A_W), tok(C_W), _layer_spec(gng, layer),
                 pl.BlockSpec((None, 1, 3, d), bm), _layer_spec(w, layer), full(fg)]
    args += [szb, x, of, ob, sza, cx, gng, mod, w, fg]
    return pl.pallas_call(
        functools.partial(_attn_out_kernel, n_src=len(kvs), final=final),
        grid=(b, t // tm),
        in_specs=in_specs,
        out_specs=tok(d),
        out_shape=jax.ShapeDtypeStruct((b, t, d), F32),
        compiler_params=_params(("parallel", "parallel")),
        name="attn_out",
    )(*args)


def _deinterleave(w32):
    even, odd = w32[..., 0::2], w32[..., 1::2]
    return jnp.concatenate([even, odd], axis=-1), jnp.concatenate([odd, even], axis=-1)


def _w_layout_kernel(wt_ref, pm_ref, ps_ref, head_ref, tail_ref):
    wt = wt_ref[0]
    tk = wt.shape[1]
    head_ref[0] = wt[:W_HEAD].T.astype(BF16)
    kpe = wt[O_KV + B_KV_LORA:O_ZB].astype(BF16)
    kp_main = _dot(pm_ref[...], kpe)
    kp_swap = _dot(ps_ref[...], kpe)
    z = lambda n: jnp.zeros((n, tk), F32)
    rows = [wt[O_ZB:O_END],
            wt[O_QA:O_KV], kp_swap, z(2 * LANE - B_Q_LORA - B_ROPE),
            wt[O_KV:O_KV + B_KV_LORA], wt[O_B:O_QA], z(B_NOPE - 4 * A_HEADS), kp_main,
            z(LANE - B_NOPE - B_ROPE)]
    tail_ref[0] = jnp.concatenate(rows, axis=0).T.astype(BF16)


def _layout_w_in(w):
    depth, d, n_in = w.shape
    tk = 256
    src_main = np.concatenate([np.arange(0, B_ROPE, 2), np.arange(1, B_ROPE, 2)])
    src_swap = np.concatenate([np.arange(1, B_ROPE, 2), np.arange(0, B_ROPE, 2)])
    sel = lambda src: jnp.asarray(np.eye(B_ROPE, dtype=np.float32)[src], dtype=BF16)
    return pl.pallas_call(
        _w_layout_kernel,
        grid=(depth, d // tk),
        in_specs=[pl.BlockSpec((1, n_in, tk), lambda l, i: (l, 0, i)),
                  pl.BlockSpec((B_ROPE, B_ROPE), lambda l, i: (0, 0)),
                  pl.BlockSpec((B_ROPE, B_ROPE), lambda l, i: (0, 0))],
        out_specs=[pl.BlockSpec((1, tk, W_HEAD), lambda l, i: (l, i, 0)),
                   pl.BlockSpec((1, tk, W_TAIL), lambda l, i: (l, i, 0))],
        out_shape=[jax.ShapeDtypeStruct((depth, d, W_HEAD), BF16), jax.ShapeDtypeStruct((depth, d, W_TAIL), BF16)],
        compiler_params=_params(("parallel", "parallel")),
        name="w_layout",
    )(jnp.swapaxes(w, 1, 2), sel(src_main), sel(src_swap))


def _layout_w_qb(w):
    w = w.astype(BF16)
    z = lambda n: jnp.zeros(w.shape[:-1] + (n,), w.dtype)
    hd = B_NOPE + B_ROPE
    main, swap = [], []
    for h in range(B_HEADS):
        rm, rs = _deinterleave(w[..., h * hd + B_NOPE:(h + 1) * hd])
        main += [w[..., h * hd:h * hd + B_NOPE], rm, z(LANE - hd)]
        swap += [z(B_NOPE), rs, z(LANE - hd)]
    return jnp.concatenate(main, axis=-1), jnp.concatenate(swap, axis=-1)


def _layout_w_kvb(w):
    w = w.astype(BF16)
    hd = B_NOPE + B_V
    wk, wv = [], []
    for h in range(B_HEADS):
        wk += [w[..., h * hd:h * hd + B_NOPE], jnp.zeros(w.shape[:-1] + (LANE - B_NOPE,), w.dtype)]
        wv += [w[..., h * hd + B_NOPE:(h + 1) * hd], jnp.zeros(w.shape[:-1] + (B_VA - B_V,), w.dtype)]
    return jnp.concatenate(wk, axis=-1), jnp.swapaxes(jnp.concatenate(wv, axis=-1), -1, -2)


def _rope_tables(t, rotate):
    half = B_ROPE // 2
    if rotate:
        pos = np.arange(t)
        n_freq = B_ROPE // 4
        inv_freq = ROPE_THETA ** (-np.arange(n_freq, dtype=np.float64) / n_freq)
        ang = np.concatenate([(pos // GRID_W)[:, None] * inv_freq, (pos % GRID_W)[:, None] * inv_freq], axis=-1)
        cos, sin = np.cos(ang).astype(np.float32), np.sin(ang).astype(np.float32)
    else:
        cos, sin = np.ones((t, half), np.float32), np.zeros((t, half), np.float32)
    z = lambda n: np.zeros((t, n), np.float32)
    tail = LANE - B_NOPE - B_ROPE
    scale = np.float32((B_NOPE + B_ROPE) ** -0.5 * math.log2(math.e))
    cq = np.concatenate([np.ones((t, B_NOPE), np.float32), cos, cos, z(tail)], axis=1) * scale
    sq = np.concatenate([z(B_NOPE), -sin, sin, z(tail)], axis=1) * scale
    ck = np.concatenate([z(B_NOPE), cos, cos, z(tail)], axis=1)
    sk = np.concatenate([z(B_NOPE), -sin, sin, z(tail)], axis=1)
    return jnp.asarray(np.concatenate([cq, sq, ck, sk], axis=1), dtype=F32)


def kernel(x, c, ctx, c_ctx, w_ada, b_ada, norm_g, w_in, gdn_conv, gdn_a_log, gdn_dt_bias, gdn_norm_g,
           mla_q_norm_g, mla_w_qb, mla_kv_norm_g, mla_w_kvb, conv_w, w_out, final_norm_g):
    bsz, t, d = x.shape
    t_ctx = ctx.shape[1]
    depth = w_ada.shape[0]
    assert t % GDN_CHUNK == 0 and t_ctx % GDN_CHUNK == 0 and t % GRID_W == 0
    tm = 512
    bb = 2 if bsz % 2 == 0 else 1

    rows = -(-(bsz + 1) // 8) * 8
    cc = jnp.concatenate([c, jnp.zeros((rows - bsz - 1, d), F32), c_ctx[None, :]], axis=0)
    mod = _ada(cc, w_ada, b_ada).reshape(depth, rows, 3, d)

    tab_x = _rope_tables(t, True)
    tab_c = _rope_tables(t_ctx, False)
    fg = final_norm_g.reshape(1, d)
    w_all, wt_all = _layout_w_in(w_in)
    wqm_all, wqs_all = _layout_w_qb(mla_w_qb)
    wk_all, wv_all = _layout_w_kvb(mla_w_kvb)
    wo_all = w_out.astype(BF16)
    ng = norm_g.reshape(depth, 1, d)
    qng = mla_q_norm_g.reshape(depth, 1, B_Q_LORA)
    kvng = mla_kv_norm_g.reshape(depth, 1, B_KV_LORA)
    gng = gdn_norm_g.reshape(depth, 1, A_DV)
    gsc = jnp.pad(jnp.stack([gdn_a_log.reshape(depth, -1), gdn_dt_bias.reshape(depth, -1)], axis=1),
                  ((0, 0), (0, 6), (2 * A_HEADS, LANE - 4 * A_HEADS)))
    tap_pad = (A_CONV - C_CONV) // 2
    wconv = jnp.pad(jnp.concatenate([gdn_conv, jnp.pad(conv_w, ((0, 0), (tap_pad, tap_pad), (0, 0)))], axis=2),
                    ((0, 0), (0, 8 - A_CONV), (0, 0)))
    s_zero = jnp.zeros((bsz, 2, A_HEADS, A_DK, A_DV), F32)

    for l in range(depth):
        last = l == depth - 1
        shared = (l, ng, w_all, wt_all, qng, wqm_all, wqs_all, kvng, wk_all, wv_all)
        if last:
            qn_c, kn_c, vv_c, gates_c, km_c, vm_c = _inproj(
                ctx, mod, False, *shared, tab_c, gsc, wconv, tm, mixer_outputs=False)
        else:
            (qn_c, kn_c, vv_c, cx_c, sza_c, gates_c, qm_c, km_c, vm_c, szb_c) = _inproj(
                ctx, mod, False, *shared, tab_c, gsc, wconv, tm)
        (qn_x, kn_x, vv_x, cx_x, sza_x, gates_x, qm_x, km_x, vm_x, szb_x) = _inproj(
            x, mod, True, *shared, tab_x, gsc, wconv, 1024)

        if last:
            (s_ctx,) = _scan(qn_c, kn_c, vv_c, gates_c, s_zero, bb, state_only=True)
        else:
            of_c, ob_c, s_ctx = _scan(qn_c, kn_c, vv_c, gates_c, s_zero, bb)
        of_x, ob_x, _ = _scan(qn_x, kn_x, vv_x, gates_x, s_ctx, bb)

        x_new = _attn_out(x, qm_x, [(km_x, vm_x), (km_c, vm_c)], szb_x, of_x, ob_x, sza_x, cx_x, gng, mod, True,
                          l, wo_all, fg, last, tm)
        if not last:
            ctx = _attn_out(ctx, qm_c, [(km_c, vm_c)], szb_c, of_c, ob_c, sza_c, cx_c, gng, mod, False,
                            l, wo_all, fg, False, tm)
        x = x_new
    return x
```
